```python
import math
import jax
import jax.numpy as jnp
from jax import lax
import numpy as np

D_MODEL = 1024
BATCH = 4
SEQ = 8192
DEPTH = 4

GRID_W = 64
CTX_LEN = 256
N_EVEN = (DEPTH + 1) // 2
N_ODD = DEPTH // 2
EPS = 1e-6
F32 = jnp.float32

A_HEADS = 4
A_DK = 128
A_DV = 128
A_CONV = 4
A_CHUNK = 64
A_QKV = 2 * A_HEADS * A_DK + A_HEADS * A_DV
B_HEADS = 4
B_DK = 128
B_DV = 128
B_CHUNK = 64
C_HEADS = 8
C_KV_HEADS = 2
C_HD = 64
C_WIN = 128
C_BLOCK = 128
ROPE_THETA = 10000.0
D_WIDTH = 512
D_BLOCKS = 8
D_BW = D_WIDTH // D_BLOCKS
D_CONV = 4
D_C = 8.0
N_EXPERTS = 16
EXPERT_FF = 1024
EC_FACTOR = 2

EVEN_SIZES = (A_QKV, A_HEADS * A_DV, 2 * A_HEADS, 2 * A_HEADS,
              B_HEADS * B_DK, B_HEADS * B_DV, 2 * B_HEADS * B_DK, B_HEADS * B_DV)
EVEN_OUT = A_HEADS * A_DV + B_HEADS * B_DV
ODD_SIZES = (C_HEADS * C_HD, C_KV_HEADS * C_HD, C_KV_HEADS * C_HD, D_WIDTH, D_WIDTH)
ODD_OUT = C_HEADS * C_HD + D_WIDTH

kernel_name = "hybrid_deltanet_hgrn2_swa_rglru_ec_moe_dit"


def _cuts(sizes):
    out, acc = [], 0
    for s in sizes[:-1]:
        acc += s
        out.append(acc)
    return out


def rmsnorm(x, w):
    xf = x.astype(F32)
    y = xf * lax.rsqrt(jnp.mean(xf * xf, axis=-1, keepdims=True) + EPS)
    return (y * w).astype(x.dtype)


def _modulate(h, shift, scale):
    return h * (1 + scale) + shift


def _l2norm(t):
    return t * lax.rsqrt(jnp.sum(t * t, axis=-1, keepdims=True) + EPS)


def _flip(t):
    return jnp.flip(t, axis=1)


def dwconv(x, w):
    k = w.shape[0]
    return lax.conv_general_dilated(x, w[:, None, :].astype(x.dtype), (1,), [((k - 1) // 2, k // 2)],
                                    dimension_numbers=('NWC', 'WIO', 'NWC'),
                                    feature_group_count=x.shape[-1])


def _to_chunks(t, length):
    b_, n, h = t.shape[:3]
    rest = t.shape[3:]
    t = t.reshape((b_, n // length, length, h) + rest)
    return t.transpose((1, 0, 3, 2) + tuple(range(4, t.ndim)))


def _from_chunks(t):
    nc, b_, h, length, d = t.shape
    return t.transpose(1, 0, 3, 2, 4).reshape(b_, nc * length, h, d)


def gated_delta_chunked(q, k, v, g, beta, s0):
    length = A_CHUNK
    dv = v.shape[-1]
    q, k, v = _to_chunks(q, length), _to_chunks(k, length), _to_chunks(v, length)
    g, beta = _to_chunks(g, length), _to_chunks(beta, length)
    cum = jnp.cumsum(g, axis=-1)
    incl = jnp.tril(jnp.ones((length, length), bool))
    strict = jnp.tril(jnp.ones((length, length), bool), -1)
    decay = jnp.exp(jnp.where(incl, cum[..., :, None] - cum[..., None, :], -jnp.inf))
    kb = k * beta[..., None]
    m = jnp.where(strict, jnp.einsum('...id,...jd->...ij', kb, k) * decay, 0.0)
    rhs = jnp.concatenate([v * beta[..., None], kb * jnp.exp(cum)[..., None]], axis=-1)
    sol = lax.linalg.triangular_solve(m + jnp.eye(length, dtype=m.dtype), rhs, left_side=True, lower=True)
    u, w = sol[..., :dv], sol[..., dv:]
    qk = jnp.einsum('...id,...jd->...ij', q, k) * decay
    q_dec = q * jnp.exp(cum)[..., None]
    k_dec = k * jnp.exp(cum[..., -1:] - cum)[..., None]
    g_last = jnp.exp(cum[..., -1])

    def step(s, inp):
        u_c, w_c, qk_c, qd_c, kd_c, gl_c = inp
        v_new = u_c - jnp.einsum('bhld,bhdv->bhlv', w_c, s)
        o = jnp.einsum('bhld,bhdv->bhlv', qd_c, s) + jnp.einsum('bhlm,bhmv->bhlv', qk_c, v_new)
        s = s * gl_c[..., None, None] + jnp.einsum('bhld,bhlv->bhdv', kd_c, v_new)
        return s, o

    s_fin, o = lax.scan(step, s0, (u, w, qk, q_dec, k_dec, g_last))
    return _from_chunks(o), s_fin


def hgrn2_chunked(q, k, v, lf, s0):
    length = B_CHUNK
    q, k, v, lf = (_to_chunks(t, length) for t in (q, k, v, lf))
    cum = jnp.cumsum(lf, axis=-2)
    incl = jnp.tril(jnp.ones((length, length), bool))[:, :, None]

    def step(s, inp):
        q_c, k_c, v_c, b_c = inp
        dec = jnp.exp(jnp.where(incl, b_c[..., :, None, :] - b_c[..., None, :, :], -jnp.inf))
        att = jnp.einsum('bhid,bhjd,bhijd->bhij', q_c, k_c, dec)
        b_last = b_c[..., -1:, :]
        o = jnp.einsum('bhij,bhjv->bhiv', att, v_c) + jnp.einsum('bhid,bhdv->bhiv', q_c * jnp.exp(b_c), s)
        s = s * jnp.exp(b_last)[..., 0, :, None] + jnp.einsum('bhjd,bhjv->bhdv', k_c * jnp.exp(b_last - b_c), v_c)
        return s, o

    s_fin, o = lax.scan(step, s0, (q, k, v, cum))
    return _from_chunks(o), s_fin


def delta_group(qkv, gate, alpha, beta, conv_w, a_log, dt_bias, norm_w, s0):
    b_, n, _ = qkv.shape
    qkv = jax.nn.silu(dwconv(qkv, conv_w)).astype(F32)
    q, k, v = jnp.split(qkv, [A_HEADS * A_DK, 2 * A_HEADS * A_DK], axis=-1)
    q = _l2norm(q.reshape(b_, n, A_HEADS, A_DK)) * (A_DK ** -0.5)
    k = _l2norm(k.reshape(b_, n, A_HEADS, A_DK))
    v = v.reshape(b_, n, A_HEADS, A_DV)
    g = -jnp.exp(a_log) * jax.nn.softplus(alpha.astype(F32).reshape(b_, n, 2, A_HEADS) + dt_bias)
    bt = jax.nn.sigmoid(beta.astype(F32).reshape(b_, n, 2, A_HEADS))
    o_f, s_f = gated_delta_chunked(q, k, v, g[:, :, 0], bt[:, :, 0], s0[0])
    o_b, s_b = gated_delta_chunked(_flip(q), _flip(k), _flip(v), _flip(g[:, :, 1]), _flip(bt[:, :, 1]), s0[1])
    o = o_f + _flip(o_b)
    o = rmsnorm(o, norm_w) * jax.nn.silu(gate.astype(F32).reshape(b_, n, A_HEADS, A_DV))
    return o.reshape(b_, n, A_HEADS * A_DV), (s_f, s_b)


def hgrn2_group(q, i, f, gate, lb, norm_w, s0):
    b_, n, _ = q.shape
    q = jax.nn.silu(q.astype(F32)).reshape(b_, n, B_HEADS, B_DK)
    v = i.astype(F32).reshape(b_, n, B_HEADS, B_DV)
    fg = lb + (1.0 - lb) * jax.nn.sigmoid(f.astype(F32).reshape(b_, n, 2, B_HEADS, B_DK))
    lf = jnp.log(fg)
    k = 1.0 - fg
    o_f, s_f = hgrn2_chunked(q, k[:, :, 0], v, lf[:, :, 0], s0[0])
    o_b, s_b = hgrn2_chunked(_flip(q), _flip(k[:, :, 1]), _flip(v), _flip(lf[:, :, 1]), s0[1])
    o = o_f + _flip(o_b)
    o = rmsnorm(o, norm_w) * jax.nn.silu(gate.astype(F32).reshape(b_, n, B_HEADS, B_DV))
    return o.reshape(b_, n, B_HEADS * B_DV), (s_f, s_b)


def even_mixer(h_c, h_l, w_in, w_out, conv_w, a_log, dt_bias, a_norm_w, lb, b_norm_w, ctx_out):
    b_ = h_l.shape[0]
    zero_a = jnp.zeros((b_, A_HEADS, A_DK, A_DV), F32)
    zero_b = jnp.zeros((b_, B_HEADS, B_DK, B_DV), F32)

    def run(h, st_a, st_b):
        qkv, ga, al, be, qb, ib, fb, gb = jnp.split(h @ w_in, _cuts(EVEN_SIZES), axis=-1)
        oa, st_a = delta_group(qkv, ga, al, be, conv_w, a_log, dt_bias, a_norm_w, st_a)
        ob, st_b = hgrn2_group(qb, ib, fb, gb, lb, b_norm_w, st_b)
        return jnp.concatenate([oa, ob], axis=-1).astype(h.dtype), st_a, st_b

    o_c, st_a, st_b = run(h_c, (zero_a, zero_a), (zero_b, zero_b))
    o_l, _, _ = run(h_l, st_a, st_b)
    y_c = o_c @ w_out if ctx_out else None
    return y_c, o_l @ w_out


def rope_2d(t, row, col):
    half = t.shape[-1] // 2
    nf = half // 2
    inv = ROPE_THETA ** (-jnp.arange(nf, dtype=F32) / nf)

    def rot(u, pos):
        ang = pos.astype(F32)[:, None] * inv
        cos, sin = jnp.cos(ang)[None, :, None, :], jnp.sin(ang)[None, :, None, :]
        u1, u2 = u[..., :nf], u[..., nf:]
        return jnp.concatenate([u1 * cos - u2 * sin, u2 * cos + u1 * sin], axis=-1)

    return jnp.concatenate([rot(t[..., :half], row), rot(t[..., half:], col)], axis=-1)


def _sink_softmax(s, sink):
    m = jnp.maximum(jnp.max(s, axis=-1, keepdims=True), sink)
    e = jnp.exp(s - m)
    return e / (jnp.sum(e, axis=-1, keepdims=True) + jnp.exp(sink - m))


def window_attention(q_l, k_l, v_l, k_c, v_c, sink):
    b_, n, h, hd = q_l.shape
    grp = h // C_KV_HEADS
    nb = n // C_BLOCK
    qb = q_l.reshape(b_, nb, C_BLOCK, C_KV_HEADS, grp, hd) * (hd ** -0.5)

    def band(t):
        tp = jnp.pad(t, ((0, 0), (C_BLOCK, C_BLOCK), (0, 0), (0, 0))).reshape(b_, nb + 2, C_BLOCK, C_KV_HEADS, hd)
        return jnp.concatenate([tp[:, :-2], tp[:, 1:-1], tp[:, 2:]], axis=2)

    kb, vb = band(k_l), band(v_l)
    s_lat = jnp.einsum('bnqkgd,bnskd->bnkgqs', qb, kb)
    s_ctx = jnp.einsum('bnqkgd,bskd->bnkgqs', qb, k_c)
    qpos = jnp.arange(C_BLOCK)[:, None]
    kpos = jnp.arange(3 * C_BLOCK)[None, :] - C_BLOCK
    s_abs = jnp.arange(nb)[:, None, None] * C_BLOCK + kpos[None]
    mask = (jnp.abs(kpos - qpos) <= C_WIN)[None] & (s_abs >= 0) & (s_abs < n)
    s_lat = jnp.where(mask[None, :, None, None], s_lat, -jnp.inf)
    sink_b = sink.astype(F32).reshape(C_KV_HEADS, grp)[None, None, :, :, None, None]
    p = _sink_softmax(jnp.concatenate([s_lat, s_ctx], axis=-1), sink_b)
    o = (jnp.einsum('bnkgqs,bnskd->bnqkgd', p[..., :3 * C_BLOCK], vb)
         + jnp.einsum('bnkgqs,bskd->bnqkgd', p[..., 3 * C_BLOCK:], v_c))
    return o.reshape(b_, n, h * hd)


def context_attention(q_c, k_c, v_c, sink):
    b_, lc, h, hd = q_c.shape
    grp = h // C_KV_HEADS
    qg = q_c.reshape(b_, lc, C_KV_HEADS, grp, hd) * (hd ** -0.5)
    s = jnp.einsum('bqkgd,bskd->bkgqs', qg, k_c)
    p = _sink_softmax(s, sink.astype(F32).reshape(C_KV_HEADS, grp)[None, :, :, None, None])
    return jnp.einsum('bkgqs,bskd->bqkgd', p, v_c).reshape(b_, lc, h * hd)


def _linear_scan(a, u, h0):
    u = u.at[:, 0].add(a[:, 0] * h0)
    _, h = lax.associative_scan(lambda l, r: (l[0] * r[0], r[0] * l[1] + r[1]), (a, u), axis=1)
    return h, h[:, -1]


def rglru_group(xb, gb, conv_w, conv_b, w_r, b_r, w_i, b_i, lam, s0):
    b_, n, _ = xb.shape
    xc = (dwconv(xb, conv_w) + conv_b).astype(F32)
    xblk = xc.reshape(b_, n, D_BLOCKS, D_BW)
    r = jax.nn.sigmoid(jnp.einsum('bnhi,zhij->bnzhj', xblk, w_r.astype(F32)).reshape(b_, n, 2, D_WIDTH) + b_r)
    gi = jax.nn.sigmoid(jnp.einsum('bnhi,zhij->bnzhj', xblk, w_i.astype(F32)).reshape(b_, n, 2, D_WIDTH) + b_i)
    log_a = -D_C * jax.nn.softplus(-lam) * r
    a = jnp.exp(log_a)
    u = jnp.sqrt(-jnp.expm1(2.0 * log_a)) * gi * xc[:, :, None, :]
    h_f, s_f = _linear_scan(a[:, :, 0], u[:, :, 0], s0[0])
    h_b, s_b = _linear_scan(_flip(a[:, :, 1]), _flip(u[:, :, 1]), s0[1])
    y = (h_f + _flip(h_b)) * jax.nn.gelu(gb.astype(F32))
    return y, (s_f, s_b)


def odd_mixer(h_c, h_l, w_in, w_out, sink, d_conv_w, d_conv_b, d_w_r, d_b_r, d_w_i, d_b_i, d_lambda,
              row, col, ctx_out):
    b_ = h_l.shape[0]
    qc, kc, vc, xdc, gdc = jnp.split(h_c @ w_in, _cuts(ODD_SIZES), axis=-1)
    ql, kl, vl, xdl, gdl = jnp.split(h_l @ w_in, _cuts(ODD_SIZES), axis=-1)

    def heads(t, nh):
        return t.astype(F32).reshape(t.shape[0], t.shape[1], nh, C_HD)

    kc, vc = heads(kc, C_KV_HEADS), heads(vc, C_KV_HEADS)
    att_l = window_attention(rope_2d(heads(ql, C_HEADS), row, col), rope_2d(heads(kl, C_KV_HEADS), row, col),
                             heads(vl, C_KV_HEADS), kc, vc, sink)
    zero = jnp.zeros((b_, D_WIDTH), F32)
    rg_c, st = rglru_group(xdc, gdc, d_conv_w, d_conv_b, d_w_r, d_b_r, d_w_i, d_b_i, d_lambda, (zero, zero))
    rg_l, _ = rglru_group(xdl, gdl, d_conv_w, d_conv_b, d_w_r, d_b_r, d_w_i, d_b_i, d_lambda, st)
    y_l = jnp.concatenate([att_l, rg_l], axis=-1).astype(h_l.dtype) @ w_out
    y_c = None
    if ctx_out:
        att_c = context_attention(heads(qc, C_HEADS), kc, vc, sink)
        y_c = jnp.concatenate([att_c, rg_c], axis=-1).astype(h_c.dtype) @ w_out
    return y_c, y_l


def expert_choice_ffn(h, router, w1, w3, w2):
    b_, n, d = h.shape
    cap = max(1, EC_FACTOR * n // N_EXPERTS)
    aff = jax.nn.softmax(jnp.einsum('bnd,de->bne', h, router).astype(F32), axis=-1)
    gate, idx = lax.top_k(jnp.swapaxes(aff, 1, 2), cap)
    xs = jax.vmap(lambda hb, ib: hb[ib])(h, idx)
    hid = jax.nn.silu(jnp.einsum('becd,edf->becf', xs, w1)) * jnp.einsum('becd,edf->becf', xs, w3)
    out = jnp.einsum('becf,efd->becd', hid, w2) * gate[..., None].astype(h.dtype)
    return jax.vmap(lambda ob, ib: jnp.zeros((n, d), ob.dtype).at[ib.reshape(-1)].add(ob.reshape(-1, d)))(out, idx)


def setup_inputs(seed: int = 0) -> dict:
    key = jax.random.key(seed)
    ks = iter(jax.random.split(key, 32))

    def nrm(shape, scale):
        return jax.random.normal(next(ks), shape, F32) * scale

    def unif(shape, lo, hi):
        return jax.random.uniform(next(ks), shape, F32, lo, hi)

    dt = jnp.exp(unif((N_EVEN, 2, A_HEADS), math.log(1e-3), math.log(1e-1)))
    a_pow = unif((N_ODD, 2, D_WIDTH), 0.9, 0.999) ** (1.0 / D_C)
    return {
        'x': nrm((BATCH, SEQ, D_MODEL), 1.0),
        'c': nrm((BATCH, D_MODEL), 1.0),
        'ctx': nrm((BATCH, CTX_LEN, D_MODEL), 1.0),
        'c_ctx': nrm((D_MODEL,), 1.0),
        'w_mod': nrm((DEPTH, D_MODEL, 6 * D_MODEL), 0.5 * D_MODEL ** -0.5),
        'b_mod': nrm((DEPTH, 6 * D_MODEL), 0.02),
        'norm1_w': 1.0 + nrm((DEPTH, D_MODEL), 0.02),
        'norm2_w': 1.0 + nrm((DEPTH, D_MODEL), 0.02),
        'final_norm_w': 1.0 + nrm((D_MODEL,), 0.02),
        'ev_w_in': nrm((N_EVEN, D_MODEL, sum(EVEN_SIZES)), D_MODEL ** -0.5),
        'ev_w_out': nrm((N_EVEN, EVEN_OUT, D_MODEL), EVEN_OUT ** -0.5),
        'a_conv_w': nrm((N_EVEN, A_CONV, A_QKV), A_CONV ** -0.5),
        'a_log': jnp.log(unif((N_EVEN, 2, A_HEADS), 1.0, 16.0)),
        'a_dt_bias': dt + jnp.log(-jnp.expm1(-dt)),
        'a_norm_w': 1.0 + nrm((N_EVEN, A_DV), 0.02),
        'b_lb_logits': nrm((N_EVEN, 2, B_HEADS, B_DK), 1.0),
        'b_norm_w': 1.0 + nrm((N_EVEN, B_DV), 0.02),
        'od_w_in': nrm((N_ODD, D_MODEL, sum(ODD_SIZES)), D_MODEL ** -0.5),
        'od_w_out': nrm((N_ODD, ODD_OUT, D_MODEL), ODD_OUT ** -0.5),
        'c_sink': nrm((N_ODD, C_HEADS), 0.5),
        'd_conv_w': nrm((N_ODD, D_CONV, D_WIDTH), D_CONV ** -0.5),
        'd_conv_b': nrm((N_ODD, D_WIDTH), 0.01),
        'd_w_r': nrm((N_ODD, 2, D_BLOCKS, D_BW, D_BW), D_BW ** -0.5),
        'd_b_r': nrm((N_ODD, 2, D_WIDTH), 0.01),
        'd_w_i': nrm((N_ODD, 2, D_BLOCKS, D_BW, D_BW), D_BW ** -0.5),
        'd_b_i': nrm((N_ODD, 2, D_WIDTH), 0.01),
        'd_lambda': jnp.log(a_pow) - jnp.log1p(-a_pow),
        'moe_router': nrm((DEPTH, D_MODEL, N_EXPERTS), D_MODEL ** -0.5),
        'moe_w1': nrm((DEPTH, N_EXPERTS, D_MODEL, EXPERT_FF), D_MODEL ** -0.5),
        'moe_w3': nrm((DEPTH, N_EXPERTS, D_MODEL, EXPERT_FF), D_MODEL ** -0.5),
        'moe_w2': nrm((DEPTH, N_EXPERTS, EXPERT_FF, D_MODEL), EXPERT_FF ** -0.5),
    }


def reference(x, c, ctx, c_ctx, w_mod, b_mod, norm1_w, norm2_w, final_norm_w,
              ev_w_in, ev_w_out, a_conv_w, a_log, a_dt_bias, a_norm_w, b_lb_logits, b_norm_w,
              od_w_in, od_w_out, c_sink, d_conv_w, d_conv_b, d_w_r, d_b_r, d_w_i, d_b_i, d_lambda,
              moe_router, moe_w1, moe_w3, moe_w2):
    n = x.shape[1]
    rows = n // GRID_W
    row = jnp.repeat(jnp.arange(rows, dtype=jnp.int32), GRID_W)
    col = jnp.tile(jnp.arange(GRID_W, dtype=jnp.int32), rows)
    lb_all = jnp.cumsum(jax.nn.softmax(b_lb_logits.astype(F32), axis=0), axis=0)
    lb_all = lb_all - lb_all[0:1]
    for l in range(DEPTH):
        last = l == DEPTH - 1
        j = l // 2
        mod = (jax.nn.silu(c) @ w_mod[l] + b_mod[l])[:, None, :]
        mod_c = jax.nn.silu(c_ctx) @ w_mod[l] + b_mod[l]
        sh1, sc1, g1, sh2, sc2, g2 = jnp.split(mod, 6, axis=-1)
        csh1, csc1, cg1, csh2, csc2, cg2 = jnp.split(mod_c, 6, axis=-1)
        h_l = _modulate(rmsnorm(x, norm1_w[l]), sh1, sc1)
        h_c = _modulate(rmsnorm(ctx, norm1_w[l]), csh1, csc1)
        if l % 2 == 0:
            y_c, y_l = even_mixer(h_c, h_l, ev_w_in[j], ev_w_out[j], a_conv_w[j], a_log[j], a_dt_bias[j],
                                  a_norm_w[j], lb_all[j], b_norm_w[j], not last)
        else:
            y_c, y_l = odd_mixer(h_c, h_l, od_w_in[j], od_w_out[j], c_sink[j], d_conv_w[j], d_conv_b[j],
                                 d_w_r[j], d_b_r[j], d_w_i[j], d_b_i[j], d_lambda[j], row, col, not last)
        x = x + g1 * y_l
        h_l = _modulate(rmsnorm(x, norm2_w[l]), sh2, sc2)
        x = x + g2 * expert_choice_ffn(h_l, moe_router[l], moe_w1[l], moe_w3[l], moe_w2[l])
        if not last:
            ctx = ctx + cg1 * y_c
            h_c = _modulate(rmsnorm(ctx, norm2_w[l]), csh2, csc2)
            ctx = ctx + cg2 * expert_choice_ffn(h_c, moe_router[l], moe_w1[l], moe_w3[l], moe_w2[l])
    return rmsnorm(x, final_norm_w)
```

```python
import functools
import math

import jax
import jax.numpy as jnp
from jax import lax
from jax.experimental import pallas as pl
from jax.experimental.pallas import tpu as pltpu

D_MODEL = 1024
DEPTH = 4
GRID_W = 64
EPS = 1e-6
F32 = jnp.float32
BF16 = jnp.bfloat16

A_HEADS = 4
A_DK = 128
A_DV = 128
A_CONV = 4
A_CHUNK = 64
A_QKV = 2 * A_HEADS * A_DK + A_HEADS * A_DV
B_HEADS = 4
B_DK = 128
B_DV = 128
B_CHUNK = 64
C_HEADS = 8
C_KV_HEADS = 2
C_HD = 64
C_WIN = 128
C_BLOCK = 128
ROPE_THETA = 10000.0
D_WIDTH = 512
D_BLOCKS = 8
D_BW = D_WIDTH // D_BLOCKS
D_CONV = 4
D_C = 8.0
N_EXPERTS = 16
EXPERT_FF = 1024
EC_FACTOR = 2

EVEN_SIZES = (A_QKV, A_HEADS * A_DV, 2 * A_HEADS, 2 * A_HEADS,
              B_HEADS * B_DK, B_HEADS * B_DV, 2 * B_HEADS * B_DK, B_HEADS * B_DV)
EVEN_OUT = A_HEADS * A_DV + B_HEADS * B_DV
ODD_SIZES = (C_HEADS * C_HD, C_KV_HEADS * C_HD, C_KV_HEADS * C_HD, D_WIDTH, D_WIDTH)
ODD_OUT = C_HEADS * C_HD + D_WIDTH

VMEM_LIMIT_BYTES = 56 * 1024 * 1024


def _cuts(sizes):
    out, acc = [], 0
    for s in sizes[:-1]:
        acc += s
        out.append(acc)
    return out


def rmsnorm(x, w):
    xf = x.astype(F32)
    y = xf * lax.rsqrt(jnp.mean(xf * xf, axis=-1, keepdims=True) + EPS)
    return (y * w).astype(x.dtype)


def _modulate(h, shift, scale):
    return h * (1 + scale) + shift


def _l2norm(t):
    return t * lax.rsqrt(jnp.sum(t * t, axis=-1, keepdims=True) + EPS)


def _flip(t):
    return jnp.flip(t, axis=1)


def dwconv(x, w):
    k = w.shape[0]
    return lax.conv_general_dilated(x, w[:, None, :].astype(x.dtype), (1,), [((k - 1) // 2, k // 2)],
                                    dimension_numbers=('NWC', 'WIO', 'NWC'),
                                    feature_group_count=x.shape[-1])


def _to_chunks(t, length):
    b_, n, h = t.shape[:3]
    rest = t.shape[3:]
    t = t.reshape((b_, n // length, length, h) + rest)
    return t.transpose((1, 0, 3, 2) + tuple(range(4, t.ndim)))


def _from_chunks(t):
    nc, b_, h, length, d = t.shape
    return t.transpose(1, 0, 3, 2, 4).reshape(b_, nc * length, h, d)


def gated_delta_chunked(q, k, v, g, beta, s0):
    length = A_CHUNK
    dv = v.shape[-1]
    q, k, v = _to_chunks(q, length), _to_chunks(k, length), _to_chunks(v, length)
    g, beta = _to_chunks(g, length), _to_chunks(beta, length)
    cum = jnp.cumsum(g, axis=-1)
    incl = jnp.tril(jnp.ones((length, length), bool))
    strict = jnp.tril(jnp.ones((length, length), bool), -1)
    decay = jnp.exp(jnp.where(incl, cum[..., :, None] - cum[..., None, :], -jnp.inf))
    kb = k * beta[..., None]
    m = jnp.where(strict, jnp.einsum('...id,...jd->...ij', kb, k) * decay, 0.0)
    rhs = jnp.concatenate([v * beta[..., None], kb * jnp.exp(cum)[..., None]], axis=-1)
    sol = lax.linalg.triangular_solve(m + jnp.eye(length, dtype=m.dtype), rhs, left_side=True, lower=True)
    u, w = sol[..., :dv], sol[..., dv:]
    qk = jnp.einsum('...id,...jd->...ij', q, k) * decay
    q_dec = q * jnp.exp(cum)[..., None]
    k_dec = k * jnp.exp(cum[..., -1:] - cum)[..., None]
    g_last = jnp.exp(cum[..., -1])

    def step(s, inp):
        u_c, w_c, qk_c, qd_c, kd_c, gl_c = inp
        v_new = u_c - jnp.einsum('bhld,bhdv->bhlv', w_c, s)
        o = jnp.einsum('bhld,bhdv->bhlv', qd_c, s) + jnp.einsum('bhlm,bhmv->bhlv', qk_c, v_new)
        s = s * gl_c[..., None, None] + jnp.einsum('bhld,bhlv->bhdv', kd_c, v_new)
        return s, o

    s_fin, o = lax.scan(step, s0, (u, w, qk, q_dec, k_dec, g_last))
    return _from_chunks(o), s_fin


def hgrn2_chunked(q, k, v, lf, s0):
    length = B_CHUNK
    q, k, v, lf = (_to_chunks(t, length) for t in (q, k, v, lf))
    cum = jnp.cumsum(lf, axis=-2)
    incl = jnp.tril(jnp.ones((length, length), bool))[:, :, None]

    def step(s, inp):
        q_c, k_c, v_c, b_c = inp
        dec = jnp.exp(jnp.where(incl, b_c[..., :, None, :] - b_c[..., None, :, :], -jnp.inf))
        att = jnp.einsum('bhid,bhjd,bhijd->bhij', q_c, k_c, dec)
        b_last = b_c[..., -1:, :]
        o = jnp.einsum('bhij,bhjv->bhiv', att, v_c) + jnp.einsum('bhid,bhdv->bhiv', q_c * jnp.exp(b_c), s)
        s = s * jnp.exp(b_last)[..., 0, :, None] + jnp.einsum('bhjd,bhjv->bhdv', k_c * jnp.exp(b_last - b_c), v_c)
        return s, o

    s_fin, o = lax.scan(step, s0, (q, k, v, cum))
    return _from_chunks(o), s_fin


def delta_group(qkv, gate, alpha, beta, conv_w, a_log, dt_bias, norm_w, s0):
    b_, n, _ = qkv.shape
    qkv = jax.nn.silu(dwconv(qkv, conv_w)).astype(F32)
    q, k, v = jnp.split(qkv, [A_HEADS * A_DK, 2 * A_HEADS * A_DK], axis=-1)
    q = _l2norm(q.reshape(b_, n, A_HEADS, A_DK)) * (A_DK ** -0.5)
    k = _l2norm(k.reshape(b_, n, A_HEADS, A_DK))
    v = v.reshape(b_, n, A_HEADS, A_DV)
    g = -jnp.exp(a_log) * jax.nn.softplus(alpha.astype(F32).reshape(b_, n, 2, A_HEADS) + dt_bias)
    bt = jax.nn.sigmoid(beta.astype(F32).reshape(b_, n, 2, A_HEADS))
    o_f, s_f = gated_delta_chunked(q, k, v, g[:, :, 0], bt[:, :, 0], s0[0])
    o_b, s_b = gated_delta_chunked(_flip(q), _flip(k), _flip(v), _flip(g[:, :, 1]), _flip(bt[:, :, 1]), s0[1])
    o = o_f + _flip(o_b)
    o = rmsnorm(o, norm_w) * jax.nn.silu(gate.astype(F32).reshape(b_, n, A_HEADS, A_DV))
    return o.reshape(b_, n, A_HEADS * A_DV), (s_f, s_b)


def hgrn2_group(q, i, f, gate, lb, norm_w, s0):
    b_, n, _ = q.shape
    q = jax.nn.silu(q.astype(F32)).reshape(b_, n, B_HEADS, B_DK)
    v = i.astype(F32).reshape(b_, n, B_HEADS, B_DV)
    fg = lb + (1.0 - lb) * jax.nn.sigmoid(f.astype(F32).reshape(b_, n, 2, B_HEADS, B_DK))
    lf = jnp.log(fg)
    k = 1.0 - fg
    o_f, s_f = hgrn2_chunked(q, k[:, :, 0], v, lf[:, :, 0], s0[0])
    o_b, s_b = hgrn2_chunked(_flip(q), _flip(k[:, :, 1]), _flip(v), _flip(lf[:, :, 1]), s0[1])
    o = o_f + _flip(o_b)
    o = rmsnorm(o, norm_w) * jax.nn.silu(gate.astype(F32).reshape(b_, n, B_HEADS, B_DV))
    return o.reshape(b_, n, B_HEADS * B_DV), (s_f, s_b)


def even_mixer(h_c, h_l, w_in, w_out, conv_w, a_log, dt_bias, a_norm_w, lb, b_norm_w, ctx_out):
    b_ = h_l.shape[0]
    zero_a = jnp.zeros((b_, A_HEADS, A_DK, A_DV), F32)
    zero_b = jnp.zeros((b_, B_HEADS, B_DK, B_DV), F32)

    def run(h, st_a, st_b):
        qkv, ga, al, be, qb, ib, fb, gb = jnp.split(h @ w_in, _cuts(EVEN_SIZES), axis=-1)
        oa, st_a = delta_group(qkv, ga, al, be, conv_w, a_log, dt_bias, a_norm_w, st_a)
        ob, st_b = hgrn2_group(qb, ib, fb, gb, lb, b_norm_w, st_b)
        return jnp.concatenate([oa, ob], axis=-1).astype(h.dtype), st_a, st_b

    o_c, st_a, st_b = run(h_c, (zero_a, zero_a), (zero_b, zero_b))
    o_l, _, _ = run(h_l, st_a, st_b)
    y_c = o_c @ w_out if ctx_out else None
    return y_c, o_l @ w_out


def rope_2d(t, row, col):
    half = t.shape[-1] // 2
    nf = half // 2
    inv = ROPE_THETA ** (-jnp.arange(nf, dtype=F32) / nf)

    def rot(u, pos):
        ang = pos.astype(F32)[:, None] * inv
        cos, sin = jnp.cos(ang)[None, :, None, :], jnp.sin(ang)[None, :, None, :]
        u1, u2 = u[..., :nf], u[..., nf:]
        return jnp.concatenate([u1 * cos - u2 * sin, u2 * cos + u1 * sin], axis=-1)

    return jnp.concatenate([rot(t[..., :half], row), rot(t[..., half:], col)], axis=-1)


def _sink_softmax(s, sink):
    m = jnp.maximum(jnp.max(s, axis=-1, keepdims=True), sink)
    e = jnp.exp(s - m)
    return e / (jnp.sum(e, axis=-1, keepdims=True) + jnp.exp(sink - m))


def window_attention(q_l, k_l, v_l, k_c, v_c, sink):
    b_, n, h, hd = q_l.shape
    grp = h // C_KV_HEADS
    nb = n // C_BLOCK
    qb = q_l.reshape(b_, nb, C_BLOCK, C_KV_HEADS, grp, hd) * (hd ** -0.5)

    def band(t):
        tp = jnp.pad(t, ((0, 0), (C_BLOCK, C_BLOCK), (0, 0), (0, 0))).reshape(b_, nb + 2, C_BLOCK, C_KV_HEADS, hd)
        return jnp.concatenate([tp[:, :-2], tp[:, 1:-1], tp[:, 2:]], axis=2)

    kb, vb = band(k_l), band(v_l)
    s_lat = jnp.einsum('bnqkgd,bnskd->bnkgqs', qb, kb)
    s_ctx = jnp.einsum('bnqkgd,bskd->bnkgqs', qb, k_c)
    qpos = jnp.arange(C_BLOCK)[:, None]
    kpos = jnp.arange(3 * C_BLOCK)[None, :] - C_BLOCK
    s_abs = jnp.arange(nb)[:, None, None] * C_BLOCK + kpos[None]
    mask = (jnp.abs(kpos - qpos) <= C_WIN)[None] & (s_abs >= 0) & (s_abs < n)
    s_lat = jnp.where(mask[None, :, None, None], s_lat, -jnp.inf)
    sink_b = sink.astype(F32).reshape(C_KV_HEADS, grp)[None, None, :, :, None, None]
    p = _sink_softmax(jnp.concatenate([s_lat, s_ctx], axis=-1), sink_b)
    o = (jnp.einsum('bnkgqs,bnskd->bnqkgd', p[..., :3 * C_BLOCK], vb)
         + jnp.einsum('bnkgqs,bskd->bnqkgd', p[..., 3 * C_BLOCK:], v_c))
    return o.reshape(b_, n, h * hd)


def context_attention(q_c, k_c, v_c, sink):
    b_, lc, h, hd = q_c.shape
    grp = h // C_KV_HEADS
    qg = q_c.reshape(b_, lc, C_KV_HEADS, grp, hd) * (hd ** -0.5)
    s = jnp.einsum('bqkgd,bskd->bkgqs', qg, k_c)
    p = _sink_softmax(s, sink.astype(F32).reshape(C_KV_HEADS, grp)[None, :, :, None, None])
    return jnp.einsum('bkgqs,bskd->bqkgd', p, v_c).reshape(b_, lc, h * hd)


def _linear_scan(a, u, h0):
    u = u.at[:, 0].add(a[:, 0] * h0)
    _, h = lax.associative_scan(lambda l, r: (l[0] * r[0], r[0] * l[1] + r[1]), (a, u), axis=1)
    return h, h[:, -1]


def rglru_group(xb, gb, conv_w, conv_b, w_r, b_r, w_i, b_i, lam, s0):
    b_, n, _ = xb.shape
    xc = (dwconv(xb, conv_w) + conv_b).astype(F32)
    xblk = xc.reshape(b_, n, D_BLOCKS, D_BW)
    r = jax.nn.sigmoid(jnp.einsum('bnhi,zhij->bnzhj', xblk, w_r.astype(F32)).reshape(b_, n, 2, D_WIDTH) + b_r)
    gi = jax.nn.sigmoid(jnp.einsum('bnhi,zhij->bnzhj', xblk, w_i.astype(F32)).reshape(b_, n, 2, D_WIDTH) + b_i)
    log_a = -D_C * jax.nn.softplus(-lam) * r
    a = jnp.exp(log_a)
    u = jnp.sqrt(-jnp.expm1(2.0 * log_a)) * gi * xc[:, :, None, :]
    h_f, s_f = _linear_scan(a[:, :, 0], u[:, :, 0], s0[0])
    h_b, s_b = _linear_scan(_flip(a[:, :, 1]), _flip(u[:, :, 1]), s0[1])
    y = (h_f + _flip(h_b)) * jax.nn.gelu(gb.astype(F32))
    return y, (s_f, s_b)


def odd_mixer(h_c, h_l, w_in, w_out, sink, d_conv_w, d_conv_b, d_w_r, d_b_r, d_w_i, d_b_i, d_lambda,
              row, col, ctx_out):
    b_ = h_l.shape[0]
    qc, kc, vc, xdc, gdc = jnp.split(h_c @ w_in, _cuts(ODD_SIZES), axis=-1)
    ql, kl, vl, xdl, gdl = jnp.split(h_l @ w_in, _cuts(ODD_SIZES), axis=-1)

    def heads(t, nh):
        return t.astype(F32).reshape(t.shape[0], t.shape[1], nh, C_HD)

    kc, vc = heads(kc, C_KV_HEADS), heads(vc, C_KV_HEADS)
    att_l = window_attention(rope_2d(heads(ql, C_HEADS), row, col), rope_2d(heads(kl, C_KV_HEADS), row, col),
                             heads(vl, C_KV_HEADS), kc, vc, sink)
    zero = jnp.zeros((b_, D_WIDTH), F32)
    rg_c, st = rglru_group(xdc, gdc, d_conv_w, d_conv_b, d_w_r, d_b_r, d_w_i, d_b_i, d_lambda, (zero, zero))
    rg_l, _ = rglru_group(xdl, gdl, d_conv_w, d_conv_b, d_w_r, d_b_r, d_w_i, d_b_i, d_lambda, st)
    y_l = jnp.concatenate([att_l, rg_l], axis=-1).astype(h_l.dtype) @ w_out
    y_c = None
    if ctx_out:
        att_c = context_attention(heads(qc, C_HEADS), kc, vc, sink)
        y_c = jnp.concatenate([att_c, rg_c], axis=-1).astype(h_c.dtype) @ w_out
    return y_c, y_l


def _expert_ffn_kernel(x_ref, g_ref, w1_ref, w3_ref, w2_ref, o_ref):
    x = x_ref[0, 0].astype(BF16)
    a = jnp.dot(x, w1_ref[0], preferred_element_type=F32)
    b = jnp.dot(x, w3_ref[0], preferred_element_type=F32)
    hid = (a * jax.nn.sigmoid(a) * b).astype(BF16)
    y = jnp.dot(hid, w2_ref[0], preferred_element_type=F32)
    o_ref[0, 0] = y * g_ref[0, 0]


def expert_ffn(xs, gate, w1, w3, w2):
    b_, e_, c_, d = xs.shape
    f = w1.shape[-1]
    tm = min(c_, 512)
    grid = (e_, b_, c_ // tm)
    return pl.pallas_call(
        _expert_ffn_kernel,
        grid=grid,
        in_specs=[
            pl.BlockSpec((1, 1, tm, d), lambda e, b, m: (b, e, m, 0)),
            pl.BlockSpec((1, 1, tm, 1), lambda e, b, m: (b, e, m, 0)),
            pl.BlockSpec((1, d, f), lambda e, b, m: (e, 0, 0)),
            pl.BlockSpec((1, d, f), lambda e, b, m: (e, 0, 0)),
            pl.BlockSpec((1, f, d), lambda e, b, m: (e, 0, 0)),
        ],
        out_specs=pl.BlockSpec((1, 1, tm, d), lambda e, b, m: (b, e, m, 0)),
        out_shape=jax.ShapeDtypeStruct((b_, e_, c_, d), F32),
        compiler_params=pltpu.CompilerParams(
            dimension_semantics=("arbitrary", "arbitrary", "arbitrary"),
            vmem_limit_bytes=VMEM_LIMIT_BYTES),
        name="expert_ffn",
    )(xs, gate, w1, w3, w2)


def expert_choice_ffn(h, router, w1, w3, w2):
    b_, n, d = h.shape
    cap = max(1, EC_FACTOR * n // N_EXPERTS)
    aff = jax.nn.softmax(jnp.einsum('bnd,de->bne', h, router).astype(F32), axis=-1)
    gate, idx = lax.top_k(jnp.swapaxes(aff, 1, 2), cap)
    xs = jax.vmap(lambda hb, ib: hb[ib])(h, idx)
    out = expert_ffn(xs, gate[..., None], w1, w3, w2)
    return jax.vmap(lambda ob, ib: jnp.zeros((n, d), ob.dtype).at[ib.reshape(-1)].add(ob.reshape(-1, d)))(out, idx)


def kernel(x, c, ctx, c_ctx, w_mod, b_mod, norm1_w, norm2_w, final_norm_w,
           ev_w_in, ev_w_out, a_conv_w, a_log, a_dt_bias, a_norm_w, b_lb_logits, b_norm_w,
           od_w_in, od_w_out, c_sink, d_conv_w, d_conv_b, d_w_r, d_b_r, d_w_i, d_b_i, d_lambda,
           moe_router, moe_w1, moe_w3, moe_w2):
    n = x.shape[1]
    rows = n // GRID_W
    row = jnp.repeat(jnp.arange(rows, dtype=jnp.int32), GRID_W)
    col = jnp.tile(jnp.arange(GRID_W, dtype=jnp.int32), rows)
    lb_all = jnp.cumsum(jax.nn.softmax(b_lb_logits.astype(F32), axis=0), axis=0)
    lb_all = lb_all - lb_all[0:1]
    for l in range(DEPTH):
        last = l == DEPTH - 1
        j = l // 2
        w1b, w3b, w2b = moe_w1[l].astype(BF16), moe_w3[l].astype(BF16), moe_w2[l].astype(BF16)
        mod = (jax.nn.silu(c) @ w_mod[l] + b_mod[l])[:, None, :]
        mod_c = jax.nn.silu(c_ctx) @ w_mod[l] + b_mod[l]
        sh1, sc1, g1, sh2, sc2, g2 = jnp.split(mod, 6, axis=-1)
        csh1, csc1, cg1, csh2, csc2, cg2 = jnp.split(mod_c, 6, axis=-1)
        h_l = _modulate(rmsnorm(x, norm1_w[l]), sh1, sc1)
        h_c = _modulate(rmsnorm(ctx, norm1_w[l]), csh1, csc1)
        if l % 2 == 0:
            y_c, y_l = even_mixer(h_c, h_l, ev_w_in[j], ev_w_out[j], a_conv_w[j], a_log[j], a_dt_bias[j],
                                  a_norm_w[j], lb_all[j], b_norm_w[j], not last)
        else:
            y_c, y_l = odd_mixer(h_c, h_l, od_w_in[j], od_w_out[j], c_sink[j], d_conv_w[j], d_conv_b[j],
                                 d_w_r[j], d_b_r[j], d_w_i[j], d_b_i[j], d_lambda[j], row, col, not last)
        x = x + g1 * y_l
        h_l = _modulate(rmsnorm(x, norm2_w[l]), sh2, sc2)
        x = x + g2 * expert_choice_ffn(h_l, moe_router[l], w1b, w3b, w2b)
        if not last:
            ctx = ctx + cg1 * y_c
            h_c = _modulate(rmsnorm(ctx, norm2_w[l]), csh2, csc2)
            ctx = ctx + cg2 * expert_choice_ffn(h_c, moe_router[l], w1b, w3b, w2b)
    return rmsnorm(x, final_norm_w)
```

```python
import functools
import math

import jax
import jax.numpy as jnp
from jax import lax
from jax.experimental import pallas as pl
from jax.experimental.pallas import tpu as pltpu

D_MODEL = 1024
DEPTH = 4
GRID_W = 64
EPS = 1e-6
F32 = jnp.float32
BF16 = jnp.bfloat16

A_HEADS = 4
A_DK = 128
A_DV = 128
A_CONV = 4
A_CHUNK = 64
A_QKV = 2 * A_HEADS * A_DK + A_HEADS * A_DV
B_HEADS = 4
B_DK = 128
B_DV = 128
B_CHUNK = 64
C_HEADS = 8
C_KV_HEADS = 2
C_HD = 64
C_WIN = 128
C_BLOCK = 128
ROPE_THETA = 10000.0
D_WIDTH = 512
D_BLOCKS = 8
D_BW = D_WIDTH // D_BLOCKS
D_CONV = 4
D_C = 8.0
N_EXPERTS = 16
EXPERT_FF = 1024
EC_FACTOR = 2

EVEN_SIZES = (A_QKV, A_HEADS * A_DV, 2 * A_HEADS, 2 * A_HEADS,
              B_HEADS * B_DK, B_HEADS * B_DV, 2 * B_HEADS * B_DK, B_HEADS * B_DV)
EVEN_OUT = A_HEADS * A_DV + B_HEADS * B_DV
ODD_SIZES = (C_HEADS * C_HD, C_KV_HEADS * C_HD, C_KV_HEADS * C_HD, D_WIDTH, D_WIDTH)
ODD_OUT = C_HEADS * C_HD + D_WIDTH

VMEM_LIMIT_BYTES = 56 * 1024 * 1024


def _cuts(sizes):
    out, acc = [], 0
    for s in sizes[:-1]:
        acc += s
        out.append(acc)
    return out


def rmsnorm(x, w):
    xf = x.astype(F32)
    y = xf * lax.rsqrt(jnp.mean(xf * xf, axis=-1, keepdims=True) + EPS)
    return (y * w).astype(x.dtype)


def _modulate(h, shift, scale):
    return h * (1 + scale) + shift


def _l2norm(t):
    return t * lax.rsqrt(jnp.sum(t * t, axis=-1, keepdims=True) + EPS)


def _flip(t):
    return jnp.flip(t, axis=1)


def dwconv(x, w):
    k = w.shape[0]
    return lax.conv_general_dilated(x, w[:, None, :].astype(x.dtype), (1,), [((k - 1) // 2, k // 2)],
                                    dimension_numbers=('NWC', 'WIO', 'NWC'),
                                    feature_group_count=x.shape[-1])


def _to_chunks(t, length):
    b_, n, h = t.shape[:3]
    rest = t.shape[3:]
    t = t.reshape((b_, n // length, length, h) + rest)
    return t.transpose((1, 0, 3, 2) + tuple(range(4, t.ndim)))


def _from_chunks(t):
    nc, b_, h, length, d = t.shape
    return t.transpose(1, 0, 3, 2, 4).reshape(b_, nc * length, h, d)


def gated_delta_chunked(q, k, v, g, beta, s0):
    length = A_CHUNK
    dv = v.shape[-1]
    q, k, v = _to_chunks(q, length), _to_chunks(k, length), _to_chunks(v, length)
    g, beta = _to_chunks(g, length), _to_chunks(beta, length)
    cum = jnp.cumsum(g, axis=-1)
    incl = jnp.tril(jnp.ones((length, length), bool))
    strict = jnp.tril(jnp.ones((length, length), bool), -1)
    decay = jnp.exp(jnp.where(incl, cum[..., :, None] - cum[..., None, :], -jnp.inf))
    kb = k * beta[..., None]
    m = jnp.where(strict, jnp.einsum('...id,...jd->...ij', kb, k) * decay, 0.0)
    rhs = jnp.concatenate([v * beta[..., None], kb * jnp.exp(cum)[..., None]], axis=-1)
    sol = lax.linalg.triangular_solve(m + jnp.eye(length, dtype=m.dtype), rhs, left_side=True, lower=True)
    u, w = sol[..., :dv], sol[..., dv:]
    qk = jnp.einsum('...id,...jd->...ij', q, k) * decay
    q_dec = q * jnp.exp(cum)[..., None]
    k_dec = k * jnp.exp(cum[..., -1:] - cum)[..., None]
    g_last = jnp.exp(cum[..., -1])

    def step(s, inp):
        u_c, w_c, qk_c, qd_c, kd_c, gl_c = inp
        v_new = u_c - jnp.einsum('bhld,bhdv->bhlv', w_c, s)
        o = jnp.einsum('bhld,bhdv->bhlv', qd_c, s) + jnp.einsum('bhlm,bhmv->bhlv', qk_c, v_new)
        s = s * gl_c[..., None, None] + jnp.einsum('bhld,bhlv->bhdv', kd_c, v_new)
        return s, o

    s_fin, o = lax.scan(step, s0, (u, w, qk, q_dec, k_dec, g_last))
    return _from_chunks(o), s_fin


def hgrn2_chunked(q, k, v, lf, s0):
    length = B_CHUNK
    q, k, v, lf = (_to_chunks(t, length) for t in (q, k, v, lf))
    cum = jnp.cumsum(lf, axis=-2)
    incl = jnp.tril(jnp.ones((length, length), bool))[:, :, None]

    def step(s, inp):
        q_c, k_c, v_c, b_c = inp
        dec = jnp.exp(jnp.where(incl, b_c[..., :, None, :] - b_c[..., None, :, :], -jnp.inf))
        att = jnp.einsum('bhid,bhjd,bhijd->bhij', q_c, k_c, dec)
        b_last = b_c[..., -1:, :]
        o = jnp.einsum('bhij,bhjv->bhiv', att, v_c) + jnp.einsum('bhid,bhdv->bhiv', q_c * jnp.exp(b_c), s)
        s = s * jnp.exp(b_last)[..., 0, :, None] + jnp.einsum('bhjd,bhjv->bhdv', k_c * jnp.exp(b_last - b_c), v_c)
        return s, o

    s_fin, o = lax.scan(step, s0, (q, k, v, cum))
    return _from_chunks(o), s_fin


def delta_group(qkv, gate, alpha, beta, conv_w, a_log, dt_bias, norm_w, s0):
    b_, n, _ = qkv.shape
    qkv = jax.nn.silu(dwconv(qkv, conv_w)).astype(F32)
    q, k, v = jnp.split(qkv, [A_HEADS * A_DK, 2 * A_HEADS * A_DK], axis=-1)
    q = _l2norm(q.reshape(b_, n, A_HEADS, A_DK)) * (A_DK ** -0.5)
    k = _l2norm(k.reshape(b_, n, A_HEADS, A_DK))
    v = v.reshape(b_, n, A_HEADS, A_DV)
    g = -jnp.exp(a_log) * jax.nn.softplus(alpha.astype(F32).reshape(b_, n, 2, A_HEADS) + dt_bias)
    bt = jax.nn.sigmoid(beta.astype(F32).reshape(b_, n, 2, A_HEADS))
    o_f, s_f = gated_delta_chunked(q, k, v, g[:, :, 0], bt[:, :, 0], s0[0])
    o_b, s_b = gated_delta_chunked(_flip(q), _flip(k), _flip(v), _flip(g[:, :, 1]), _flip(bt[:, :, 1]), s0[1])
    o = o_f + _flip(o_b)
    o = rmsnorm(o, norm_w) * jax.nn.silu(gate.astype(F32).reshape(b_, n, A_HEADS, A_DV))
    return o.reshape(b_, n, A_HEADS * A_DV), (s_f, s_b)


def hgrn2_group(q, i, f, gate, lb, norm_w, s0):
    b_, n, _ = q.shape
    q = jax.nn.silu(q.astype(F32)).reshape(b_, n, B_HEADS, B_DK)
    v = i.astype(F32).reshape(b_, n, B_HEADS, B_DV)
    fg = lb + (1.0 - lb) * jax.nn.sigmoid(f.astype(F32).reshape(b_, n, 2, B_HEADS, B_DK))
    lf = jnp.log(fg)
    k = 1.0 - fg
    o_f, s_f = hgrn2_chunked(q, k[:, :, 0], v, lf[:, :, 0], s0[0])
    o_b, s_b = hgrn2_chunked(_flip(q), _flip(k[:, :, 1]), _flip(v), _flip(lf[:, :, 1]), s0[1])
    o = o_f + _flip(o_b)
    o = rmsnorm(o, norm_w) * jax.nn.silu(gate.astype(F32).reshape(b_, n, B_HEADS, B_DV))
    return o.reshape(b_, n, B_HEADS * B_DV), (s_f, s_b)


def even_mixer(h_c, h_l, w_in, w_out, conv_w, a_log, dt_bias, a_norm_w, lb, b_norm_w, ctx_out):
    b_ = h_l.shape[0]
    zero_a = jnp.zeros((b_, A_HEADS, A_DK, A_DV), F32)
    zero_b = jnp.zeros((b_, B_HEADS, B_DK, B_DV), F32)

    def run(h, st_a, st_b):
        qkv, ga, al, be, qb, ib, fb, gb = jnp.split(h @ w_in, _cuts(EVEN_SIZES), axis=-1)
        oa, st_a = delta_group(qkv, ga, al, be, conv_w, a_log, dt_bias, a_norm_w, st_a)
        ob, st_b = hgrn2_group(qb, ib, fb, gb, lb, b_norm_w, st_b)
        return jnp.concatenate([oa, ob], axis=-1).astype(h.dtype), st_a, st_b

    o_c, st_a, st_b = run(h_c, (zero_a, zero_a), (zero_b, zero_b))
    o_l, _, _ = run(h_l, st_a, st_b)
    y_c = o_c @ w_out if ctx_out else None
    return y_c, o_l @ w_out


def rope_2d(t, row, col):
    half = t.shape[-1] // 2
    nf = half // 2
    inv = ROPE_THETA ** (-jnp.arange(nf, dtype=F32) / nf)

    def rot(u, pos):
        ang = pos.astype(F32)[:, None] * inv
        cos, sin = jnp.cos(ang)[None, :, None, :], jnp.sin(ang)[None, :, None, :]
        u1, u2 = u[..., :nf], u[..., nf:]
        return jnp.concatenate([u1 * cos - u2 * sin, u2 * cos + u1 * sin], axis=-1)

    return jnp.concatenate([rot(t[..., :half], row), rot(t[..., half:], col)], axis=-1)


def _sink_softmax(s, sink):
    m = jnp.maximum(jnp.max(s, axis=-1, keepdims=True), sink)
    e = jnp.exp(s - m)
    return e / (jnp.sum(e, axis=-1, keepdims=True) + jnp.exp(sink - m))


def window_attention(q_l, k_l, v_l, k_c, v_c, sink):
    b_, n, h, hd = q_l.shape
    grp = h // C_KV_HEADS
    nb = n // C_BLOCK
    qb = q_l.reshape(b_, nb, C_BLOCK, C_KV_HEADS, grp, hd) * (hd ** -0.5)

    def band(t):
        tp = jnp.pad(t, ((0, 0), (C_BLOCK, C_BLOCK), (0, 0), (0, 0))).reshape(b_, nb + 2, C_BLOCK, C_KV_HEADS, hd)
        return jnp.concatenate([tp[:, :-2], tp[:, 1:-1], tp[:, 2:]], axis=2)

    kb, vb = band(k_l), band(v_l)
    s_lat = jnp.einsum('bnqkgd,bnskd->bnkgqs', qb, kb)
    s_ctx = jnp.einsum('bnqkgd,bskd->bnkgqs', qb, k_c)
    qpos = jnp.arange(C_BLOCK)[:, None]
    kpos = jnp.arange(3 * C_BLOCK)[None, :] - C_BLOCK
    s_abs = jnp.arange(nb)[:, None, None] * C_BLOCK + kpos[None]
    mask = (jnp.abs(kpos - qpos) <= C_WIN)[None] & (s_abs >= 0) & (s_abs < n)
    s_lat = jnp.where(mask[None, :, None, None], s_lat, -jnp.inf)
    sink_b = sink.astype(F32).reshape(C_KV_HEADS, grp)[None, None, :, :, None, None]
    p = _sink_softmax(jnp.concatenate([s_lat, s_ctx], axis=-1), sink_b)
    o = (jnp.einsum('bnkgqs,bnskd->bnqkgd', p[..., :3 * C_BLOCK], vb)
         + jnp.einsum('bnkgqs,bskd->bnqkgd', p[..., 3 * C_BLOCK:], v_c))
    return o.reshape(b_, n, h * hd)


def context_attention(q_c, k_c, v_c, sink):
    b_, lc, h, hd = q_c.shape
    grp = h // C_KV_HEADS
    qg = q_c.reshape(b_, lc, C_KV_HEADS, grp, hd) * (hd ** -0.5)
    s = jnp.einsum('bqkgd,bskd->bkgqs', qg, k_c)
    p = _sink_softmax(s, sink.astype(F32).reshape(C_KV_HEADS, grp)[None, :, :, None, None])
    return jnp.einsum('bkgqs,bskd->bqkgd', p, v_c).reshape(b_, lc, h * hd)


def _linear_scan(a, u, h0):
    u = u.at[:, 0].add(a[:, 0] * h0)
    _, h = lax.associative_scan(lambda l, r: (l[0] * r[0], r[0] * l[1] + r[1]), (a, u), axis=1)
    return h, h[:, -1]


def rglru_group(xb, gb, conv_w, conv_b, w_r, b_r, w_i, b_i, lam, s0):
    b_, n, _ = xb.shape
    xc = (dwconv(xb, conv_w) + conv_b).astype(F32)
    xblk = xc.reshape(b_, n, D_BLOCKS, D_BW)
    r = jax.nn.sigmoid(jnp.einsum('bnhi,zhij->bnzhj', xblk, w_r.astype(F32)).reshape(b_, n, 2, D_WIDTH) + b_r)
    gi = jax.nn.sigmoid(jnp.einsum('bnhi,zhij->bnzhj', xblk, w_i.astype(F32)).reshape(b_, n, 2, D_WIDTH) + b_i)
    log_a = -D_C * jax.nn.softplus(-lam) * r
    a = jnp.exp(log_a)
    u = jnp.sqrt(-jnp.expm1(2.0 * log_a)) * gi * xc[:, :, None, :]
    h_f, s_f = _linear_scan(a[:, :, 0], u[:, :, 0], s0[0])
    h_b, s_b = _linear_scan(_flip(a[:, :, 1]), _flip(u[:, :, 1]), s0[1])
    y = (h_f + _flip(h_b)) * jax.nn.gelu(gb.astype(F32))
    return y, (s_f, s_b)


def odd_mixer(h_c, h_l, w_in, w_out, sink, d_conv_w, d_conv_b, d_w_r, d_b_r, d_w_i, d_b_i, d_lambda,
              row, col, ctx_out):
    b_ = h_l.shape[0]
    qc, kc, vc, xdc, gdc = jnp.split(h_c @ w_in, _cuts(ODD_SIZES), axis=-1)
    ql, kl, vl, xdl, gdl = jnp.split(h_l @ w_in, _cuts(ODD_SIZES), axis=-1)

    def heads(t, nh):
        return t.astype(F32).reshape(t.shape[0], t.shape[1], nh, C_HD)

    kc, vc = heads(kc, C_KV_HEADS), heads(vc, C_KV_HEADS)
    att_l = window_attention(rope_2d(heads(ql, C_HEADS), row, col), rope_2d(heads(kl, C_KV_HEADS), row, col),
                             heads(vl, C_KV_HEADS), kc, vc, sink)
    zero = jnp.zeros((b_, D_WIDTH), F32)
    rg_c, st = rglru_group(xdc, gdc, d_conv_w, d_conv_b, d_w_r, d_b_r, d_w_i, d_b_i, d_lambda, (zero, zero))
    rg_l, _ = rglru_group(xdl, gdl, d_conv_w, d_conv_b, d_w_r, d_b_r, d_w_i, d_b_i, d_lambda, st)
    y_l = jnp.concatenate([att_l, rg_l], axis=-1).astype(h_l.dtype) @ w_out
    y_c = None
    if ctx_out:
        att_c = context_attention(heads(qc, C_HEADS), kc, vc, sink)
        y_c = jnp.concatenate([att_c, rg_c], axis=-1).astype(h_c.dtype) @ w_out
    return y_c, y_l


LANES = 128
ROW_TILE = 256


def _norm_mod(x, nw, shift, scale):
    y = x * lax.rsqrt(jnp.mean(x * x, axis=-1, keepdims=True) + EPS) * nw
    return y * (1.0 + scale) + shift


def _in_proj_kernel(x_ref, nw_ref, sh_ref, sc_ref, w_ref, *o_refs, splits):
    h = _norm_mod(x_ref[0], nw_ref[...], sh_ref[0], sc_ref[0]).astype(BF16)
    off = 0
    for o_ref, s in zip(o_refs, splits):
        o_ref[0] = jnp.dot(h, w_ref[:, off:off + s], preferred_element_type=F32)
        off += s


def in_proj(x, norm_w, shift, scale, w, splits):
    b_, n, d = x.shape
    tm = min(n, ROW_TILE)
    ntot = sum(splits)
    vec = pl.BlockSpec((1, 1, d), lambda b, i: (b, 0, 0))
    return pl.pallas_call(
        functools.partial(_in_proj_kernel, splits=tuple(splits)),
        grid=(b_, n // tm),
        in_specs=[pl.BlockSpec((1, tm, d), lambda b, i: (b, i, 0)),
                  pl.BlockSpec((1, d), lambda b, i: (0, 0)), vec, vec,
                  pl.BlockSpec((d, ntot), lambda b, i: (0, 0))],
        out_specs=[pl.BlockSpec((1, tm, s), lambda b, i: (b, i, 0)) for s in splits],
        out_shape=[jax.ShapeDtypeStruct((b_, n, s), F32) for s in splits],
        compiler_params=pltpu.CompilerParams(dimension_semantics=("arbitrary", "arbitrary"),
                                             vmem_limit_bytes=VMEM_LIMIT_BYTES),
        name="in_proj",
    )(x, norm_w.reshape(1, d), shift[:, None, :], scale[:, None, :], w)


def _post_mixer_kernel(*refs, n_parts):
    parts = refs[:n_parts]
    w_ref, x_ref, g1_ref, nw_ref, sh_ref, sc_ref, rt_ref, x1_ref, h2_ref, aff_ref = refs[n_parts:]
    y = None
    off = 0
    for p_ref in parts:
        k = p_ref.shape[-1]
        t = jnp.dot(p_ref[0].astype(BF16), w_ref[off:off + k, :], preferred_element_type=F32)
        y = t if y is None else y + t
        off += k
    x1 = x_ref[0] + g1_ref[0] * y
    x1_ref[0] = x1
    h2 = _norm_mod(x1, nw_ref[...], sh_ref[0], sc_ref[0])
    h2_ref[0] = h2
    logits = lax.dot_general(rt_ref[...], h2.astype(BF16), (((1,), (1,)), ((), ())),
                             preferred_element_type=F32)
    e = jnp.exp(logits - jnp.max(logits, axis=0, keepdims=True))
    aff_ref[0] = e / jnp.sum(e, axis=0, keepdims=True)


def post_mixer(parts, w_out, x, g1, norm_w, shift, scale, router_t):
    b_, n, d = x.shape
    e_ = router_t.shape[0]
    tm = min(n, ROW_TILE)
    vec = pl.BlockSpec((1, 1, d), lambda b, i: (b, 0, 0))
    tok = lambda k: pl.BlockSpec((1, tm, k), lambda b, i: (b, i, 0))
    return pl.pallas_call(
        functools.partial(_post_mixer_kernel, n_parts=len(parts)),
        grid=(b_, n // tm),
        in_specs=[tok(p.shape[-1]) for p in parts] + [
            pl.BlockSpec(w_out.shape, lambda b, i: (0, 0)), tok(d), vec,
            pl.BlockSpec((1, d), lambda b, i: (0, 0)), vec, vec,
            pl.BlockSpec((e_, d), lambda b, i: (0, 0))],
        out_specs=[tok(d), tok(d), pl.BlockSpec((1, e_, tm), lambda b, i: (b, 0, i))],
        out_shape=[jax.ShapeDtypeStruct((b_, n, d), F32), jax.ShapeDtypeStruct((b_, n, d), F32),
                   jax.ShapeDtypeStruct((b_, e_, n), F32)],
        compiler_params=pltpu.CompilerParams(dimension_semantics=("arbitrary", "arbitrary"),
                                             vmem_limit_bytes=VMEM_LIMIT_BYTES),
        name="post_mixer",
    )(*parts, w_out, x, g1[:, None, :], norm_w.reshape(1, d), shift[:, None, :], scale[:, None, :], router_t)


ROPE_NF = C_HD // 4
NEG_BIG = -1e30


def _rope(t, cos, sin_signed):
    w = t.shape[-1]
    lane = lax.broadcasted_iota(jnp.int32, t.shape, 1)
    partner = jnp.where(lane % (2 * ROPE_NF) < ROPE_NF,
                        pltpu.roll(t, w - ROPE_NF, axis=1), pltpu.roll(t, ROPE_NF, axis=1))
    return t * cos + partner * sin_signed


def _dup_kv_head(t, kh):
    lane = lax.broadcasted_iota(jnp.int32, t.shape, 1)
    rolled = pltpu.roll(t, C_HD, axis=1)
    own_half = (lane < C_HD) if kh == 0 else (lane >= C_HD)
    return jnp.where(own_half, t, rolled)


def _attn_kernel(*refs, banded, n_blocks):
    if banded:
        (q_ref, kp_ref, k_ref, kn_ref, vp_ref, v_ref, vn_ref, cq_ref, sq_ref, cp_ref, sp_ref, cn_ref, sn_ref,
         kc_ref, vc_ref, sink_ref, o_ref) = refs
    else:
        q_ref, kc_ref, vc_ref, sink_ref, o_ref = refs
    nb = pl.program_id(1)
    blk = q_ref.shape[1]
    q = q_ref[0] * (C_HD ** -0.5)
    kc = kc_ref[0]
    vc = vc_ref[0]
    if banded:
        cq, sq = cq_ref[...], sq_ref[...]
        q = _rope(q, jnp.concatenate([cq] * (C_HEADS // C_KV_HEADS), axis=1),
                  jnp.concatenate([sq] * (C_HEADS // C_KV_HEADS), axis=1))
        kb = [_rope(kp_ref[0], cp_ref[...], sp_ref[...]), _rope(k_ref[0], cq, sq), _rope(kn_ref[0], cn_ref[...], sn_ref[...])]
        keys = jnp.concatenate(kb + [kc], axis=0)
        vals = jnp.concatenate([vp_ref[0], v_ref[0], vn_ref[0], vc], axis=0)
        qpos = lax.broadcasted_iota(jnp.int32, (blk, keys.shape[0]), 0)
        kcol = lax.broadcasted_iota(jnp.int32, (blk, keys.shape[0]), 1)
        rel = kcol - blk
        in_band = (jnp.abs(rel - qpos) <= C_WIN) & (rel + nb * blk >= 0) & (rel + nb * blk < n_blocks * blk)
        mask = in_band | (kcol >= 3 * blk)
    else:
        keys, vals, mask = kc, vc, None
    keys = keys.astype(BF16)
    vals = vals.astype(BF16)
    lane_q = lax.broadcasted_iota(jnp.int32, (blk, 2 * C_HD), 1)
    grp = C_HEADS // C_KV_HEADS
    for pair in range(C_HEADS // 2):
        kh = (2 * pair) // grp
        k_dup = _dup_kv_head(keys, kh)
        v_dup = _dup_kv_head(vals, kh)
        q_pair = q[:, pair * 2 * C_HD:(pair + 1) * 2 * C_HD]
        outs = []
        for sub in range(2):
            own = (lane_q < C_HD) if sub == 0 else (lane_q >= C_HD)
            qh = jnp.where(own, q_pair, 0.0).astype(BF16)
            s = lax.dot_general(qh, k_dup, (((1,), (1,)), ((), ())), preferred_element_type=F32)
            if mask is not None:
                s = jnp.where(mask, s, NEG_BIG)
            sink = sink_ref[2 * pair + sub]
            m = jnp.maximum(jnp.max(s, axis=-1, keepdims=True), sink)
            e = jnp.exp(s - m)
            denom = jnp.sum(e, axis=-1, keepdims=True) + jnp.exp(sink - m)
            p = (e / denom).astype(BF16)
            outs.append(jnp.dot(p, v_dup, preferred_element_type=F32))
        o_ref[0, :, pair * 2 * C_HD:(pair + 1) * 2 * C_HD] = jnp.where(lane_q < C_HD, outs[0], outs[1])


def _rope_tables(n):
    rows = n // GRID_W
    row = jnp.repeat(jnp.arange(rows, dtype=jnp.int32), GRID_W).astype(F32)
    col = jnp.tile(jnp.arange(GRID_W, dtype=jnp.int32), rows).astype(F32)
    inv = ROPE_THETA ** (-jnp.arange(ROPE_NF, dtype=F32) / ROPE_NF)
    ang_r, ang_c = row[:, None] * inv, col[:, None] * inv
    cos = jnp.concatenate([jnp.cos(ang_r)] * 2 + [jnp.cos(ang_c)] * 2, axis=-1)
    sin = jnp.concatenate([-jnp.sin(ang_r), jnp.sin(ang_r), -jnp.sin(ang_c), jnp.sin(ang_c)], axis=-1)
    return jnp.concatenate([cos, cos], axis=-1), jnp.concatenate([sin, sin], axis=-1)


def window_attention_pallas(q, k, v, kc, vc, sink, cos, sin):
    b_, n, hq = q.shape
    hk = k.shape[-1]
    lc = kc.shape[1]
    blk = C_BLOCK
    nb_ = n // blk
    cur = lambda b, i: (b, i, 0)
    prv = lambda b, i: (b, jnp.maximum(i - 1, 0), 0)
    nxt = lambda b, i: (b, jnp.minimum(i + 1, nb_ - 1), 0)
    tcur = lambda b, i: (i, 0)
    tprv = lambda b, i: (jnp.maximum(i - 1, 0), 0)
    tnxt = lambda b, i: (jnp.minimum(i + 1, nb_ - 1), 0)
    kspec = lambda m: pl.BlockSpec((1, blk, hk), m)
    tspec = lambda m: pl.BlockSpec((blk, hk), m)
    cspec = pl.BlockSpec((1, lc, hk), lambda b, i: (b, 0, 0))
    return pl.pallas_call(
        functools.partial(_attn_kernel, banded=True, n_blocks=nb_),
        grid=(b_, nb_),
        in_specs=[pl.BlockSpec((1, blk, hq), cur), kspec(prv), kspec(cur), kspec(nxt),
                  kspec(prv), kspec(cur), kspec(nxt),
                  tspec(tcur), tspec(tcur), tspec(tprv), tspec(tprv), tspec(tnxt), tspec(tnxt),
                  cspec, cspec, pl.BlockSpec(memory_space=pltpu.SMEM)],
        out_specs=pl.BlockSpec((1, blk, hq), cur),
        out_shape=jax.ShapeDtypeStruct((b_, n, hq), F32),
        compiler_params=pltpu.CompilerParams(dimension_semantics=("arbitrary", "arbitrary"),
                                             vmem_limit_bytes=VMEM_LIMIT_BYTES),
        name="window_attention",
    )(q, k, k, k, v, v, v, cos, sin, cos, sin, cos, sin, kc, vc, sink)


def context_attention_pallas(q, kc, vc, sink):
    b_, lc, hq = q.shape
    hk = kc.shape[-1]
    blk = min(lc, C_BLOCK)
    cspec = pl.BlockSpec((1, lc, hk), lambda b, i: (b, 0, 0))
    return pl.pallas_call(
        functools.partial(_attn_kernel, banded=False, n_blocks=lc // blk),
        grid=(b_, lc // blk),
        in_specs=[pl.BlockSpec((1, blk, hq), lambda b, i: (b, i, 0)), cspec, cspec,
                  pl.BlockSpec(memory_space=pltpu.SMEM)],
        out_specs=pl.BlockSpec((1, blk, hq), lambda b, i: (b, i, 0)),
        out_shape=jax.ShapeDtypeStruct((b_, lc, hq), F32),
        compiler_params=pltpu.CompilerParams(dimension_semantics=("arbitrary", "arbitrary"),
                                             vmem_limit_bytes=VMEM_LIMIT_BYTES),
        name="context_attention",
    )(q, kc, vc, sink)


SUBLANES = 8
HALO = SUBLANES


def _scan8(a, u, reverse):
    row = lax.broadcasted_iota(jnp.int32, a.shape, 0)
    for s in (1, 2, 4):
        shift = (SUBLANES - s) if reverse else s
        a_sh = pltpu.roll(a, shift, axis=0)
        u_sh = pltpu.roll(u, shift, axis=0)
        valid = (row < SUBLANES - s) if reverse else (row >= s)
        u = jnp.where(valid, a * u_sh + u, u)
        a = jnp.where(valid, a * a_sh, a)
    return a, u


def _gelu_tanh(x):
    return 0.5 * x * (1.0 + jnp.tanh(0.7978845608028654 * (x + 0.044715 * (x * x * x))))


def _rglru_kernel(*refs, tile, n_tiles, reverse, combine):
    if combine:
        (x_ref, xp_ref, xn_ref, cw_ref, cb_ref, wbd_ref, bias_ref, lam_ref, h0_ref, hf_ref, gd_ref,
         o_ref, hl_ref, xbuf, a_s, u_s, carry) = refs
    else:
        (x_ref, xp_ref, xn_ref, cw_ref, cb_ref, wbd_ref, bias_ref, lam_ref, h0_ref,
         o_ref, hl_ref, xbuf, a_s, u_s, carry) = refs
    i = pl.program_id(1)
    t = (n_tiles - 1 - i) if reverse else i
    w = x_ref.shape[-1]

    @pl.when(i == 0)
    def _():
        carry[...] = jnp.broadcast_to(h0_ref[0], (SUBLANES, w))

    xbuf[pl.ds(0, HALO), :] = jnp.where(t > 0, xp_ref[0], 0.0)
    xbuf[pl.ds(HALO, tile), :] = x_ref[0]
    xbuf[pl.ds(HALO + tile, HALO), :] = jnp.where(t < n_tiles - 1, xn_ref[0], 0.0)
    xc = cb_ref[...] + sum(cw_ref[pl.ds(k, 1), :] * xbuf[pl.ds(HALO - 1 + k, tile), :] for k in range(D_CONV))

    z = jnp.dot(xc.astype(BF16), wbd_ref[...], preferred_element_type=F32) + bias_ref[...]
    r = jax.nn.sigmoid(z[:, :w])
    gi = jax.nn.sigmoid(z[:, w:])
    lam = lam_ref[...]
    log_a = (-D_C * jnp.log(1.0 + jnp.exp(-lam))) * r
    a = jnp.exp(log_a)
    a_s[...] = a
    u_s[...] = jnp.sqrt(1.0 - a * a) * gi * xc

    n_groups = tile // SUBLANES

    def step(g, c):
        gg = (n_groups - 1 - g) if reverse else g
        r0 = pl.multiple_of(gg * SUBLANES, SUBLANES)
        ac, uc = _scan8(a_s[pl.ds(r0, SUBLANES), :], u_s[pl.ds(r0, SUBLANES), :], reverse)
        h = ac * c + uc
        o_ref[0, pl.ds(r0, SUBLANES), :] = h
        last = h[0:1, :] if reverse else h[SUBLANES - 1:SUBLANES, :]
        return jnp.broadcast_to(last, (SUBLANES, w))

    c_fin = lax.fori_loop(0, n_groups, step, carry[...], unroll=4 if n_groups % 4 == 0 else 1)
    carry[...] = c_fin

    if combine:
        o_ref[0] = (hf_ref[0] + o_ref[0]) * _gelu_tanh(gd_ref[0])

    @pl.when(i == n_tiles - 1)
    def _():
        hl_ref[0] = c_fin[0:1, :]


def _rglru_sweep(xd, conv_w, conv_b, wbd, bias, lam, h0, hf=None, gd=None, *, reverse):
    b_, n, w = xd.shape
    combine = hf is not None
    tile = min(n, 512)
    n_tiles = n // tile
    blocks_per_tile = tile // HALO
    n_blocks = n // HALO

    def tmap(b, i):
        return (b, (n_tiles - 1 - i) if reverse else i, 0)

    def pmap(b, i):
        t = (n_tiles - 1 - i) if reverse else i
        return (b, jnp.maximum(t * blocks_per_tile - 1, 0), 0)

    def nmap(b, i):
        t = (n_tiles - 1 - i) if reverse else i
        return (b, jnp.minimum((t + 1) * blocks_per_tile, n_blocks - 1), 0)

    const2 = lambda b, i: (0, 0)
    in_specs = [
        pl.BlockSpec((1, tile, w), tmap),
        pl.BlockSpec((1, HALO, w), pmap),
        pl.BlockSpec((1, HALO, w), nmap),
        pl.BlockSpec((D_CONV, w), const2),
        pl.BlockSpec((1, w), const2),
        pl.BlockSpec((w, 2 * w), const2),
        pl.BlockSpec((1, 2 * w), const2),
        pl.BlockSpec((1, w), const2),
        pl.BlockSpec((1, 1, w), lambda b, i: (b, 0, 0)),
    ]
    args = [xd, xd, xd, conv_w, conv_b, wbd, bias, lam, h0]
    if combine:
        in_specs += [pl.BlockSpec((1, tile, w), tmap), pl.BlockSpec((1, tile, w), tmap)]
        args += [hf, gd]
    return pl.pallas_call(
        functools.partial(_rglru_kernel, tile=tile, n_tiles=n_tiles, reverse=reverse, combine=combine),
        grid=(b_, n_tiles),
        in_specs=in_specs,
        out_specs=[pl.BlockSpec((1, tile, w), tmap), pl.BlockSpec((1, 1, w), lambda b, i: (b, 0, 0))],
        out_shape=[jax.ShapeDtypeStruct((b_, n, w), F32), jax.ShapeDtypeStruct((b_, 1, w), F32)],
        scratch_shapes=[pltpu.VMEM((tile + 2 * HALO, w), F32), pltpu.VMEM((tile, w), F32),
                        pltpu.VMEM((tile, w), F32), pltpu.VMEM((SUBLANES, w), F32)],
        compiler_params=pltpu.CompilerParams(dimension_semantics=("arbitrary", "arbitrary"),
                                             vmem_limit_bytes=VMEM_LIMIT_BYTES),
        name="rglru_bwd_combine" if combine else "rglru_fwd",
    )(*args)


def _block_diag(wz):
    eye = jnp.eye(D_BLOCKS, dtype=wz.dtype)
    return jnp.einsum('hij,hg->higj', wz, eye).reshape(D_WIDTH, D_WIDTH)


def rglru_pallas(xd, gd, conv_w, conv_b, w_r, b_r, w_i, b_i, lam, s0):
    cb = conv_b.reshape(1, D_WIDTH)
    outs = []
    for z in range(2):
        wbd = jnp.concatenate([_block_diag(w_r[z]), _block_diag(w_i[z])], axis=1).astype(BF16)
        bias = jnp.concatenate([b_r[z], b_i[z]]).reshape(1, 2 * D_WIDTH)
        outs.append((wbd, bias, lam[z].reshape(1, D_WIDTH)))
    hf, sf = _rglru_sweep(xd, conv_w, cb, *outs[0], s0[0][:, None, :], reverse=False)
    y, sb = _rglru_sweep(xd, conv_w, cb, *outs[1], s0[1][:, None, :], hf, gd, reverse=True)
    return y, (sf[:, 0], sb[:, 0])


def _expert_ffn_kernel(x_ref, g_ref, w1_ref, w3_ref, w2_ref, o_ref):
    x = x_ref[0, 0].astype(BF16)
    a = jnp.dot(x, w1_ref[0], preferred_element_type=F32)
    b = jnp.dot(x, w3_ref[0], preferred_element_type=F32)
    hid = (a * jax.nn.sigmoid(a) * b).astype(BF16)
    y = jnp.dot(hid, w2_ref[0], preferred_element_type=F32)
    o_ref[0, 0] = y * g_ref[0, 0]


def expert_ffn(xs, gate, w1, w3, w2):
    b_, e_, c_, d = xs.shape
    f = w1.shape[-1]
    tm = min(c_, 512)
    grid = (e_, b_, c_ // tm)
    return pl.pallas_call(
        _expert_ffn_kernel,
        grid=grid,
        in_specs=[
            pl.BlockSpec((1, 1, tm, d), lambda e, b, m: (b, e, m, 0)),
            pl.BlockSpec((1, 1, tm, 1), lambda e, b, m: (b, e, m, 0)),
            pl.BlockSpec((1, d, f), lambda e, b, m: (e, 0, 0)),
            pl.BlockSpec((1, d, f), lambda e, b, m: (e, 0, 0)),
            pl.BlockSpec((1, f, d), lambda e, b, m: (e, 0, 0)),
        ],
        out_specs=pl.BlockSpec((1, 1, tm, d), lambda e, b, m: (b, e, m, 0)),
        out_shape=jax.ShapeDtypeStruct((b_, e_, c_, d), F32),
        compiler_params=pltpu.CompilerParams(
            dimension_semantics=("arbitrary", "arbitrary", "arbitrary"),
            vmem_limit_bytes=VMEM_LIMIT_BYTES),
        name="expert_ffn",
    )(xs, gate, w1, w3, w2)


def moe_from_affinity(h, aff, w1, w3, w2):
    b_, n, d = h.shape
    cap = max(1, EC_FACTOR * n // N_EXPERTS)
    gate, idx = lax.top_k(aff, cap)
    xs = jax.vmap(lambda hb, ib: hb[ib])(h, idx)
    out = expert_ffn(xs, gate[..., None], w1, w3, w2)
    return jax.vmap(lambda ob, ib: jnp.zeros((n, d), ob.dtype).at[ib.reshape(-1)].add(ob.reshape(-1, d)))(out, idx)


AB_PAD = LANES
EVEN_GROUPS = (A_QKV, A_HEADS * A_DV, AB_PAD, B_HEADS * B_DK, B_HEADS * B_DV, 2 * B_HEADS * B_DK, B_HEADS * B_DV)


def _even_w_in(w):
    c = _cuts(EVEN_SIZES)
    ab = jnp.pad(w[:, c[1]:c[3]], ((0, 0), (0, AB_PAD - (c[3] - c[1]))))
    return jnp.concatenate([w[:, :c[1]], ab, w[:, c[3]:]], axis=1).astype(BF16)


def _even_core(outs, conv_w, a_log, dt_bias, a_norm_w, lb, b_norm_w, st_a, st_b):
    qkv, ga, ab, qb, ib, fb, gb = outs
    al, be = ab[..., :2 * A_HEADS], ab[..., 2 * A_HEADS:4 * A_HEADS]
    oa, st_a = delta_group(qkv, ga, al, be, conv_w, a_log, dt_bias, a_norm_w, st_a)
    ob, st_b = hgrn2_group(qb, ib, fb, gb, lb, b_norm_w, st_b)
    return [oa, ob], st_a, st_b


def kernel(x, c, ctx, c_ctx, w_mod, b_mod, norm1_w, norm2_w, final_norm_w,
           ev_w_in, ev_w_out, a_conv_w, a_log, a_dt_bias, a_norm_w, b_lb_logits, b_norm_w,
           od_w_in, od_w_out, c_sink, d_conv_w, d_conv_b, d_w_r, d_b_r, d_w_i, d_b_i, d_lambda,
           moe_router, moe_w1, moe_w3, moe_w2):
    b_, n, d = x.shape
    cos, sin = _rope_tables(n)
    lb_all = jnp.cumsum(jax.nn.softmax(b_lb_logits.astype(F32), axis=0), axis=0)
    lb_all = lb_all - lb_all[0:1]
    for l in range(DEPTH):
        last = l == DEPTH - 1
        j = l // 2
        w1b, w3b, w2b = moe_w1[l].astype(BF16), moe_w3[l].astype(BF16), moe_w2[l].astype(BF16)
        router_t = moe_router[l].T.astype(BF16)
        mod = jax.nn.silu(c) @ w_mod[l] + b_mod[l]
        mod_c = jnp.broadcast_to(jax.nn.silu(c_ctx) @ w_mod[l] + b_mod[l], (b_, 6 * d))
        sh1, sc1, g1, sh2, sc2, g2 = jnp.split(mod, 6, axis=-1)
        csh1, csc1, cg1, csh2, csc2, cg2 = jnp.split(mod_c, 6, axis=-1)
        if l % 2 == 0:
            w_in, w_out = _even_w_in(ev_w_in[j]), ev_w_out[j].astype(BF16)
            zero_a = jnp.zeros((b_, A_HEADS, A_DK, A_DV), F32)
            zero_b = jnp.zeros((b_, B_HEADS, B_DK, B_DV), F32)
            pars = (a_conv_w[j], a_log[j], a_dt_bias[j], a_norm_w[j], lb_all[j], b_norm_w[j])
            parts_c, st_a, st_b = _even_core(in_proj(ctx, norm1_w[l], csh1, csc1, w_in, EVEN_GROUPS), *pars,
                                             (zero_a, zero_a), (zero_b, zero_b))
            parts_l, _, _ = _even_core(in_proj(x, norm1_w[l], sh1, sc1, w_in, EVEN_GROUPS), *pars, st_a, st_b)
        else:
            w_in, w_out = od_w_in[j].astype(BF16), od_w_out[j].astype(BF16)
            qc, kc, vc, xdc, gdc = in_proj(ctx, norm1_w[l], csh1, csc1, w_in, ODD_SIZES)
            ql, kl, vl, xdl, gdl = in_proj(x, norm1_w[l], sh1, sc1, w_in, ODD_SIZES)
            att_l = window_attention_pallas(ql, kl, vl, kc, vc, c_sink[j], cos, sin)
            rg_pars = (d_conv_w[j], d_conv_b[j], d_w_r[j], d_b_r[j], d_w_i[j], d_b_i[j], d_lambda[j])
            zero = jnp.zeros((b_, D_WIDTH), F32)
            rg_c, st = rglru_pallas(xdc, gdc, *rg_pars, (zero, zero))
            rg_l, _ = rglru_pallas(xdl, gdl, *rg_pars, st)
            parts_l = [att_l, rg_l]
            if not last:
                parts_c = [context_attention_pallas(qc, kc, vc, c_sink[j]), rg_c]
        x1, h2, aff = post_mixer(parts_l, w_out, x, g1, norm2_w[l], sh2, sc2, router_t)
        x = x1 + g2[:, None, :] * moe_from_affinity(h2, aff, w1b, w3b, w2b)
        if not last:
            c1, hc2, affc = post_mixer(parts_c, w_out, ctx, cg1, norm2_w[l], csh2, csc2, router_t)
            ctx = c1 + cg2[:, None, :] * moe_from_affinity(hc2, affc, w1b, w3b, w2b)
    return rmsnorm(x, final_norm_w)
```

```python
import functools

import jax
import jax.numpy as jnp
from jax import lax
from jax.experimental import pallas as pl
from jax.experimental.pallas import tpu as pltpu

D_MODEL = 1024
DEPTH = 4
GRID_W = 64
EPS = 1e-6
F32 = jnp.float32
BF16 = jnp.bfloat16
HI = lax.Precision.HIGHEST

A_HEADS = 4
A_DK = 128
A_DV = 128
A_CONV = 4
A_CHUNK = 64
A_QKV = 2 * A_HEADS * A_DK + A_HEADS * A_DV
B_HEADS = 4
B_DK = 128
B_DV = 128
B_CHUNK = 64
C_HEADS = 8
C_KV_HEADS = 2
C_HD = 64
C_WIN = 128
C_BLOCK = 128
ROPE_THETA = 10000.0
D_WIDTH = 512
D_BLOCKS = 8
D_BW = D_WIDTH // D_BLOCKS
D_CONV = 4
D_C = 8.0
N_EXPERTS = 16
EXPERT_FF = 1024
EC_FACTOR = 2

EVEN_SIZES = (A_QKV, A_HEADS * A_DV, 2 * A_HEADS, 2 * A_HEADS,
              B_HEADS * B_DK, B_HEADS * B_DV, 2 * B_HEADS * B_DK, B_HEADS * B_DV)
ODD_SIZES = (C_HEADS * C_HD, C_KV_HEADS * C_HD, C_KV_HEADS * C_HD, D_WIDTH, D_WIDTH)

LANES = 128
SUBLANES = 8
VMEM_LIMIT_BYTES = 56 * 1024 * 1024

ROW_TILE = 256
HALO = SUBLANES


def _cuts(sizes):
    out, acc = [], 0
    for s in sizes[:-1]:
        acc += s
        out.append(acc)
    return out


def _params(n_axes):
    return pltpu.CompilerParams(dimension_semantics=("arbitrary",) * n_axes, vmem_limit_bytes=VMEM_LIMIT_BYTES)


def _dot(a, b, precision=None):
    return jnp.dot(a, b, preferred_element_type=F32, precision=precision)


def _dot_nt(a, b, precision=None):
    return lax.dot_general(a, b, (((1,), (1,)), ((), ())), preferred_element_type=F32, precision=precision)


def _dot_tn(a, b):
    return lax.dot_general(a, b, (((0,), (0,)), ((), ())), preferred_element_type=F32)


def _silu(x):
    return x * jax.nn.sigmoid(x)


MOD_COLS = 1536


def _mod_kernel(c_ref, w_ref, b_ref, o_ref):
    o_ref[0] = _dot(_silu(c_ref[...]), w_ref[0]) + b_ref[0]


def modulation(c_rows, w_mod, b_mod):
    r, d = c_rows.shape
    depth, _, wide = w_mod.shape
    return pl.pallas_call(
        _mod_kernel,
        grid=(depth, wide // MOD_COLS),
        in_specs=[pl.BlockSpec((r, d), lambda l, j: (0, 0)),
                  pl.BlockSpec((1, d, MOD_COLS), lambda l, j: (l, 0, j)),
                  pl.BlockSpec((1, 1, MOD_COLS), lambda l, j: (l, 0, j))],
        out_specs=pl.BlockSpec((1, r, MOD_COLS), lambda l, j: (l, 0, j)),
        out_shape=jax.ShapeDtypeStruct((depth, r, wide), F32),
        compiler_params=_params(2),
        name="modulation",
    )(c_rows, w_mod, b_mod[:, None, :])


def _norm_mod(x, nw, shift, scale):
    y = x * lax.rsqrt(jnp.mean(x * x, axis=-1, keepdims=True) + EPS) * nw
    return y * (1.0 + scale) + shift


def _in_proj_kernel(x_ref, nw_ref, sh_ref, sc_ref, w_ref, *o_refs, splits):
    h = _norm_mod(x_ref[0], nw_ref[...], sh_ref[0], sc_ref[0]).astype(BF16)
    off = 0
    for o_ref, s in zip(o_refs, splits):
        o_ref[0] = _dot(h, w_ref[:, off:off + s])
        off += s


def in_proj(x, norm_w, shift, scale, w, splits):
    b_, n, d = x.shape
    tm = min(n, ROW_TILE)
    ntot = sum(splits)
    vec = pl.BlockSpec((1, 1, d), lambda b, i: (b, 0, 0))
    return pl.pallas_call(
        functools.partial(_in_proj_kernel, splits=tuple(splits)),
        grid=(b_, n // tm),
        in_specs=[pl.BlockSpec((1, tm, d), lambda b, i: (b, i, 0)),
                  pl.BlockSpec((1, d), lambda b, i: (0, 0)), vec, vec,
                  pl.BlockSpec((d, ntot), lambda b, i: (0, 0))],
        out_specs=[pl.BlockSpec((1, tm, s), lambda b, i: (b, i, 0)) for s in splits],
        out_shape=[jax.ShapeDtypeStruct((b_, n, s), F32) for s in splits],
        compiler_params=_params(2),
        name="in_proj",
    )(x, norm_w.reshape(1, d), shift[:, None, :], scale[:, None, :], w)


def _post_mixer_kernel(*refs, n_parts):
    parts = refs[:n_parts]
    w_ref, x_ref, g1_ref, nw_ref, sh_ref, sc_ref, rt_ref, x1_ref, h2_ref, aff_ref = refs[n_parts:]
    y = None
    off = 0
    for p_ref in parts:
        k = p_ref.shape[-1]
        t = _dot(p_ref[0].astype(BF16), w_ref[off:off + k, :])
        y = t if y is None else y + t
        off += k
    x1 = x_ref[0] + g1_ref[0] * y
    x1_ref[0] = x1
    h2 = _norm_mod(x1, nw_ref[...], sh_ref[0], sc_ref[0])
    h2_ref[0] = h2
    logits = _dot_nt(rt_ref[...], h2.astype(BF16))
    e = jnp.exp(logits - jnp.max(logits, axis=0, keepdims=True))
    aff_ref[0] = e / jnp.sum(e, axis=0, keepdims=True)


def post_mixer(parts, w_out, x, g1, norm_w, shift, scale, router_t):
    b_, n, d = x.shape
    e_ = router_t.shape[0]
    tm = min(n, ROW_TILE)
    vec = pl.BlockSpec((1, 1, d), lambda b, i: (b, 0, 0))
    tok = lambda k: pl.BlockSpec((1, tm, k), lambda b, i: (b, i, 0))
    return pl.pallas_call(
        functools.partial(_post_mixer_kernel, n_parts=len(parts)),
        grid=(b_, n // tm),
        in_specs=[tok(p.shape[-1]) for p in parts] + [
            pl.BlockSpec(w_out.shape, lambda b, i: (0, 0)), tok(d), vec,
            pl.BlockSpec((1, d), lambda b, i: (0, 0)), vec, vec,
            pl.BlockSpec((e_, d), lambda b, i: (0, 0))],
        out_specs=[tok(d), tok(d), pl.BlockSpec((1, e_, tm), lambda b, i: (b, 0, i))],
        out_shape=[jax.ShapeDtypeStruct((b_, n, d), F32), jax.ShapeDtypeStruct((b_, n, d), F32),
                   jax.ShapeDtypeStruct((b_, e_, n), F32)],
        compiler_params=_params(2),
        name="post_mixer",
    )(*parts, w_out, x, g1[:, None, :], norm_w.reshape(1, d), shift[:, None, :], scale[:, None, :], router_t)


def _final_kernel(x_ref, m_ref, g_ref, nw_ref, o_ref):
    x = x_ref[0] + g_ref[0] * m_ref[0]
    o_ref[0] = x * lax.rsqrt(jnp.mean(x * x, axis=-1, keepdims=True) + EPS) * nw_ref[...]


def final_norm(x1, m, g2, norm_w):
    b_, n, d = x1.shape
    tm = min(n, 2 * ROW_TILE)
    tok = pl.BlockSpec((1, tm, d), lambda b, i: (b, i, 0))
    return pl.pallas_call(
        _final_kernel,
        grid=(b_, n // tm),
        in_specs=[tok, tok, pl.BlockSpec((1, 1, d), lambda b, i: (b, 0, 0)), pl.BlockSpec((1, d), lambda b, i: (0, 0))],
        out_specs=tok,
        out_shape=jax.ShapeDtypeStruct((b_, n, d), F32),
        compiler_params=_params(2),
        name="final_norm",
    )(x1, m, g2[:, None, :], norm_w.reshape(1, d))


CHUNK = A_CHUNK
SUB = 16
EVEN_TILE = 256
HA = A_HEADS * A_DK
HB = B_HEADS * B_DK
AB_PAD = LANES
EVEN_GROUPS = (A_QKV, A_HEADS * A_DV, AB_PAD, HB, B_HEADS * B_DV, 2 * HB, B_HEADS * B_DV)


def _shift_rows(x, s, reverse):
    if s == 0:
        return x
    return pltpu.roll(x, (x.shape[0] - s) if reverse else s, axis=0)


def _even_kernel(*refs, tile, n_tiles, reverse, combine):
    if combine:
        (qkv_ref, qkvp_ref, qkvn_ref, ab_ref, qb_ref, ib_ref, fb_ref, cw_ref, exp_ref, alog_ref, dtb_ref, lb_ref,
         sa0_ref, sb0_ref, of_ref, ga_ref, gb_ref, anw_ref, bnw_ref,
         o_ref, sa_ref, sb_ref, xbuf, q_s, k_s, v_s, g_s, bt_s, hq_s, hk_s, hlf_s, st_a, st_b) = refs
    else:
        (qkv_ref, qkvp_ref, qkvn_ref, ab_ref, qb_ref, ib_ref, fb_ref, cw_ref, exp_ref, alog_ref, dtb_ref, lb_ref,
         sa0_ref, sb0_ref,
         o_ref, sa_ref, sb_ref, xbuf, q_s, k_s, v_s, g_s, bt_s, hq_s, hk_s, hlf_s, st_a, st_b) = refs
    i = pl.program_id(1)
    t = (n_tiles - 1 - i) if reverse else i

    @pl.when(i == 0)
    def _():
        st_a[...] = sa0_ref[0]
        st_b[...] = sb0_ref[0]

    xbuf[pl.ds(0, HALO), :] = jnp.where(t > 0, qkvp_ref[0], 0.0)
    xbuf[pl.ds(HALO, tile), :] = qkv_ref[0]
    xbuf[pl.ds(HALO + tile, HALO), :] = jnp.where(t < n_tiles - 1, qkvn_ref[0], 0.0)
    conv = _silu(sum(cw_ref[pl.ds(k, 1), :] * xbuf[pl.ds(HALO - 1 + k, tile), :] for k in range(A_CONV)))
    for h in range(A_HEADS):
        qh = conv[:, h * A_DK:(h + 1) * A_DK]
        kh = conv[:, HA + h * A_DK:HA + (h + 1) * A_DK]
        q_s[:, h * A_DK:(h + 1) * A_DK] = qh * (lax.rsqrt(jnp.sum(qh * qh, axis=-1, keepdims=True) + EPS)
                                               * (A_DK ** -0.5))
        k_s[:, h * A_DK:(h + 1) * A_DK] = kh * lax.rsqrt(jnp.sum(kh * kh, axis=-1, keepdims=True) + EPS)
    v_s[...] = conv[:, 2 * HA:]
    ab_x = _dot(ab_ref[0], exp_ref[...], HI)
    g_s[...] = -jnp.exp(alog_ref[...]) * jax.nn.softplus(ab_x[:, :HA] + dtb_ref[...])
    bt_s[...] = jax.nn.sigmoid(ab_x[:, HA:])
    hq_s[...] = _silu(qb_ref[0])
    lb = lb_ref[...]
    fg = lb + (1.0 - lb) * jax.nn.sigmoid(fb_ref[0])
    hlf_s[...] = jnp.log(fg)
    hk_s[...] = 1.0 - fg

    L = CHUNK
    ii = lax.broadcasted_iota(jnp.int32, (L, L), 0)
    jj = lax.broadcasted_iota(jnp.int32, (L, L), 1)
    incl = (jj >= ii) if reverse else (jj <= ii)
    strict = (jj > ii) if reverse else (jj < ii)
    same_blk = (ii // SUB) == (jj // SUB)
    eye = (ii == jj).astype(F32)
    tri = incl.astype(F32)
    tri_blk = (incl & same_blk).astype(F32)
    ones_blk = same_blk.astype(F32)
    ones_ll = jnp.ones((L, L), F32)
    row_in_blk = lax.broadcasted_iota(jnp.int32, (L, B_DK), 0) % SUB
    last_row = 0 if reverse else L - 1
    n_chunks = tile // L

    def chunk_body(ci, carry):
        c = (n_chunks - 1 - ci) if reverse else ci
        r0 = pl.multiple_of(c * L, L)
        rows = pl.ds(r0, L)
        cum_all = _dot(tri, g_s[rows, :], HI)
        for h in range(A_HEADS):
            hs = slice(h * A_DK, (h + 1) * A_DK)
            q, k, v = q_s[rows, hs], k_s[rows, hs], v_s[rows, hs]
            beta = bt_s[rows, hs]
            cum = cum_all[:, hs]
            cum_ll = cum[:, :L]
            r_ll = _dot(ones_ll, eye * cum_ll, HI)
            decay = jnp.where(incl, jnp.exp(jnp.where(incl, cum_ll - r_ll, 0.0)), 0.0)
            kb = k * beta
            m = jnp.where(strict, _dot_nt(kb, k) * decay, 0.0)
            qk = _dot_nt(q, k) * decay
            ecum = jnp.exp(cum)
            rhs = jnp.concatenate([v * beta, kb * ecum], axis=1)
            p = -m
            tinv = eye + p
            for _ in range(5):
                p = _dot(p, p, HI)
                tinv = tinv + _dot(tinv, p, HI)
            sol = _dot(tinv, rhs, HI)
            u, w = sol[:, :A_DV], sol[:, A_DV:]
            total = cum[last_row:last_row + 1, :]
            s = st_a[h]
            v_new = u - _dot(w, s)
            o_ref[0, rows, hs] = _dot(q * ecum, s) + _dot(qk, v_new)
            st_a[h] = s * jnp.exp(total) + _dot_tn(k * jnp.exp(total - cum), v_new)
        lf_all = hlf_s[rows, :]
        b_all = _dot(tri_blk, lf_all, HI)
        e_all = _dot(ones_blk, lf_all, HI)
        for h in range(B_HEADS):
            hs = slice(h * B_DK, (h + 1) * B_DK)
            q, k, v = hq_s[rows, hs], hk_s[rows, hs], ib_ref[0, rows, hs]
            b, e = b_all[:, hs], e_all[:, hs]
            qs = q * jnp.exp(b)
            ks = k * jnp.exp(e - b)
            od = jnp.zeros((L, B_DV), F32)
            for s_ in range(SUB):
                valid = (row_in_blk + s_ < SUB) if reverse else (row_in_blk >= s_)
                kk, bb, vv = _shift_rows(k, s_, reverse), _shift_rows(b, s_, reverse), _shift_rows(v, s_, reverse)
                prod = jnp.where(valid, q * kk * jnp.exp(jnp.where(valid, b - bb, 0.0)), 0.0)
                od = od + jnp.sum(prod, axis=-1, keepdims=True) * vv
            st = st_b[h]
            outs = [None] * (L // SUB)
            for blk in (range(L // SUB - 1, -1, -1) if reverse else range(L // SUB)):
                bs = slice(blk * SUB, (blk + 1) * SUB)
                outs[blk] = _dot_nt(qs[bs], st) + od[bs]
                st = st * jnp.exp(e[blk * SUB:blk * SUB + 1, :]) + _dot_tn(v[bs], ks[bs])
            st_b[h] = st
            o_ref[0, rows, pl.ds(HA + h * B_DV, B_DV)] = jnp.concatenate(outs, axis=0)
        return carry

    lax.fori_loop(0, n_chunks, chunk_body, 0)

    if combine:
        tot = of_ref[0] + o_ref[0]
        for h in range(A_HEADS + B_HEADS):
            hs = slice(h * A_DV, (h + 1) * A_DV)
            x = tot[:, hs]
            nw = anw_ref[...] if h < A_HEADS else bnw_ref[...]
            gate = ga_ref[0, :, hs] if h < A_HEADS else gb_ref[0, :, pl.ds((h - A_HEADS) * B_DV, B_DV)]
            y = x * lax.rsqrt(jnp.mean(x * x, axis=-1, keepdims=True) + EPS) * nw
            o_ref[0, :, hs] = y * _silu(gate)

    @pl.when(i == n_tiles - 1)
    def _():
        sa_ref[0] = st_a[...]
        sb_ref[0] = st_b[...]


def _even_sweep(outs, z, conv_w, a_log, dt_bias, lb, s_a, s_b, of=None, a_norm_w=None, b_norm_w=None):
    qkv, ga, ab, qb, ib, fb, gb = outs
    b_, n, _ = qkv.shape
    reverse = z == 1
    combine = of is not None
    tile = min(n, EVEN_TILE)
    n_tiles = n // tile
    bpt = tile // HALO
    nblk = n // HALO
    tidx = lambda i: (n_tiles - 1 - i) if reverse else i
    tmap = lambda b, i: (b, tidx(i), 0)
    pmap = lambda b, i: (b, jnp.maximum(tidx(i) * bpt - 1, 0), 0)
    nmap = lambda b, i: (b, jnp.minimum((tidx(i) + 1) * bpt, nblk - 1), 0)
    zmap = lambda b, i: (b, tidx(i), z)
    const2 = lambda b, i: (0, 0)
    smap = lambda b, i: (b, 0, 0, 0)
    src = jnp.arange(AB_PAD)[:, None]
    dst_head = (jnp.arange(2 * HA)[None, :] % HA) // A_DK
    is_beta = jnp.arange(2 * HA)[None, :] >= HA
    expand = (src == jnp.where(is_beta, 2 * A_HEADS, 0) + z * A_HEADS + dst_head).astype(F32)
    rep = lambda p: jnp.repeat(p, A_DK).reshape(1, HA)
    in_specs = [
        pl.BlockSpec((1, tile, A_QKV), tmap), pl.BlockSpec((1, HALO, A_QKV), pmap), pl.BlockSpec((1, HALO, A_QKV), nmap),
        pl.BlockSpec((1, tile, AB_PAD), tmap), pl.BlockSpec((1, tile, HB), tmap), pl.BlockSpec((1, tile, HB), tmap),
        pl.BlockSpec((1, tile, HB), zmap),
        pl.BlockSpec((A_CONV, A_QKV), const2), pl.BlockSpec((AB_PAD, 2 * HA), const2),
        pl.BlockSpec((1, HA), const2), pl.BlockSpec((1, HA), const2), pl.BlockSpec((1, HB), const2),
        pl.BlockSpec((1, A_HEADS, A_DK, A_DV), smap), pl.BlockSpec((1, B_HEADS, B_DV, B_DK), smap),
    ]
    args = [qkv, qkv, qkv, ab, qb, ib, fb, conv_w, expand, rep(a_log[z]), rep(dt_bias[z]), lb[z].reshape(1, HB), s_a, s_b]
    if combine:
        in_specs += [pl.BlockSpec((1, tile, HA + HB), tmap), pl.BlockSpec((1, tile, HA), tmap),
                     pl.BlockSpec((1, tile, HB), tmap), pl.BlockSpec((1, A_DV), const2), pl.BlockSpec((1, B_DV), const2)]
        args += [of, ga, gb, a_norm_w.reshape(1, A_DV), b_norm_w.reshape(1, B_DV)]
    w5 = lambda: pltpu.VMEM((tile, HA), F32)
    return pl.pallas_call(
        functools.partial(_even_kernel, tile=tile, n_tiles=n_tiles, reverse=reverse, combine=combine),
        grid=(b_, n_tiles),
        in_specs=in_specs,
        out_specs=[pl.BlockSpec((1, tile, HA + HB), tmap),
                   pl.BlockSpec((1, A_HEADS, A_DK, A_DV), smap), pl.BlockSpec((1, B_HEADS, B_DV, B_DK), smap)],
        out_shape=[jax.ShapeDtypeStruct((b_, n, HA + HB), F32), jax.ShapeDtypeStruct(s_a.shape, F32),
                   jax.ShapeDtypeStruct(s_b.shape, F32)],
        scratch_shapes=[pltpu.VMEM((tile + 2 * HALO, A_QKV), F32), w5(), w5(), w5(), w5(), w5(), w5(), w5(), w5(),
                        pltpu.VMEM((A_HEADS, A_DK, A_DV), F32), pltpu.VMEM((B_HEADS, B_DV, B_DK), F32)],
        compiler_params=_params(2),
        name="even_bwd_combine" if combine else "even_fwd",
    )(*args)


def even_mixer_pallas(outs, conv_w, a_log, dt_bias, a_norm_w, lb, b_norm_w, states):
    (saf, sbf), (sab, sbb) = states
    o_f, saf, sbf = _even_sweep(outs, 0, conv_w, a_log, dt_bias, lb, saf, sbf)
    o, sab, sbb = _even_sweep(outs, 1, conv_w, a_log, dt_bias, lb, sab, sbb, o_f, a_norm_w, b_norm_w)
    return o, ((saf, sbf), (sab, sbb))


def _even_w_in(w):
    c = _cuts(EVEN_SIZES)
    ab = jnp.pad(w[:, c[1]:c[3]], ((0, 0), (0, AB_PAD - (c[3] - c[1]))))
    return jnp.concatenate([w[:, :c[1]], ab, w[:, c[3]:]], axis=1).astype(BF16)


ROPE_NF = C_HD // 4
NEG_BIG = -1e30


def _rope(t, cos, sin_signed):
    w = t.shape[-1]
    lane = lax.broadcasted_iota(jnp.int32, t.shape, 1)
    partner = jnp.where(lane % (2 * ROPE_NF) < ROPE_NF,
                        pltpu.roll(t, w - ROPE_NF, axis=1), pltpu.roll(t, ROPE_NF, axis=1))
    return t * cos + partner * sin_signed


def _dup_kv_head(t, kh):
    lane = lax.broadcasted_iota(jnp.int32, t.shape, 1)
    rolled = pltpu.roll(t, C_HD, axis=1)
    own_half = (lane < C_HD) if kh == 0 else (lane >= C_HD)
    return jnp.where(own_half, t, rolled)


def _attn_kernel(*refs, banded, n_blocks):
    if banded:
        (q_ref, kp_ref, k_ref, kn_ref, vp_ref, v_ref, vn_ref, cq_ref, sq_ref, cp_ref, sp_ref, cn_ref, sn_ref,
         kc_ref, vc_ref, sink_ref, o_ref) = refs
    else:
        q_ref, kc_ref, vc_ref, sink_ref, o_ref = refs
    nb = pl.program_id(1)
    blk = q_ref.shape[1]
    q = q_ref[0] * (C_HD ** -0.5)
    kc = kc_ref[0]
    vc = vc_ref[0]
    if banded:
        cq, sq = cq_ref[...], sq_ref[...]
        q = _rope(q, jnp.concatenate([cq] * (C_HEADS // C_KV_HEADS), axis=1),
                  jnp.concatenate([sq] * (C_HEADS // C_KV_HEADS), axis=1))
        kb = [_rope(kp_ref[0], cp_ref[...], sp_ref[...]), _rope(k_ref[0], cq, sq), _rope(kn_ref[0], cn_ref[...], sn_ref[...])]
        keys = jnp.concatenate(kb + [kc], axis=0)
        vals = jnp.concatenate([vp_ref[0], v_ref[0], vn_ref[0], vc], axis=0)
        qpos = lax.broadcasted_iota(jnp.int32, (blk, keys.shape[0]), 0)
        kcol = lax.broadcasted_iota(jnp.int32, (blk, keys.shape[0]), 1)
        rel = kcol - blk
        in_band = (jnp.abs(rel - qpos) <= C_WIN) & (rel + nb * blk >= 0) & (rel + nb * blk < n_blocks * blk)
        mask = in_band | (kcol >= 3 * blk)
    else:
        keys, vals, mask = kc, vc, None
    keys = keys.astype(BF16)
    vals = vals.astype(BF16)
    lane_q = lax.broadcasted_iota(jnp.int32, (blk, 2 * C_HD), 1)
    grp = C_HEADS // C_KV_HEADS
    for pair in range(C_HEADS // 2):
        kh = (2 * pair) // grp
        k_dup = _dup_kv_head(keys, kh)
        v_dup = _dup_kv_head(vals, kh)
        q_pair = q[:, pair * 2 * C_HD:(pair + 1) * 2 * C_HD]
        outs = []
        for sub in range(2):
            own = (lane_q < C_HD) if sub == 0 else (lane_q >= C_HD)
            qh = jnp.where(own, q_pair, 0.0).astype(BF16)
            s = _dot_nt(qh, k_dup)
            if mask is not None:
                s = jnp.where(mask, s, NEG_BIG)
            sink = sink_ref[2 * pair + sub]
            m = jnp.maximum(jnp.max(s, axis=-1, keepdims=True), sink)
            e = jnp.exp(s - m)
            denom = jnp.sum(e, axis=-1, keepdims=True) + jnp.exp(sink - m)
            p = (e / denom).astype(BF16)
            outs.append(_dot(p, v_dup))
        o_ref[0, :, pair * 2 * C_HD:(pair + 1) * 2 * C_HD] = jnp.where(lane_q < C_HD, outs[0], outs[1])


def _rope_tables(n):
    rows = n // GRID_W
    row = jnp.repeat(jnp.arange(rows, dtype=jnp.int32), GRID_W).astype(F32)
    col = jnp.tile(jnp.arange(GRID_W, dtype=jnp.int32), rows).astype(F32)
    inv = ROPE_THETA ** (-jnp.arange(ROPE_NF, dtype=F32) / ROPE_NF)
    ang_r, ang_c = row[:, None] * inv, col[:, None] * inv
    cos = jnp.concatenate([jnp.cos(ang_r)] * 2 + [jnp.cos(ang_c)] * 2, axis=-1)
    sin = jnp.concatenate([-jnp.sin(ang_r), jnp.sin(ang_r), -jnp.sin(ang_c), jnp.sin(ang_c)], axis=-1)
    return jnp.concatenate([cos, cos], axis=-1), jnp.concatenate([sin, sin], axis=-1)


def window_attention_pallas(q, k, v, kc, vc, sink, cos, sin):
    b_, n, hq = q.shape
    hk = k.shape[-1]
    lc = kc.shape[1]
    blk = C_BLOCK
    nb_ = n // blk
    cur = lambda b, i: (b, i, 0)
    prv = lambda b, i: (b, jnp.maximum(i - 1, 0), 0)
    nxt = lambda b, i: (b, jnp.minimum(i + 1, nb_ - 1), 0)
    tcur = lambda b, i: (i, 0)
    tprv = lambda b, i: (jnp.maximum(i - 1, 0), 0)
    tnxt = lambda b, i: (jnp.minimum(i + 1, nb_ - 1), 0)
    kspec = lambda m: pl.BlockSpec((1, blk, hk), m)
    tspec = lambda m: pl.BlockSpec((blk, hk), m)
    cspec = pl.BlockSpec((1, lc, hk), lambda b, i: (b, 0, 0))
    return pl.pallas_call(
        functools.partial(_attn_kernel, banded=True, n_blocks=nb_),
        grid=(b_, nb_),
        in_specs=[pl.BlockSpec((1, blk, hq), cur), kspec(prv), kspec(cur), kspec(nxt),
                  kspec(prv), kspec(cur), kspec(nxt),
                  tspec(tcur), tspec(tcur), tspec(tprv), tspec(tprv), tspec(tnxt), tspec(tnxt),
                  cspec, cspec, pl.BlockSpec(memory_space=pltpu.SMEM)],
        out_specs=pl.BlockSpec((1, blk, hq), cur),
        out_shape=jax.ShapeDtypeStruct((b_, n, hq), F32),
        compiler_params=_params(2),
        name="window_attention",
    )(q, k, k, k, v, v, v, cos, sin, cos, sin, cos, sin, kc, vc, sink)


def context_attention_pallas(q, kc, vc, sink):
    b_, lc, hq = q.shape
    hk = kc.shape[-1]
    blk = min(lc, C_BLOCK)
    cspec = pl.BlockSpec((1, lc, hk), lambda b, i: (b, 0, 0))
    return pl.pallas_call(
        functools.partial(_attn_kernel, banded=False, n_blocks=lc // blk),
        grid=(b_, lc // blk),
        in_specs=[pl.BlockSpec((1, blk, hq), lambda b, i: (b, i, 0)), cspec, cspec,
                  pl.BlockSpec(memory_space=pltpu.SMEM)],
        out_specs=pl.BlockSpec((1, blk, hq), lambda b, i: (b, i, 0)),
        out_shape=jax.ShapeDtypeStruct((b_, lc, hq), F32),
        compiler_params=_params(2),
        name="context_attention",
    )(q, kc, vc, sink)


def _scan8(a, u, reverse):
    row = lax.broadcasted_iota(jnp.int32, a.shape, 0)
    for s in (1, 2, 4):
        shift = (SUBLANES - s) if reverse else s
        a_sh = pltpu.roll(a, shift, axis=0)
        u_sh = pltpu.roll(u, shift, axis=0)
        valid = (row < SUBLANES - s) if reverse else (row >= s)
        u = jnp.where(valid, a * u_sh + u, u)
        a = jnp.where(valid, a * a_sh, a)
    return a, u


def _gelu_tanh(x):
    return 0.5 * x * (1.0 + jnp.tanh(0.7978845608028654 * (x + 0.044715 * (x * x * x))))


def _rglru_kernel(*refs, tile, n_tiles, reverse, combine):
    if combine:
        (x_ref, xp_ref, xn_ref, cw_ref, cb_ref, wbd_ref, bias_ref, lam_ref, h0_ref, hf_ref, gd_ref,
         o_ref, hl_ref, xbuf, a_s, u_s, carry) = refs
    else:
        (x_ref, xp_ref, xn_ref, cw_ref, cb_ref, wbd_ref, bias_ref, lam_ref, h0_ref,
         o_ref, hl_ref, xbuf, a_s, u_s, carry) = refs
    i = pl.program_id(1)
    t = (n_tiles - 1 - i) if reverse else i
    w = x_ref.shape[-1]

    @pl.when(i == 0)
    def _():
        carry[...] = jnp.broadcast_to(h0_ref[0], (SUBLANES, w))

    xbuf[pl.ds(0, HALO), :] = jnp.where(t > 0, xp_ref[0], 0.0)
    xbuf[pl.ds(HALO, tile), :] = x_ref[0]
    xbuf[pl.ds(HALO + tile, HALO), :] = jnp.where(t < n_tiles - 1, xn_ref[0], 0.0)
    xc = cb_ref[...] + sum(cw_ref[pl.ds(k, 1), :] * xbuf[pl.ds(HALO - 1 + k, tile), :] for k in range(D_CONV))

    z = _dot(xc.astype(BF16), wbd_ref[...]) + bias_ref[...]
    r = jax.nn.sigmoid(z[:, :w])
    gi = jax.nn.sigmoid(z[:, w:])
    lam = lam_ref[...]
    log_a = (-D_C * jnp.log(1.0 + jnp.exp(-lam))) * r
    a = jnp.exp(log_a)
    a_s[...] = a
    u_s[...] = jnp.sqrt(1.0 - a * a) * gi * xc

    n_groups = tile // SUBLANES

    def step(g, c):
        gg = (n_groups - 1 - g) if reverse else g
        r0 = pl.multiple_of(gg * SUBLANES, SUBLANES)
        ac, uc = _scan8(a_s[pl.ds(r0, SUBLANES), :], u_s[pl.ds(r0, SUBLANES), :], reverse)
        h = ac * c + uc
        o_ref[0, pl.ds(r0, SUBLANES), :] = h
        last = h[0:1, :] if reverse else h[SUBLANES - 1:SUBLANES, :]
        return jnp.broadcast_to(last, (SUBLANES, w))

    c_fin = lax.fori_loop(0, n_groups, step, carry[...], unroll=4 if n_groups % 4 == 0 else 1)
    carry[...] = c_fin

    if combine:
        o_ref[0] = (hf_ref[0] + o_ref[0]) * _gelu_tanh(gd_ref[0])

    @pl.when(i == n_tiles - 1)
    def _():
        hl_ref[0] = c_fin[0:1, :]


def _rglru_sweep(xd, conv_w, conv_b, wbd, bias, lam, h0, hf=None, gd=None, *, reverse):
    b_, n, w = xd.shape
    combine = hf is not None
    tile = min(n, 512)
    n_tiles = n // tile
    blocks_per_tile = tile // HALO
    n_blocks = n // HALO

    def tmap(b, i):
        return (b, (n_tiles - 1 - i) if reverse else i, 0)

    def pmap(b, i):
        t = (n_tiles - 1 - i) if reverse else i
        return (b, jnp.maximum(t * blocks_per_tile - 1, 0), 0)

    def nmap(b, i):
        t = (n_tiles - 1 - i) if reverse else i
        return (b, jnp.minimum((t + 1) * blocks_per_tile, n_blocks - 1), 0)

    const2 = lambda b, i: (0, 0)
    in_specs = [
        pl.BlockSpec((1, tile, w), tmap),
        pl.BlockSpec((1, HALO, w), pmap),
        pl.BlockSpec((1, HALO, w), nmap),
        pl.BlockSpec((D_CONV, w), const2),
        pl.BlockSpec((1, w), const2),
        pl.BlockSpec((w, 2 * w), const2),
        pl.BlockSpec((1, 2 * w), const2),
        pl.BlockSpec((1, w), const2),
        pl.BlockSpec((1, 1, w), lambda b, i: (b, 0, 0)),
    ]
    args = [xd, xd, xd, conv_w, conv_b, wbd, bias, lam, h0]
    if combine:
        in_specs += [pl.BlockSpec((1, tile, w), tmap), pl.BlockSpec((1, tile, w), tmap)]
        args += [hf, gd]
    return pl.pallas_call(
        functools.partial(_rglru_kernel, tile=tile, n_tiles=n_tiles, reverse=reverse, combine=combine),
        grid=(b_, n_tiles),
        in_specs=in_specs,
        out_specs=[pl.BlockSpec((1, tile, w), tmap), pl.BlockSpec((1, 1, w), lambda b, i: (b, 0, 0))],
        out_shape=[jax.ShapeDtypeStruct((b_, n, w), F32), jax.ShapeDtypeStruct((b_, 1, w), F32)],
        scratch_shapes=[pltpu.VMEM((tile + 2 * HALO, w), F32), pltpu.VMEM((tile, w), F32),
                        pltpu.VMEM((tile, w), F32), pltpu.VMEM((SUBLANES, w), F32)],
        compiler_params=_params(2),
        name="rglru_bwd_combine" if combine else "rglru_fwd",
    )(*args)


def _block_diag(wz):
    eye = jnp.eye(D_BLOCKS, dtype=wz.dtype)
    return jnp.einsum('hij,hg->higj', wz, eye).reshape(D_WIDTH, D_WIDTH)


def rglru_pallas(xd, gd, conv_w, conv_b, w_r, b_r, w_i, b_i, lam, s0):
    cb = conv_b.reshape(1, D_WIDTH)
    outs = []
    for z in range(2):
        wbd = jnp.concatenate([_block_diag(w_r[z]), _block_diag(w_i[z])], axis=1).astype(BF16)
        bias = jnp.concatenate([b_r[z], b_i[z]]).reshape(1, 2 * D_WIDTH)
        outs.append((wbd, bias, lam[z].reshape(1, D_WIDTH)))
    hf, sf = _rglru_sweep(xd, conv_w, cb, *outs[0], s0[0][:, None, :], reverse=False)
    y, sb = _rglru_sweep(xd, conv_w, cb, *outs[1], s0[1][:, None, :], hf, gd, reverse=True)
    return y, (sf[:, 0], sb[:, 0])


def _expert_ffn_kernel(x_ref, g_ref, w1_ref, w3_ref, w2_ref, o_ref):
    x = x_ref[0, 0].astype(BF16)
    a = _dot(x, w1_ref[0])
    b = _dot(x, w3_ref[0])
    hid = (_silu(a) * b).astype(BF16)
    o_ref[0, 0] = _dot(hid, w2_ref[0]) * g_ref[0, 0]


def expert_ffn(xs, gate, w1, w3, w2):
    b_, e_, c_, d = xs.shape
    f = w1.shape[-1]
    tm = min(c_, 512)
    grid = (e_, b_, c_ // tm)
    return pl.pallas_call(
        _expert_ffn_kernel,
        grid=grid,
        in_specs=[
            pl.BlockSpec((1, 1, tm, d), lambda e, b, m: (b, e, m, 0)),
            pl.BlockSpec((1, 1, tm, 1), lambda e, b, m: (b, e, m, 0)),
            pl.BlockSpec((1, d, f), lambda e, b, m: (e, 0, 0)),
            pl.BlockSpec((1, d, f), lambda e, b, m: (e, 0, 0)),
            pl.BlockSpec((1, f, d), lambda e, b, m: (e, 0, 0)),
        ],
        out_specs=pl.BlockSpec((1, 1, tm, d), lambda e, b, m: (b, e, m, 0)),
        out_shape=jax.ShapeDtypeStruct((b_, e_, c_, d), F32),
        compiler_params=_params(3),
        name="expert_ffn",
    )(xs, gate, w1, w3, w2)


def moe_from_affinity(h, aff, w1, w3, w2):
    b_, n, d = h.shape
    cap = max(1, EC_FACTOR * n // N_EXPERTS)
    gate, idx = lax.top_k(aff, cap)
    xs = jax.vmap(lambda hb, ib: hb[ib])(h, idx)
    out = expert_ffn(xs, gate[..., None], w1, w3, w2)
    return jax.vmap(lambda ob, ib: jnp.zeros((n, d), ob.dtype).at[ib.reshape(-1)].add(ob.reshape(-1, d)))(out, idx)


def kernel(x, c, ctx, c_ctx, w_mod, b_mod, norm1_w, norm2_w, final_norm_w,
           ev_w_in, ev_w_out, a_conv_w, a_log, a_dt_bias, a_norm_w, b_lb_logits, b_norm_w,
           od_w_in, od_w_out, c_sink, d_conv_w, d_conv_b, d_w_r, d_b_r, d_w_i, d_b_i, d_lambda,
           moe_router, moe_w1, moe_w3, moe_w2):
    b_, n, d = x.shape
    cos, sin = _rope_tables(n)
    lb_all = jnp.cumsum(jax.nn.softmax(b_lb_logits.astype(F32), axis=0), axis=0)
    lb_all = lb_all - lb_all[0:1]
    c_rows = jnp.concatenate([c, c_ctx[None, :], jnp.zeros((-(b_ + 1) % SUBLANES, d), F32)], axis=0)
    mod_all = modulation(c_rows, w_mod, b_mod)
    out = None
    for l in range(DEPTH):
        last = l == DEPTH - 1
        j = l // 2
        w1b, w3b, w2b = moe_w1[l].astype(BF16), moe_w3[l].astype(BF16), moe_w2[l].astype(BF16)
        router_t = moe_router[l].T.astype(BF16)
        mod = mod_all[l, :b_]
        mod_c = jnp.broadcast_to(mod_all[l, b_], (b_, 6 * d))
        sh1, sc1, g1, sh2, sc2, g2 = jnp.split(mod, 6, axis=-1)
        csh1, csc1, cg1, csh2, csc2, cg2 = jnp.split(mod_c, 6, axis=-1)
        if l % 2 == 0:
            w_in, w_out = _even_w_in(ev_w_in[j]), ev_w_out[j].astype(BF16)
            zero = jnp.zeros((b_, A_HEADS, A_DK, A_DV), F32)
            pars = (a_conv_w[j], a_log[j], a_dt_bias[j], a_norm_w[j], lb_all[j].reshape(2, HB), b_norm_w[j])
            o_c, st = even_mixer_pallas(in_proj(ctx, norm1_w[l], csh1, csc1, w_in, EVEN_GROUPS), *pars,
                                        ((zero, zero), (zero, zero)))
            o_l, _ = even_mixer_pallas(in_proj(x, norm1_w[l], sh1, sc1, w_in, EVEN_GROUPS), *pars, st)
            parts_l, parts_c = [o_l], [o_c]
        else:
            w_in, w_out = od_w_in[j].astype(BF16), od_w_out[j].astype(BF16)
            qc, kc, vc, xdc, gdc = in_proj(ctx, norm1_w[l], csh1, csc1, w_in, ODD_SIZES)
            ql, kl, vl, xdl, gdl = in_proj(x, norm1_w[l], sh1, sc1, w_in, ODD_SIZES)
            att_l = window_attention_pallas(ql, kl, vl, kc, vc, c_sink[j], cos, sin)
            rg_pars = (d_conv_w[j], d_conv_b[j], d_w_r[j], d_b_r[j], d_w_i[j], d_b_i[j], d_lambda[j])
            zero = jnp.zeros((b_, D_WIDTH), F32)
            rg_c, st = rglru_pallas(xdc, gdc, *rg_pars, (zero, zero))
            rg_l, _ = rglru_pallas(xdl, gdl, *rg_pars, st)
            parts_l = [att_l, rg_l]
            if not last:
                parts_c = [context_attention_pallas(qc, kc, vc, c_sink[j]), rg_c]
        x1, h2, aff = post_mixer(parts_l, w_out, x, g1, norm2_w[l], sh2, sc2, router_t)
        moe_l = moe_from_affinity(h2, aff, w1b, w3b, w2b)
        if last:
            out = final_norm(x1, moe_l, g2, final_norm_w)
        else:
            x = x1 + g2[:, None, :] * moe_l
            c1, hc2, affc = post_mixer(parts_c, w_out, ctx, cg1, norm2_w[l], csh2, csc2, router_t)
            ctx = c1 + cg2[:, None, :] * moe_from_affinity(hc2, affc, w1b, w3b, w2b)
    return out
```

```python
import functools

import jax
import jax.numpy as jnp
from jax import lax
from jax.experimental import pallas as pl
from jax.experimental.pallas import tpu as pltpu

D_MODEL = 1024
DEPTH = 4
GRID_W = 64
EPS = 1e-6
F32 = jnp.float32
BF16 = jnp.bfloat16
HI = lax.Precision.HIGHEST

A_HEADS = 4
A_DK = 128
A_DV = 128
A_CONV = 4
A_CHUNK = 64
A_QKV = 2 * A_HEADS * A_DK + A_HEADS * A_DV
B_HEADS = 4
B_DK = 128
B_DV = 128
B_CHUNK = 64
C_HEADS = 8
C_KV_HEADS = 2
C_HD = 64
C_WIN = 128
C_BLOCK = 128
ROPE_THETA = 10000.0
D_WIDTH = 512
D_BLOCKS = 8
D_BW = D_WIDTH // D_BLOCKS
D_CONV = 4
D_C = 8.0
N_EXPERTS = 16
EXPERT_FF = 1024
EC_FACTOR = 2

EVEN_SIZES = (A_QKV, A_HEADS * A_DV, 2 * A_HEADS, 2 * A_HEADS,
              B_HEADS * B_DK, B_HEADS * B_DV, 2 * B_HEADS * B_DK, B_HEADS * B_DV)
ODD_SIZES = (C_HEADS * C_HD, C_KV_HEADS * C_HD, C_KV_HEADS * C_HD, D_WIDTH, D_WIDTH)

LANES = 128
SUBLANES = 8
VMEM_LIMIT_BYTES = 56 * 1024 * 1024

ROW_TILE = 256
HALO = SUBLANES


def _cuts(sizes):
    out, acc = [], 0
    for s in sizes[:-1]:
        acc += s
        out.append(acc)
    return out


def _params(n_axes):
    return pltpu.CompilerParams(dimension_semantics=("arbitrary",) * n_axes, vmem_limit_bytes=VMEM_LIMIT_BYTES)


def _dot(a, b, precision=None):
    return jnp.dot(a, b, preferred_element_type=F32, precision=precision)


def _dot_nt(a, b, precision=None):
    return lax.dot_general(a, b, (((1,), (1,)), ((), ())), preferred_element_type=F32, precision=precision)


def _dot_tn(a, b):
    return lax.dot_general(a, b, (((0,), (0,)), ((), ())), preferred_element_type=F32)


def _silu(x):
    return x * jax.nn.sigmoid(x)


def _rms(x):
    return x * lax.rsqrt(jnp.mean(x * x, axis=-1, keepdims=True) + EPS)


MOD_COLS = 1536


def _mod_kernel(c_ref, w_ref, b_ref, o_ref):
    o_ref[0] = _dot(_silu(c_ref[...]), w_ref[0]) + b_ref[0]


def modulation(c_rows, w_mod, b_mod):
    r, d = c_rows.shape
    depth, _, wide = w_mod.shape
    return pl.pallas_call(
        _mod_kernel,
        grid=(depth, wide // MOD_COLS),
        in_specs=[pl.BlockSpec((r, d), lambda l, j: (0, 0)),
                  pl.BlockSpec((1, d, MOD_COLS), lambda l, j: (l, 0, j)),
                  pl.BlockSpec((1, 1, MOD_COLS), lambda l, j: (l, 0, j))],
        out_specs=pl.BlockSpec((1, r, MOD_COLS), lambda l, j: (l, 0, j)),
        out_shape=jax.ShapeDtypeStruct((depth, r, wide), F32),
        compiler_params=_params(2),
        name="modulation",
    )(c_rows, w_mod, b_mod[:, None, :])


def _norm_mod(x, nw, shift, scale):
    return _rms(x) * nw * (1.0 + scale) + shift


def _in_proj_kernel(x_ref, nw_ref, sh_ref, sc_ref, w_ref, *o_refs, splits):
    h = _norm_mod(x_ref[0], nw_ref[...], sh_ref[0], sc_ref[0]).astype(BF16)
    off = 0
    for o_ref, s in zip(o_refs, splits):
        o_ref[0] = _dot(h, w_ref[:, off:off + s])
        off += s


def in_proj(x, norm_w, shift, scale, w, splits):
    b_, n, d = x.shape
    tm = min(n, ROW_TILE)
    ntot = sum(splits)
    vec = pl.BlockSpec((1, 1, d), lambda b, i: (b, 0, 0))
    return pl.pallas_call(
        functools.partial(_in_proj_kernel, splits=tuple(splits)),
        grid=(b_, n // tm),
        in_specs=[pl.BlockSpec((1, tm, d), lambda b, i: (b, i, 0)),
                  pl.BlockSpec((1, d), lambda b, i: (0, 0)), vec, vec,
                  pl.BlockSpec((d, ntot), lambda b, i: (0, 0))],
        out_specs=[pl.BlockSpec((1, tm, s), lambda b, i: (b, i, 0)) for s in splits],
        out_shape=[jax.ShapeDtypeStruct((b_, n, s), F32) for s in splits],
        compiler_params=_params(2),
        name="in_proj",
    )(x, norm_w.reshape(1, d), shift[:, None, :], scale[:, None, :], w)


def _post_mixer_kernel(*refs, n_parts):
    parts = refs[:n_parts]
    w_ref, x_ref, g1_ref, nw_ref, sh_ref, sc_ref, rt_ref, x1_ref, h2_ref, aff_ref = refs[n_parts:]
    y = None
    off = 0
    for p_ref in parts:
        k = p_ref.shape[-1]
        t = _dot(p_ref[0].astype(BF16), w_ref[off:off + k, :])
        y = t if y is None else y + t
        off += k
    x1 = x_ref[0] + g1_ref[0] * y
    x1_ref[0] = x1
    h2 = _norm_mod(x1, nw_ref[...], sh_ref[0], sc_ref[0])
    h2_ref[0] = h2
    logits = _dot_nt(rt_ref[...], h2.astype(BF16))
    e = jnp.exp(logits - jnp.max(logits, axis=0, keepdims=True))
    aff_ref[0] = e / jnp.sum(e, axis=0, keepdims=True)


def post_mixer(parts, w_out, x, g1, norm_w, shift, scale, router_t):
    b_, n, d = x.shape
    e_ = router_t.shape[0]
    tm = min(n, ROW_TILE)
    vec = pl.BlockSpec((1, 1, d), lambda b, i: (b, 0, 0))
    tok = lambda k: pl.BlockSpec((1, tm, k), lambda b, i: (b, i, 0))
    return pl.pallas_call(
        functools.partial(_post_mixer_kernel, n_parts=len(parts)),
        grid=(b_, n // tm),
        in_specs=[tok(p.shape[-1]) for p in parts] + [
            pl.BlockSpec(w_out.shape, lambda b, i: (0, 0)), tok(d), vec,
            pl.BlockSpec((1, d), lambda b, i: (0, 0)), vec, vec,
            pl.BlockSpec((e_, d), lambda b, i: (0, 0))],
        out_specs=[tok(d), tok(d), pl.BlockSpec((1, e_, tm), lambda b, i: (b, 0, i))],
        out_shape=[jax.ShapeDtypeStruct((b_, n, d), F32), jax.ShapeDtypeStruct((b_, n, d), F32),
                   jax.ShapeDtypeStruct((b_, e_, n), F32)],
        compiler_params=_params(2),
        name="post_mixer",
    )(*parts, w_out, x, g1[:, None, :], norm_w.reshape(1, d), shift[:, None, :], scale[:, None, :], router_t)


def _final_kernel(x_ref, nw_ref, o_ref):
    o_ref[0] = _rms(x_ref[0]) * nw_ref[...]


def final_norm(x, norm_w):
    b_, n, d = x.shape
    tm = min(n, 2 * ROW_TILE)
    tok = pl.BlockSpec((1, tm, d), lambda b, i: (b, i, 0))
    return pl.pallas_call(
        _final_kernel,
        grid=(b_, n // tm),
        in_specs=[tok, pl.BlockSpec((1, d), lambda b, i: (0, 0))],
        out_specs=tok,
        out_shape=jax.ShapeDtypeStruct((b_, n, d), F32),
        compiler_params=_params(2),
        name="final_norm",
    )(x, norm_w.reshape(1, d))


CHUNK = A_CHUNK
SUB = 16
EVEN_TILE = 256
HA = A_HEADS * A_DK
HB = B_HEADS * B_DK
AB_PAD = LANES
EVEN_GROUPS = (A_QKV, A_HEADS * A_DV, AB_PAD, HB, B_HEADS * B_DV, 2 * HB, B_HEADS * B_DV)
NEUMANN_STEPS = 5


def _split2(x):
    hi = x.astype(BF16)
    return hi, (x - hi.astype(F32)).astype(BF16)


def _dot_split(a, b):
    (ah, al), (bh, bl) = a, b
    return _dot(ah, bh) + _dot(ah, bl) + _dot(al, bh)


def _dot_exact_lhs(a_bf16, x):
    x1 = x.astype(BF16)
    r1 = x - x1.astype(F32)
    x2 = r1.astype(BF16)
    x3 = (r1 - x2.astype(F32)).astype(BF16)
    return _dot(a_bf16, x1) + _dot(a_bf16, x2) + _dot(a_bf16, x3)


def _even_kernel(*refs, tile, n_tiles, reverse, combine):
    if combine:
        (qkv_ref, qkvp_ref, qkvn_ref, ab_ref, qb_ref, ib_ref, fb_ref, cw_ref, exp_ref, alog_ref, dtb_ref, lb_ref,
         sa0_ref, sb0_ref, of_ref, ga_ref, gb_ref, anw_ref, bnw_ref,
         o_ref, sa_ref, sb_ref, xbuf, g_scr, st_a, st_b) = refs
    else:
        (qkv_ref, qkvp_ref, qkvn_ref, ab_ref, qb_ref, ib_ref, fb_ref, cw_ref, exp_ref, alog_ref, dtb_ref, lb_ref,
         sa0_ref, sb0_ref,
         o_ref, sa_ref, sb_ref, xbuf, g_scr, st_a, st_b) = refs
    i = pl.program_id(1)
    t = (n_tiles - 1 - i) if reverse else i

    @pl.when(i == 0)
    def _():
        st_a[...] = sa0_ref[0]
        st_b[...] = sb0_ref[0]

    L = CHUNK
    n_chunks = tile // L
    n_blocks = tile // SUB
    chunk_order = range(n_chunks - 1, -1, -1) if reverse else range(n_chunks)
    block_order = range(n_blocks - 1, -1, -1) if reverse else range(n_blocks)

    ti = lax.broadcasted_iota(jnp.int32, (tile, tile), 0)
    tj = lax.broadcasted_iota(jnp.int32, (tile, tile), 1)
    t_incl = (tj >= ti) if reverse else (tj <= ti)
    tri_chunk = (t_incl & ((ti // L) == (tj // L))).astype(BF16)
    tri_blk = (t_incl & ((ti // SUB) == (tj // SUB))).astype(BF16)
    ones_blk = ((ti // SUB) == (tj // SUB)).astype(BF16)
    ii = lax.broadcasted_iota(jnp.int32, (L, L), 0)
    jj = lax.broadcasted_iota(jnp.int32, (L, L), 1)
    incl = (jj >= ii) if reverse else (jj <= ii)
    strict = (jj > ii) if reverse else (jj < ii)
    eye = (ii == jj).astype(F32)
    last_row = 0 if reverse else L - 1

    xbuf[pl.ds(0, HALO), :] = jnp.where(t > 0, qkvp_ref[0], 0.0)
    xbuf[pl.ds(HALO, tile), :] = qkv_ref[0]
    xbuf[pl.ds(HALO + tile, HALO), :] = jnp.where(t < n_tiles - 1, qkvn_ref[0], 0.0)
    conv = _silu(sum(cw_ref[pl.ds(k, 1), :] * xbuf[pl.ds(HALO - 1 + k, tile), :] for k in range(A_CONV)))
    ab_x = _dot(ab_ref[0], exp_ref[...], HI)
    g_all = -jnp.exp(alog_ref[...]) * jax.nn.softplus(ab_x[:, :HA] + dtb_ref[...])
    beta_all = jax.nn.sigmoid(ab_x[:, HA:])
    cum_all = _dot_exact_lhs(tri_chunk, g_all)
    ecum_all = jnp.exp(cum_all)

    chains = [(c, h) for c in chunk_order for h in range(A_HEADS)]
    pre = {}
    neg_m = []
    for c, h in chains:
        rs = slice(c * L, (c + 1) * L)
        hs = slice(h * A_DK, (h + 1) * A_DK)
        qh = conv[rs, hs]
        kh = conv[rs, HA + h * A_DK:HA + (h + 1) * A_DK]
        q = qh * (lax.rsqrt(jnp.sum(qh * qh, axis=-1, keepdims=True) + EPS) * (A_DK ** -0.5))
        k = kh * lax.rsqrt(jnp.sum(kh * kh, axis=-1, keepdims=True) + EPS)
        v = conv[rs, 2 * HA + h * A_DV:2 * HA + (h + 1) * A_DV]
        beta, cum, ecum = beta_all[rs, hs], cum_all[rs, hs], ecum_all[rs, hs]
        r_ll = cum.T[:L, :]
        decay = jnp.where(incl, jnp.exp(jnp.where(incl, cum[:, :L] - r_ll, 0.0)), 0.0)
        kb = k * beta
        k16 = k.astype(BF16)
        neg_m.append(jnp.where(strict, -_dot_nt(kb.astype(BF16), k16) * decay, 0.0))
        qk = _dot_nt(q.astype(BF16), k16) * decay
        total = cum[last_row:last_row + 1, :]
        pre[(c, h)] = dict(rhs=jnp.concatenate([v * beta, kb * ecum], axis=1), qk=qk.astype(BF16),
                           qd=(q * ecum).astype(BF16), kd=(k * jnp.exp(total - cum)).astype(BF16),
                           gl=jnp.exp(total))
    p = [_split2(m) for m in neg_m]
    tinv = [eye + m for m in neg_m]
    for _ in range(NEUMANN_STEPS):
        p = [_split2(_dot_split(x, x)) for x in p]
        tinv = [tv + _dot_split(_split2(tv), x) for tv, x in zip(tinv, p)]
    sol_of = {ch: _dot_split(_split2(tv), _split2(pre[ch]["rhs"])) for tv, ch in zip(tinv, chains)}

    for c in chunk_order:
        rs = slice(c * L, (c + 1) * L)
        s_old = [st_a[h] for h in range(A_HEADS)]
        s16 = [s.astype(BF16) for s in s_old]
        ws = [_dot(sol_of[(c, h)][:, A_DV:].astype(BF16), s16[h]) for h in range(A_HEADS)]
        qs_ = [_dot(pre[(c, h)]["qd"], s16[h]) for h in range(A_HEADS)]
        v_new = [(sol_of[(c, h)][:, :A_DV] - ws[h]).astype(BF16) for h in range(A_HEADS)]
        for h in range(A_HEADS):
            o_ref[0, rs, h * A_DV:(h + 1) * A_DV] = qs_[h] + _dot(pre[(c, h)]["qk"], v_new[h])
            st_a[h] = s_old[h] * pre[(c, h)]["gl"] + _dot_tn(pre[(c, h)]["kd"], v_new[h])

    hq = _silu(qb_ref[0])
    hv = ib_ref[0]
    lb = lb_ref[...]
    fg = lb + (1.0 - lb) * jax.nn.sigmoid(fb_ref[0])
    lf = jnp.log(fg)
    hk = 1.0 - fg
    b_all = _dot_exact_lhs(tri_blk, lf)
    e_all = _dot_exact_lhs(ones_blk, lf)
    qs_all = (hq * jnp.exp(b_all)).astype(BF16)
    ks_all = (hk * jnp.exp(e_all - b_all)).astype(BF16)
    hv16 = hv.astype(BF16)
    row_in_blk = lax.broadcasted_iota(jnp.int32, (tile, HB), 0) % SUB
    od = [jnp.zeros((tile, B_DV), F32) for _ in range(B_HEADS)]
    for s_ in range(SUB):
        valid = (row_in_blk + s_ < SUB) if reverse else (row_in_blk >= s_)
        shift = ((tile - s_) if reverse else s_) % tile
        roll = (lambda x: x) if s_ == 0 else (lambda x: pltpu.roll(x, shift, axis=0))
        prod = jnp.where(valid, hq * roll(hk) * jnp.exp(jnp.where(valid, b_all - roll(b_all), 0.0)), 0.0)
        vv = roll(hv)
        for h in range(B_HEADS):
            hs = slice(h * B_DK, (h + 1) * B_DK)
            od[h] = od[h] + jnp.sum(prod[:, hs], axis=-1, keepdims=True) * vv[:, hs]
    for blk in block_order:
        bs = slice(blk * SUB, (blk + 1) * SUB)
        for h in range(B_HEADS):
            hs = slice(h * B_DK, (h + 1) * B_DK)
            g_scr[blk * B_HEADS + h] = _dot_tn(hv16[bs, hs], ks_all[bs, hs])

    st = [st_b[h] for h in range(B_HEADS)]
    for blk in block_order:
        bs = slice(blk * SUB, (blk + 1) * SUB)
        for h in range(B_HEADS):
            hs = slice(h * B_DK, (h + 1) * B_DK)
            o_ref[0, bs, HA + h * B_DV:HA + (h + 1) * B_DV] = _dot_nt(qs_all[bs, hs], st[h].astype(BF16)) + od[h][bs]
            st[h] = st[h] * jnp.exp(e_all[blk * SUB:blk * SUB + 1, hs]) + g_scr[blk * B_HEADS + h]
    for h in range(B_HEADS):
        st_b[h] = st[h]

    if combine:
        tot = of_ref[0] + o_ref[0]
        for h in range(A_HEADS + B_HEADS):
            hs = slice(h * A_DV, (h + 1) * A_DV)
            nw = anw_ref[...] if h < A_HEADS else bnw_ref[...]
            gate = ga_ref[0, :, hs] if h < A_HEADS else gb_ref[0, :, pl.ds((h - A_HEADS) * B_DV, B_DV)]
            o_ref[0, :, hs] = _rms(tot[:, hs]) * nw * _silu(gate)

    @pl.when(i == n_tiles - 1)
    def _():
        sa_ref[0] = st_a[...]
        sb_ref[0] = st_b[...]


def _even_sweep(outs, z, conv_w, a_log, dt_bias, lb, s_a, s_b, of=None, a_norm_w=None, b_norm_w=None):
    qkv, ga, ab, qb, ib, fb, gb = outs
    b_, n, _ = qkv.shape
    reverse = z == 1
    combine = of is not None
    tile = min(n, EVEN_TILE)
    n_tiles = n // tile
    bpt = tile // HALO
    nblk = n // HALO
    tidx = lambda i: (n_tiles - 1 - i) if reverse else i
    tmap = lambda b, i: (b, tidx(i), 0)
    pmap = lambda b, i: (b, jnp.maximum(tidx(i) * bpt - 1, 0), 0)
    nmap = lambda b, i: (b, jnp.minimum((tidx(i) + 1) * bpt, nblk - 1), 0)
    zmap = lambda b, i: (b, tidx(i), z)
    const2 = lambda b, i: (0, 0)
    smap = lambda b, i: (b, 0, 0, 0)
    src = lax.broadcasted_iota(jnp.int32, (AB_PAD, 2 * HA), 0)
    dst = lax.broadcasted_iota(jnp.int32, (AB_PAD, 2 * HA), 1)
    expand = (src == jnp.where(dst >= HA, 2 * A_HEADS, 0) + z * A_HEADS + (dst % HA) // A_DK).astype(F32)
    rep = lambda p_: jnp.broadcast_to(p_[:, None], (A_HEADS, A_DK)).reshape(1, HA)
    in_specs = [
        pl.BlockSpec((1, tile, A_QKV), tmap), pl.BlockSpec((1, HALO, A_QKV), pmap), pl.BlockSpec((1, HALO, A_QKV), nmap),
        pl.BlockSpec((1, tile, AB_PAD), tmap), pl.BlockSpec((1, tile, HB), tmap), pl.BlockSpec((1, tile, HB), tmap),
        pl.BlockSpec((1, tile, HB), zmap),
        pl.BlockSpec((A_CONV, A_QKV), const2), pl.BlockSpec((AB_PAD, 2 * HA), const2),
        pl.BlockSpec((1, HA), const2), pl.BlockSpec((1, HA), const2), pl.BlockSpec((1, HB), const2),
        pl.BlockSpec((1, A_HEADS, A_DK, A_DV), smap), pl.BlockSpec((1, B_HEADS, B_DV, B_DK), smap),
    ]
    args = [qkv, qkv, qkv, ab, qb, ib, fb, conv_w, expand, rep(a_log[z]), rep(dt_bias[z]), lb[z].reshape(1, HB), s_a, s_b]
    if combine:
        in_specs += [pl.BlockSpec((1, tile, HA + HB), tmap), pl.BlockSpec((1, tile, HA), tmap),
                     pl.BlockSpec((1, tile, HB), tmap), pl.BlockSpec((1, A_DV), const2), pl.BlockSpec((1, B_DV), const2)]
        args += [of, ga, gb, a_norm_w.reshape(1, A_DV), b_norm_w.reshape(1, B_DV)]
    return pl.pallas_call(
        functools.partial(_even_kernel, tile=tile, n_tiles=n_tiles, reverse=reverse, combine=combine),
        grid=(b_, n_tiles),
        in_specs=in_specs,
        out_specs=[pl.BlockSpec((1, tile, HA + HB), tmap),
                   pl.BlockSpec((1, A_HEADS, A_DK, A_DV), smap), pl.BlockSpec((1, B_HEADS, B_DV, B_DK), smap)],
        out_shape=[jax.ShapeDtypeStruct((b_, n, HA + HB), F32), jax.ShapeDtypeStruct(s_a.shape, F32),
                   jax.ShapeDtypeStruct(s_b.shape, F32)],
        scratch_shapes=[pltpu.VMEM((tile + 2 * HALO, A_QKV), F32),
                        pltpu.VMEM((tile // SUB * B_HEADS, B_DV, B_DK), F32),
                        pltpu.VMEM((A_HEADS, A_DK, A_DV), F32), pltpu.VMEM((B_HEADS, B_DV, B_DK), F32)],
        compiler_params=_params(2),
        name="even_bwd_combine" if combine else "even_fwd",
    )(*args)


def even_mixer_pallas(outs, conv_w, a_log, dt_bias, a_norm_w, lb, b_norm_w, states):
    (saf, sbf), (sab, sbb) = states
    o_f, saf, sbf = _even_sweep(outs, 0, conv_w, a_log, dt_bias, lb, saf, sbf)
    o, sab, sbb = _even_sweep(outs, 1, conv_w, a_log, dt_bias, lb, sab, sbb, o_f, a_norm_w, b_norm_w)
    return o, ((saf, sbf), (sab, sbb))


def _even_w_in(w):
    c = _cuts(EVEN_SIZES)
    ab = jnp.pad(w[:, c[1]:c[3]], ((0, 0), (0, AB_PAD - (c[3] - c[1]))))
    return jnp.concatenate([w[:, :c[1]], ab, w[:, c[3]:]], axis=1).astype(BF16)


ROPE_NF = C_HD // 4
NEG_BIG = -1e30


def _rope(t, cos, sin_signed):
    w = t.shape[-1]
    lane = lax.broadcasted_iota(jnp.int32, t.shape, 1)
    partner = jnp.where(lane % (2 * ROPE_NF) < ROPE_NF,
                        pltpu.roll(t, w - ROPE_NF, axis=1), pltpu.roll(t, ROPE_NF, axis=1))
    return t * cos + partner * sin_signed


def _dup_kv_head(t, kh):
    lane = lax.broadcasted_iota(jnp.int32, t.shape, 1)
    rolled = pltpu.roll(t, C_HD, axis=1)
    own_half = (lane < C_HD) if kh == 0 else (lane >= C_HD)
    return jnp.where(own_half, t, rolled)


def _attn_kernel(*refs, banded, n_blocks):
    if banded:
        (q_ref, kp_ref, k_ref, kn_ref, vp_ref, v_ref, vn_ref, cq_ref, sq_ref, cp_ref, sp_ref, cn_ref, sn_ref,
         kc_ref, vc_ref, sink_ref, o_ref) = refs
    else:
        q_ref, kc_ref, vc_ref, sink_ref, o_ref = refs
    nb = pl.program_id(1)
    blk = q_ref.shape[1]
    q = q_ref[0] * (C_HD ** -0.5)
    kc = kc_ref[0]
    vc = vc_ref[0]
    if banded:
        cq, sq = cq_ref[...], sq_ref[...]
        q = _rope(q, jnp.concatenate([cq] * (C_HEADS // C_KV_HEADS), axis=1),
                  jnp.concatenate([sq] * (C_HEADS // C_KV_HEADS), axis=1))
        kb = [_rope(kp_ref[0], cp_ref[...], sp_ref[...]), _rope(k_ref[0], cq, sq), _rope(kn_ref[0], cn_ref[...], sn_ref[...])]
        keys = jnp.concatenate(kb + [kc], axis=0)
        vals = jnp.concatenate([vp_ref[0], v_ref[0], vn_ref[0], vc], axis=0)
        qpos = lax.broadcasted_iota(jnp.int32, (blk, keys.shape[0]), 0)
        kcol = lax.broadcasted_iota(jnp.int32, (blk, keys.shape[0]), 1)
        rel = kcol - blk
        in_band = (jnp.abs(rel - qpos) <= C_WIN) & (rel + nb * blk >= 0) & (rel + nb * blk < n_blocks * blk)
        mask = in_band | (kcol >= 3 * blk)
    else:
        keys, vals, mask = kc, vc, None
    keys = keys.astype(BF16)
    vals = vals.astype(BF16)
    lane_q = lax.broadcasted_iota(jnp.int32, (blk, 2 * C_HD), 1)
    grp = C_HEADS // C_KV_HEADS
    for pair in range(C_HEADS // 2):
        kh = (2 * pair) // grp
        k_dup = _dup_kv_head(keys, kh)
        v_dup = _dup_kv_head(vals, kh)
        q_pair = q[:, pair * 2 * C_HD:(pair + 1) * 2 * C_HD]
        outs = []
        for sub in range(2):
            own = (lane_q < C_HD) if sub == 0 else (lane_q >= C_HD)
            qh = jnp.where(own, q_pair, 0.0).astype(BF16)
            s = _dot_nt(qh, k_dup)
            if mask is not None:
                s = jnp.where(mask, s, NEG_BIG)
            sink = sink_ref[2 * pair + sub]
            m = jnp.maximum(jnp.max(s, axis=-1, keepdims=True), sink)
            e = jnp.exp(s - m)
            denom = jnp.sum(e, axis=-1, keepdims=True) + jnp.exp(sink - m)
            p = (e / denom).astype(BF16)
            outs.append(_dot(p, v_dup))
        o_ref[0, :, pair * 2 * C_HD:(pair + 1) * 2 * C_HD] = jnp.where(lane_q < C_HD, outs[0], outs[1])


def _rope_tables(n):
    tok = lax.iota(jnp.int32, n)
    row = (tok // GRID_W).astype(F32)
    col = (tok % GRID_W).astype(F32)
    inv = ROPE_THETA ** (-lax.iota(F32, ROPE_NF) / ROPE_NF)
    ang_r, ang_c = row[:, None] * inv, col[:, None] * inv
    cos = jnp.concatenate([jnp.cos(ang_r)] * 2 + [jnp.cos(ang_c)] * 2, axis=-1)
    sin = jnp.concatenate([-jnp.sin(ang_r), jnp.sin(ang_r), -jnp.sin(ang_c), jnp.sin(ang_c)], axis=-1)
    return jnp.concatenate([cos, cos], axis=-1), jnp.concatenate([sin, sin], axis=-1)


def window_attention_pallas(q, k, v, kc, vc, sink, cos, sin):
    b_, n, hq = q.shape
    hk = k.shape[-1]
    lc = kc.shape[1]
    blk = C_BLOCK
    nb_ = n // blk
    cur = lambda b, i: (b, i, 0)
    prv = lambda b, i: (b, jnp.maximum(i - 1, 0), 0)
    nxt = lambda b, i: (b, jnp.minimum(i + 1, nb_ - 1), 0)
    tcur = lambda b, i: (i, 0)
    tprv = lambda b, i: (jnp.maximum(i - 1, 0), 0)
    tnxt = lambda b, i: (jnp.minimum(i + 1, nb_ - 1), 0)
    kspec = lambda m: pl.BlockSpec((1, blk, hk), m)
    tspec = lambda m: pl.BlockSpec((blk, hk), m)
    cspec = pl.BlockSpec((1, lc, hk), lambda b, i: (b, 0, 0))
    return pl.pallas_call(
        functools.partial(_attn_kernel, banded=True, n_blocks=nb_),
        grid=(b_, nb_),
        in_specs=[pl.BlockSpec((1, blk, hq), cur), kspec(prv), kspec(cur), kspec(nxt),
                  kspec(prv), kspec(cur), kspec(nxt),
                  tspec(tcur), tspec(tcur), tspec(tprv), tspec(tprv), tspec(tnxt), tspec(tnxt),
                  cspec, cspec, pl.BlockSpec(memory_space=pltpu.SMEM)],
        out_specs=pl.BlockSpec((1, blk, hq), cur),
        out_shape=jax.ShapeDtypeStruct((b_, n, hq), F32),
        compiler_params=_params(2),
        name="window_attention",
    )(q, k, k, k, v, v, v, cos, sin, cos, sin, cos, sin, kc, vc, sink)


def context_attention_pallas(q, kc, vc, sink):
    b_, lc, hq = q.shape
    hk = kc.shape[-1]
    blk = min(lc, C_BLOCK)
    cspec = pl.BlockSpec((1, lc, hk), lambda b, i: (b, 0, 0))
    return pl.pallas_call(
        functools.partial(_attn_kernel, banded=False, n_blocks=lc // blk),
        grid=(b_, lc // blk),
        in_specs=[pl.BlockSpec((1, blk, hq), lambda b, i: (b, i, 0)), cspec, cspec,
                  pl.BlockSpec(memory_space=pltpu.SMEM)],
        out_specs=pl.BlockSpec((1, blk, hq), lambda b, i: (b, i, 0)),
        out_shape=jax.ShapeDtypeStruct((b_, lc, hq), F32),
        compiler_params=_params(2),
        name="context_attention",
    )(q, kc, vc, sink)


def _scan8(a, u, reverse):
    row = lax.broadcasted_iota(jnp.int32, a.shape, 0)
    for s in (1, 2, 4):
        shift = (SUBLANES - s) if reverse else s
        a_sh = pltpu.roll(a, shift, axis=0)
        u_sh = pltpu.roll(u, shift, axis=0)
        valid = (row < SUBLANES - s) if reverse else (row >= s)
        u = jnp.where(valid, a * u_sh + u, u)
        a = jnp.where(valid, a * a_sh, a)
    return a, u


def _gelu_tanh(x):
    return 0.5 * x * (1.0 + jnp.tanh(0.7978845608028654 * (x + 0.044715 * (x * x * x))))


def _rglru_kernel(*refs, tile, n_tiles, reverse, combine):
    if combine:
        (x_ref, xp_ref, xn_ref, cw_ref, cb_ref, wbd_ref, bias_ref, lam_ref, h0_ref, hf_ref, gd_ref,
         o_ref, hl_ref, xbuf, a_s, u_s, carry) = refs
    else:
        (x_ref, xp_ref, xn_ref, cw_ref, cb_ref, wbd_ref, bias_ref, lam_ref, h0_ref,
         o_ref, hl_ref, xbuf, a_s, u_s, carry) = refs
    i = pl.program_id(1)
    t = (n_tiles - 1 - i) if reverse else i
    w = x_ref.shape[-1]

    @pl.when(i == 0)
    def _():
        carry[...] = jnp.broadcast_to(h0_ref[0], (SUBLANES, w))

    xbuf[pl.ds(0, HALO), :] = jnp.where(t > 0, xp_ref[0], 0.0)
    xbuf[pl.ds(HALO, tile), :] = x_ref[0]
    xbuf[pl.ds(HALO + tile, HALO), :] = jnp.where(t < n_tiles - 1, xn_ref[0], 0.0)
    xc = cb_ref[...] + sum(cw_ref[pl.ds(k, 1), :] * xbuf[pl.ds(HALO - 1 + k, tile), :] for k in range(D_CONV))

    z = _dot(xc.astype(BF16), wbd_ref[...]) + bias_ref[...]
    r = jax.nn.sigmoid(z[:, :w])
    gi = jax.nn.sigmoid(z[:, w:])
    lam = lam_ref[...]
    log_a = (-D_C * jnp.log(1.0 + jnp.exp(-lam))) * r
    a = jnp.exp(log_a)
    a_s[...] = a
    u_s[...] = jnp.sqrt(1.0 - a * a) * gi * xc

    n_groups = tile // SUBLANES

    def step(g, c):
        gg = (n_groups - 1 - g) if reverse else g
        r0 = pl.multiple_of(gg * SUBLANES, SUBLANES)
        ac, uc = _scan8(a_s[pl.ds(r0, SUBLANES), :], u_s[pl.ds(r0, SUBLANES), :], reverse)
        h = ac * c + uc
        o_ref[0, pl.ds(r0, SUBLANES), :] = h
        last = h[0:1, :] if reverse else h[SUBLANES - 1:SUBLANES, :]
        return jnp.broadcast_to(last, (SUBLANES, w))

    c_fin = lax.fori_loop(0, n_groups, step, carry[...], unroll=4 if n_groups % 4 == 0 else 1)
    carry[...] = c_fin

    if combine:
        o_ref[0] = (hf_ref[0] + o_ref[0]) * _gelu_tanh(gd_ref[0])

    @pl.when(i == n_tiles - 1)
    def _():
        hl_ref[0] = c_fin[0:1, :]


def _rglru_sweep(xd, conv_w, conv_b, wbd, bias, lam, h0, hf=None, gd=None, *, reverse):
    b_, n, w = xd.shape
    combine = hf is not None
    tile = min(n, 512)
    n_tiles = n // tile
    blocks_per_tile = tile // HALO
    n_blocks = n // HALO

    def tmap(b, i):
        return (b, (n_tiles - 1 - i) if reverse else i, 0)

    def pmap(b, i):
        t = (n_tiles - 1 - i) if reverse else i
        return (b, jnp.maximum(t * blocks_per_tile - 1, 0), 0)

    def nmap(b, i):
        t = (n_tiles - 1 - i) if reverse else i
        return (b, jnp.minimum((t + 1) * blocks_per_tile, n_blocks - 1), 0)

    const2 = lambda b, i: (0, 0)
    in_specs = [
        pl.BlockSpec((1, tile, w), tmap),
        pl.BlockSpec((1, HALO, w), pmap),
        pl.BlockSpec((1, HALO, w), nmap),
        pl.BlockSpec((D_CONV, w), const2),
        pl.BlockSpec((1, w), const2),
        pl.BlockSpec((w, 2 * w), const2),
        pl.BlockSpec((1, 2 * w), const2),
        pl.BlockSpec((1, w), const2),
        pl.BlockSpec((1, 1, w), lambda b, i: (b, 0, 0)),
    ]
    args = [xd, xd, xd, conv_w, conv_b, wbd, bias, lam, h0]
    if combine:
        in_specs += [pl.BlockSpec((1, tile, w), tmap), pl.BlockSpec((1, tile, w), tmap)]
        args += [hf, gd]
    return pl.pallas_call(
        functools.partial(_rglru_kernel, tile=tile, n_tiles=n_tiles, reverse=reverse, combine=combine),
        grid=(b_, n_tiles),
        in_specs=in_specs,
        out_specs=[pl.BlockSpec((1, tile, w), tmap), pl.BlockSpec((1, 1, w), lambda b, i: (b, 0, 0))],
        out_shape=[jax.ShapeDtypeStruct((b_, n, w), F32), jax.ShapeDtypeStruct((b_, 1, w), F32)],
        scratch_shapes=[pltpu.VMEM((tile + 2 * HALO, w), F32), pltpu.VMEM((tile, w), F32),
                        pltpu.VMEM((tile, w), F32), pltpu.VMEM((SUBLANES, w), F32)],
        compiler_params=_params(2),
        name="rglru_bwd_combine" if combine else "rglru_fwd",
    )(*args)


def _block_diag(wz):
    eye = jnp.eye(D_BLOCKS, dtype=wz.dtype)
    return jnp.einsum('hij,hg->higj', wz, eye).reshape(D_WIDTH, D_WIDTH)


def rglru_pallas(xd, gd, conv_w, conv_b, w_r, b_r, w_i, b_i, lam, s0):
    cb = conv_b.reshape(1, D_WIDTH)
    outs = []
    for z in range(2):
        wbd = jnp.concatenate([_block_diag(w_r[z]), _block_diag(w_i[z])], axis=1).astype(BF16)
        bias = jnp.concatenate([b_r[z], b_i[z]]).reshape(1, 2 * D_WIDTH)
        outs.append((wbd, bias, lam[z].reshape(1, D_WIDTH)))
    hf, sf = _rglru_sweep(xd, conv_w, cb, *outs[0], s0[0][:, None, :], reverse=False)
    y, sb = _rglru_sweep(xd, conv_w, cb, *outs[1], s0[1][:, None, :], hf, gd, reverse=True)
    return y, (sf[:, 0], sb[:, 0])


AFF_BITS = 31


def _lane_cumsum_exclusive(x01):
    e_, n = x01.shape
    li = lax.broadcasted_iota(jnp.int32, (LANES, LANES), 0)
    lj = lax.broadcasted_iota(jnp.int32, (LANES, LANES), 1)
    strict_upper = (li < lj).astype(BF16)
    ones = jnp.ones((LANES, LANES), BF16)
    base = jnp.zeros((e_, LANES), F32)
    outs = []
    for g in range(n // LANES):
        xg = x01[:, g * LANES:(g + 1) * LANES].astype(BF16)
        outs.append(base + _dot(xg, strict_upper))
        base = base + _dot(xg, ones)
    return jnp.concatenate(outs, axis=1)


def _route_kernel(aff_ref, idx_ref, gate_ref, sp_s, af_s, ps_s, pe_s, *, cap):
    aff = aff_ref[0]
    e_, n = aff.shape
    n_groups = n // LANES
    rb_rows = min(cap, LANES)
    bits = pltpu.bitcast(aff, jnp.int32)
    thr = jnp.zeros((e_, 1), jnp.int32)
    for bit in range(AFF_BITS - 1, -1, -1):
        cand = thr | (1 << bit)
        cnt = jnp.sum((bits >= cand).astype(F32), axis=1, keepdims=True)
        thr = jnp.where(cnt >= cap, cand, thr)
    gt = bits > thr
    eq = bits == thr
    need = cap - jnp.sum(gt.astype(F32), axis=1, keepdims=True)
    sel = gt | (eq & (_lane_cumsum_exclusive(eq.astype(F32)) < need))
    pos = _lane_cumsum_exclusive(sel.astype(F32))
    selpos = jnp.where(sel, pos, -1.0)
    tok_g = lax.broadcasted_iota(jnp.int32, (n, n_groups), 0) // LANES
    grp = lax.broadcasted_iota(jnp.int32, (n, n_groups), 1)
    cnt_g = _dot(sel.astype(BF16), (tok_g == grp).astype(BF16))
    gi = lax.broadcasted_iota(jnp.int32, (n_groups, n_groups), 0)
    gj = lax.broadcasted_iota(jnp.int32, (n_groups, n_groups), 1)
    start_g = _dot(cnt_g, (gi < gj).astype(F32), HI)
    ps_s[...] = start_g
    pe_s[...] = start_g + cnt_g
    for g in range(n_groups):
        sp_s[g] = selpos[:, g * LANES:(g + 1) * LANES]
        af_s[g] = aff[:, g * LANES:(g + 1) * LANES]
    slot0 = lax.broadcasted_iota(jnp.int32, (rb_rows, LANES), 0).astype(F32)
    lane = lax.broadcasted_iota(jnp.int32, (1, LANES), 1).astype(F32)

    def per_block(k, carry):
        e = k // (cap // rb_rows)
        rb = k % (cap // rb_rows)
        first = lax.convert_element_type(rb * rb_rows, F32)
        slot = slot0 + first

        def per_group(g, acc):
            acc_i, acc_g = acc
            hit = sp_s[g, pl.ds(e, 1), :] == slot
            tok = lane + lax.convert_element_type(g * LANES, F32)
            return (acc_i + jnp.where(hit, tok, 0.0), acc_g + jnp.where(hit, af_s[g, pl.ds(e, 1), :], 0.0))

        g_lo = jnp.sum((pe_s[pl.ds(e, 1), :] <= first).astype(F32), axis=1, keepdims=True)[0, 0].astype(jnp.int32)
        g_hi = jnp.sum((ps_s[pl.ds(e, 1), :] < first + rb_rows).astype(F32), axis=1, keepdims=True)[0, 0].astype(jnp.int32)
        zero = jnp.zeros((rb_rows, LANES), F32)
        acc_i, acc_g = lax.fori_loop(g_lo, g_hi, per_group, (zero, zero))
        rows = pl.ds(pl.multiple_of(rb * rb_rows, rb_rows), rb_rows)
        idx_ref[0, e, rows, :] = jnp.sum(acc_i, axis=1, keepdims=True).astype(jnp.int32)
        gate_ref[0, e, rows, :] = jnp.sum(acc_g, axis=1, keepdims=True)
        return carry

    lax.fori_loop(0, e_ * (cap // rb_rows), per_block, 0)


def route(aff):
    b_, e_, n = aff.shape
    cap = max(1, EC_FACTOR * n // N_EXPERTS)
    grp_scr = lambda: pltpu.VMEM((n // LANES, e_, LANES), F32)
    cnt_scr = lambda: pltpu.VMEM((e_, n // LANES), F32)
    idx, gate = pl.pallas_call(
        functools.partial(_route_kernel, cap=cap),
        grid=(b_,),
        in_specs=[pl.BlockSpec((1, e_, n), lambda b: (b, 0, 0))],
        out_specs=[pl.BlockSpec((1, e_, cap, 1), lambda b: (b, 0, 0, 0)),
                   pl.BlockSpec((1, e_, cap, 1), lambda b: (b, 0, 0, 0))],
        out_shape=[jax.ShapeDtypeStruct((b_, e_, cap, 1), jnp.int32), jax.ShapeDtypeStruct((b_, e_, cap, 1), F32)],
        scratch_shapes=[grp_scr(), grp_scr(), cnt_scr(), cnt_scr()],
        compiler_params=_params(1),
        name="route",
    )(aff)
    return idx.reshape(b_, e_, 1, cap), gate


ROW_UNROLL = 8


def _gather_kernel(idx_ref, h_ref, o_ref, *, cap):
    def body(r, carry):
        o_ref[0, 0, pl.ds(r, 1), :] = h_ref[0, pl.ds(idx_ref[0, 0, 0, r], 1), :]
        return carry
    lax.fori_loop(0, cap, body, 0, unroll=ROW_UNROLL)


def gather_rows(h, idx):
    b_, n, d = h.shape
    _, e_, _, cap = idx.shape
    return pl.pallas_call(
        functools.partial(_gather_kernel, cap=cap),
        grid=(b_, e_),
        in_specs=[pl.BlockSpec((1, 1, 1, cap), lambda b, e: (b, e, 0, 0), memory_space=pltpu.SMEM),
                  pl.BlockSpec((1, n, d), lambda b, e: (b, 0, 0), pipeline_mode=pl.Buffered(1))],
        out_specs=pl.BlockSpec((1, 1, cap, d), lambda b, e: (b, e, 0, 0)),
        out_shape=jax.ShapeDtypeStruct((b_, e_, cap, d), F32),
        compiler_params=_params(2),
        name="gather_rows",
    )(idx, h)


def _expert_ffn_kernel(x_ref, g_ref, g2_ref, w1_ref, w3_ref, w2_ref, o_ref):
    x = x_ref[0, 0].astype(BF16)
    hid = (_silu(_dot(x, w1_ref[0])) * _dot(x, w3_ref[0])).astype(BF16)
    o_ref[0, 0] = _dot(hid, w2_ref[0]) * g_ref[0, 0] * g2_ref[0]


def expert_ffn(xs, gate, g2, w1, w3, w2):
    b_, e_, c_, d = xs.shape
    f = w1.shape[-1]
    tm = min(c_, 512)
    return pl.pallas_call(
        _expert_ffn_kernel,
        grid=(e_, b_, c_ // tm),
        in_specs=[
            pl.BlockSpec((1, 1, tm, d), lambda e, b, m: (b, e, m, 0)),
            pl.BlockSpec((1, 1, tm, 1), lambda e, b, m: (b, e, m, 0)),
            pl.BlockSpec((1, 1, d), lambda e, b, m: (b, 0, 0)),
            pl.BlockSpec((1, d, f), lambda e, b, m: (e, 0, 0)),
            pl.BlockSpec((1, d, f), lambda e, b, m: (e, 0, 0)),
            pl.BlockSpec((1, f, d), lambda e, b, m: (e, 0, 0)),
        ],
        out_specs=pl.BlockSpec((1, 1, tm, d), lambda e, b, m: (b, e, m, 0)),
        out_shape=jax.ShapeDtypeStruct((b_, e_, c_, d), F32),
        compiler_params=_params(3),
        name="expert_ffn",
    )(xs, gate, g2[:, None, :], w1, w3, w2)


def _combine_kernel(idx_ref, y_ref, x1_hbm, o_hbm, acc, sem, *, cap, n_experts):
    b = pl.program_id(0)
    e = pl.program_id(1)

    @pl.when(e == 0)
    def _():
        cp = pltpu.make_async_copy(x1_hbm.at[b], acc, sem.at[0])
        cp.start()
        cp.wait()

    def body(r, carry):
        row = pl.ds(idx_ref[0, 0, 0, r], 1)
        acc[row, :] = acc[row, :] + y_ref[0, 0, pl.ds(r, 1), :]
        return carry
    lax.fori_loop(0, cap, body, 0, unroll=ROW_UNROLL)

    @pl.when(e == n_experts - 1)
    def _():
        cp = pltpu.make_async_copy(acc, o_hbm.at[b], sem.at[1])
        cp.start()
        cp.wait()


def combine_rows(x1, y, idx):
    b_, n, d = x1.shape
    _, e_, _, cap = idx.shape
    return pl.pallas_call(
        functools.partial(_combine_kernel, cap=cap, n_experts=e_),
        grid=(b_, e_),
        in_specs=[pl.BlockSpec((1, 1, 1, cap), lambda b, e: (b, e, 0, 0), memory_space=pltpu.SMEM),
                  pl.BlockSpec((1, 1, cap, d), lambda b, e: (b, e, 0, 0)),
                  pl.BlockSpec(memory_space=pl.ANY)],
        out_specs=pl.BlockSpec(memory_space=pl.ANY),
        out_shape=jax.ShapeDtypeStruct((b_, n, d), F32),
        scratch_shapes=[pltpu.VMEM((n, d), F32), pltpu.SemaphoreType.DMA((2,))],
        compiler_params=_params(2),
        name="combine_rows",
    )(idx, y, x1)


def moe_residual(x1, h2, aff, g2, w1, w3, w2):
    idx, gate = route(aff)
    y = expert_ffn(gather_rows(h2, idx), gate, g2, w1, w3, w2)
    return combine_rows(x1, y, idx)


def kernel(x, c, ctx, c_ctx, w_mod, b_mod, norm1_w, norm2_w, final_norm_w,
           ev_w_in, ev_w_out, a_conv_w, a_log, a_dt_bias, a_norm_w, b_lb_logits, b_norm_w,
           od_w_in, od_w_out, c_sink, d_conv_w, d_conv_b, d_w_r, d_b_r, d_w_i, d_b_i, d_lambda,
           moe_router, moe_w1, moe_w3, moe_w2):
    b_, n, d = x.shape
    cos, sin = _rope_tables(n)
    lb_all = jnp.cumsum(jax.nn.softmax(b_lb_logits.astype(F32), axis=0), axis=0)
    lb_all = lb_all - lb_all[0:1]
    c_rows = jnp.concatenate([c, c_ctx[None, :], jnp.zeros((-(b_ + 1) % SUBLANES, d), F32)], axis=0)
    mod_all = modulation(c_rows, w_mod, b_mod)
    for l in range(DEPTH):
        last = l == DEPTH - 1
        j = l // 2
        w1b, w3b, w2b = moe_w1[l].astype(BF16), moe_w3[l].astype(BF16), moe_w2[l].astype(BF16)
        router_t = moe_router[l].T.astype(BF16)
        mod = mod_all[l, :b_]
        mod_c = jnp.broadcast_to(mod_all[l, b_], (b_, 6 * d))
        sh1, sc1, g1, sh2, sc2, g2 = jnp.split(mod, 6, axis=-1)
        csh1, csc1, cg1, csh2, csc2, cg2 = jnp.split(mod_c, 6, axis=-1)
        if l % 2 == 0:
            w_in, w_out = _even_w_in(ev_w_in[j]), ev_w_out[j].astype(BF16)
            zero = jnp.zeros((b_, A_HEADS, A_DK, A_DV), F32)
            pars = (a_conv_w[j], a_log[j], a_dt_bias[j], a_norm_w[j], lb_all[j].reshape(2, HB), b_norm_w[j])
            o_c, st = even_mixer_pallas(in_proj(ctx, norm1_w[l], csh1, csc1, w_in, EVEN_GROUPS), *pars,
                                        ((zero, zero), (zero, zero)))
            o_l, _ = even_mixer_pallas(in_proj(x, norm1_w[l], sh1, sc1, w_in, EVEN_GROUPS), *pars, st)
            parts_l, parts_c = [o_l], [o_c]
        else:
            w_in, w_out = od_w_in[j].astype(BF16), od_w_out[j].astype(BF16)
            qc, kc, vc, xdc, gdc = in_proj(ctx, norm1_w[l], csh1, csc1, w_in, ODD_SIZES)
            ql, kl, vl, xdl, gdl = in_proj(x, norm1_w[l], sh1, sc1, w_in, ODD_SIZES)
            att_l = window_attention_pallas(ql, kl, vl, kc, vc, c_sink[j], cos, sin)
            rg_pars = (d_conv_w[j], d_conv_b[j], d_w_r[j], d_b_r[j], d_w_i[j], d_b_i[j], d_lambda[j])
            zero = jnp.zeros((b_, D_WIDTH), F32)
            rg_c, st = rglru_pallas(xdc, gdc, *rg_pars, (zero, zero))
            rg_l, _ = rglru_pallas(xdl, gdl, *rg_pars, st)
            parts_l = [att_l, rg_l]
            if not last:
                parts_c = [context_attention_pallas(qc, kc, vc, c_sink[j]), rg_c]
        x1, h2, aff = post_mixer(parts_l, w_out, x, g1, norm2_w[l], sh2, sc2, router_t)
        x = moe_residual(x1, h2, aff, g2, w1b, w3b, w2b)
        if not last:
            c1, hc2, affc = post_mixer(parts_c, w_out, ctx, cg1, norm2_w[l], csh2, csc2, router_t)
            ctx = moe_residual(c1, hc2, affc, cg2, w1b, w3b, w2b)
    return final_norm(x, final_norm_w)
```

```python
import functools

import jax
import jax.numpy as jnp
from jax import lax
from jax.experimental import pallas as pl
from jax.experimental.pallas import tpu as pltpu

D_MODEL = 1024
DEPTH = 4
GRID_W = 64
EPS = 1e-6
F32 = jnp.float32
BF16 = jnp.bfloat16
HI = lax.Precision.HIGHEST

A_HEADS = 4
A_DK = 128
A_DV = 128
A_CONV = 4
A_CHUNK = 64
A_QKV = 2 * A_HEADS * A_DK + A_HEADS * A_DV
B_HEADS = 4
B_DK = 128
B_DV = 128
B_CHUNK = 64
C_HEADS = 8
C_KV_HEADS = 2
C_HD = 64
C_WIN = 128
C_BLOCK = 128
ROPE_THETA = 10000.0
D_WIDTH = 512
D_BLOCKS = 8
D_BW = D_WIDTH // D_BLOCKS
D_CONV = 4
D_C = 8.0
N_EXPERTS = 16
EXPERT_FF = 1024
EC_FACTOR = 2

EVEN_SIZES = (A_QKV, A_HEADS * A_DV, 2 * A_HEADS, 2 * A_HEADS,
              B_HEADS * B_DK, B_HEADS * B_DV, 2 * B_HEADS * B_DK, B_HEADS * B_DV)
ODD_SIZES = (C_HEADS * C_HD, C_KV_HEADS * C_HD, C_KV_HEADS * C_HD, D_WIDTH, D_WIDTH)

LANES = 128
SUBLANES = 8
VMEM_LIMIT_BYTES = 56 * 1024 * 1024

ROW_TILE = 512
HALO = SUBLANES


def _cuts(sizes):
    out, acc = [], 0
    for s in sizes[:-1]:
        acc += s
        out.append(acc)
    return out


def _params(n_axes):
    return pltpu.CompilerParams(dimension_semantics=("arbitrary",) * n_axes, vmem_limit_bytes=VMEM_LIMIT_BYTES)


def _dot(a, b, precision=None):
    return jnp.dot(a, b, preferred_element_type=F32, precision=precision)


def _dot_nt(a, b, precision=None):
    return lax.dot_general(a, b, (((1,), (1,)), ((), ())), preferred_element_type=F32, precision=precision)


def _dot_tn(a, b):
    return lax.dot_general(a, b, (((0,), (0,)), ((), ())), preferred_element_type=F32)


def _silu(x):
    return x * jax.nn.sigmoid(x)


def _rms(x):
    return x * lax.rsqrt(jnp.mean(x * x, axis=-1, keepdims=True) + EPS)


MOD_COLS = 1536


def _mod_kernel(c_ref, w_ref, b_ref, o_ref):
    o_ref[0] = _dot(_silu(c_ref[...]), w_ref[0]) + b_ref[0]


def modulation(c_rows, w_mod, b_mod):
    r, d = c_rows.shape
    depth, _, wide = w_mod.shape
    return pl.pallas_call(
        _mod_kernel,
        grid=(depth, wide // MOD_COLS),
        in_specs=[pl.BlockSpec((r, d), lambda l, j: (0, 0)),
                  pl.BlockSpec((1, d, MOD_COLS), lambda l, j: (l, 0, j)),
                  pl.BlockSpec((1, 1, MOD_COLS), lambda l, j: (l, 0, j))],
        out_specs=pl.BlockSpec((1, r, MOD_COLS), lambda l, j: (l, 0, j)),
        out_shape=jax.ShapeDtypeStruct((depth, r, wide), F32),
        compiler_params=_params(2),
        name="modulation",
    )(c_rows, w_mod, b_mod[:, None, :])


def _norm_mod(x, nw, shift, scale):
    return _rms(x) * nw * (1.0 + scale) + shift


def _in_proj_kernel(x_ref, nw_ref, sh_ref, sc_ref, w_ref, *o_refs, splits):
    h = _norm_mod(x_ref[0], nw_ref[...], sh_ref[0], sc_ref[0]).astype(BF16)
    off = 0
    for o_ref, s in zip(o_refs, splits):
        o_ref[0] = _dot(h, w_ref[:, off:off + s])
        off += s


def in_proj(x, norm_w, shift, scale, w, splits):
    b_, n, d = x.shape
    tm = min(n, ROW_TILE)
    ntot = sum(splits)
    vec = pl.BlockSpec((1, 1, d), lambda b, i: (b, 0, 0))
    return pl.pallas_call(
        functools.partial(_in_proj_kernel, splits=tuple(splits)),
        grid=(b_, n // tm),
        in_specs=[pl.BlockSpec((1, tm, d), lambda b, i: (b, i, 0)),
                  pl.BlockSpec((1, d), lambda b, i: (0, 0)), vec, vec,
                  pl.BlockSpec((d, ntot), lambda b, i: (0, 0))],
        out_specs=[pl.BlockSpec((1, tm, s), lambda b, i: (b, i, 0)) for s in splits],
        out_shape=[jax.ShapeDtypeStruct((b_, n, s), F32) for s in splits],
        compiler_params=_params(2),
        name="in_proj",
    )(x, norm_w.reshape(1, d), shift[:, None, :], scale[:, None, :], w)


def _post_mixer_kernel(*refs, n_parts):
    parts = refs[:n_parts]
    w_ref, x_ref, g1_ref, nw_ref, sh_ref, sc_ref, rt_ref, x1_ref, h2_ref, aff_ref = refs[n_parts:]
    y = None
    off = 0
    for p_ref in parts:
        k = p_ref.shape[-1]
        t = _dot(p_ref[0].astype(BF16), w_ref[off:off + k, :])
        y = t if y is None else y + t
        off += k
    x1 = x_ref[0] + g1_ref[0] * y
    x1_ref[0] = x1
    h2 = _norm_mod(x1, nw_ref[...], sh_ref[0], sc_ref[0])
    h2_ref[0] = h2
    logits = _dot_nt(rt_ref[...], h2.astype(BF16))
    e = jnp.exp(logits - jnp.max(logits, axis=0, keepdims=True))
    aff_ref[0] = e / jnp.sum(e, axis=0, keepdims=True)


def post_mixer(parts, w_out, x, g1, norm_w, shift, scale, router_t):
    b_, n, d = x.shape
    e_ = router_t.shape[0]
    tm = min(n, ROW_TILE)
    vec = pl.BlockSpec((1, 1, d), lambda b, i: (b, 0, 0))
    tok = lambda k: pl.BlockSpec((1, tm, k), lambda b, i: (b, i, 0))
    return pl.pallas_call(
        functools.partial(_post_mixer_kernel, n_parts=len(parts)),
        grid=(b_, n // tm),
        in_specs=[tok(p.shape[-1]) for p in parts] + [
            pl.BlockSpec(w_out.shape, lambda b, i: (0, 0)), tok(d), vec,
            pl.BlockSpec((1, d), lambda b, i: (0, 0)), vec, vec,
            pl.BlockSpec((e_, d), lambda b, i: (0, 0))],
        out_specs=[tok(d), tok(d), pl.BlockSpec((1, e_, tm), lambda b, i: (b, 0, i))],
        out_shape=[jax.ShapeDtypeStruct((b_, n, d), F32), jax.ShapeDtypeStruct((b_, n, d), F32),
                   jax.ShapeDtypeStruct((b_, e_, n), F32)],
        compiler_params=_params(2),
        name="post_mixer",
    )(*parts, w_out, x, g1[:, None, :], norm_w.reshape(1, d), shift[:, None, :], scale[:, None, :], router_t)


def _final_kernel(x_ref, nw_ref, o_ref):
    o_ref[0] = _rms(x_ref[0]) * nw_ref[...]


def final_norm(x, norm_w):
    b_, n, d = x.shape
    tm = min(n, 2 * ROW_TILE)
    tok = pl.BlockSpec((1, tm, d), lambda b, i: (b, i, 0))
    return pl.pallas_call(
        _final_kernel,
        grid=(b_, n // tm),
        in_specs=[tok, pl.BlockSpec((1, d), lambda b, i: (0, 0))],
        out_specs=tok,
        out_shape=jax.ShapeDtypeStruct((b_, n, d), F32),
        compiler_params=_params(2),
        name="final_norm",
    )(x, norm_w.reshape(1, d))


CHUNK = A_CHUNK
SUB = 16
EVEN_TILE = 256
HA = A_HEADS * A_DK
HB = B_HEADS * B_DK
AB_PAD = LANES
EVEN_GROUPS = (A_QKV, A_HEADS * A_DV, AB_PAD, HB, B_HEADS * B_DV, 2 * HB, B_HEADS * B_DV)
NEUMANN_STEPS = 5


def _split2(x):
    hi = x.astype(BF16)
    return hi, (x - hi.astype(F32)).astype(BF16)


def _dot_split(a, b):
    (ah, al), (bh, bl) = a, b
    return _dot(ah, bh) + _dot(ah, bl) + _dot(al, bh)


def _dot_exact_lhs(a_bf16, x):
    x1 = x.astype(BF16)
    r1 = x - x1.astype(F32)
    x2 = r1.astype(BF16)
    x3 = (r1 - x2.astype(F32)).astype(BF16)
    return _dot(a_bf16, x1) + _dot(a_bf16, x2) + _dot(a_bf16, x3)


def _even_kernel(*refs, tile, n_tiles, reverse, combine):
    if combine:
        (qkv_ref, qkvp_ref, qkvn_ref, ab_ref, qb_ref, ib_ref, fb_ref, cw_ref, exp_ref, alog_ref, dtb_ref, lb_ref,
         sa0_ref, sb0_ref, of_ref, ga_ref, gb_ref, anw_ref, bnw_ref,
         o_ref, sa_ref, sb_ref, xbuf, g_scr, st_a, st_b) = refs
    else:
        (qkv_ref, qkvp_ref, qkvn_ref, ab_ref, qb_ref, ib_ref, fb_ref, cw_ref, exp_ref, alog_ref, dtb_ref, lb_ref,
         sa0_ref, sb0_ref,
         o_ref, sa_ref, sb_ref, xbuf, g_scr, st_a, st_b) = refs
    i = pl.program_id(1)
    t = (n_tiles - 1 - i) if reverse else i

    @pl.when(i == 0)
    def _():
        st_a[...] = sa0_ref[0]
        st_b[...] = sb0_ref[0]

    L = CHUNK
    n_chunks = tile // L
    n_blocks = tile // SUB
    chunk_order = range(n_chunks - 1, -1, -1) if reverse else range(n_chunks)
    block_order = range(n_blocks - 1, -1, -1) if reverse else range(n_blocks)

    ti = lax.broadcasted_iota(jnp.int32, (tile, tile), 0)
    tj = lax.broadcasted_iota(jnp.int32, (tile, tile), 1)
    t_incl = (tj >= ti) if reverse else (tj <= ti)
    tri_chunk = (t_incl & ((ti // L) == (tj // L))).astype(BF16)
    tri_blk = (t_incl & ((ti // SUB) == (tj // SUB))).astype(BF16)
    ones_blk = ((ti // SUB) == (tj // SUB)).astype(BF16)
    ii = lax.broadcasted_iota(jnp.int32, (L, L), 0)
    jj = lax.broadcasted_iota(jnp.int32, (L, L), 1)
    incl = (jj >= ii) if reverse else (jj <= ii)
    strict = (jj > ii) if reverse else (jj < ii)
    eye = (ii == jj).astype(F32)
    last_row = 0 if reverse else L - 1

    xbuf[pl.ds(0, HALO), :] = jnp.where(t > 0, qkvp_ref[0], 0.0)
    xbuf[pl.ds(HALO, tile), :] = qkv_ref[0]
    xbuf[pl.ds(HALO + tile, HALO), :] = jnp.where(t < n_tiles - 1, qkvn_ref[0], 0.0)
    conv = _silu(sum(cw_ref[pl.ds(k, 1), :] * xbuf[pl.ds(HALO - 1 + k, tile), :] for k in range(A_CONV)))
    ab_x = _dot(ab_ref[0], exp_ref[...], HI)
    g_all = -jnp.exp(alog_ref[...]) * jax.nn.softplus(ab_x[:, :HA] + dtb_ref[...])
    beta_all = jax.nn.sigmoid(ab_x[:, HA:])
    cum_all = _dot_exact_lhs(tri_chunk, g_all)
    ecum_all = jnp.exp(cum_all)

    chains = [(c, h) for c in chunk_order for h in range(A_HEADS)]
    pre = {}
    neg_m = []
    for c, h in chains:
        rs = slice(c * L, (c + 1) * L)
        hs = slice(h * A_DK, (h + 1) * A_DK)
        qh = conv[rs, hs]
        kh = conv[rs, HA + h * A_DK:HA + (h + 1) * A_DK]
        q = qh * (lax.rsqrt(jnp.sum(qh * qh, axis=-1, keepdims=True) + EPS) * (A_DK ** -0.5))
        k = kh * lax.rsqrt(jnp.sum(kh * kh, axis=-1, keepdims=True) + EPS)
        v = conv[rs, 2 * HA + h * A_DV:2 * HA + (h + 1) * A_DV]
        beta, cum, ecum = beta_all[rs, hs], cum_all[rs, hs], ecum_all[rs, hs]
        r_ll = cum.T[:L, :]
        decay = jnp.where(incl, jnp.exp(jnp.where(incl, cum[:, :L] - r_ll, 0.0)), 0.0)
        kb = k * beta
        k16 = k.astype(BF16)
        neg_m.append(jnp.where(strict, -_dot_nt(kb.astype(BF16), k16) * decay, 0.0))
        qk = _dot_nt(q.astype(BF16), k16) * decay
        total = cum[last_row:last_row + 1, :]
        pre[(c, h)] = dict(rhs=jnp.concatenate([v * beta, kb * ecum], axis=1), qk=qk.astype(BF16),
                           qd=(q * ecum).astype(BF16), kd=(k * jnp.exp(total - cum)).astype(BF16),
                           gl=jnp.exp(total))
    p = [_split2(m) for m in neg_m]
    tinv = [eye + m for m in neg_m]
    for _ in range(NEUMANN_STEPS):
        p = [_split2(_dot_split(x, x)) for x in p]
        tinv = [tv + _dot_split(_split2(tv), x) for tv, x in zip(tinv, p)]
    sol_of = {ch: _dot_split(_split2(tv), _split2(pre[ch]["rhs"])) for tv, ch in zip(tinv, chains)}

    for c in chunk_order:
        rs = slice(c * L, (c + 1) * L)
        s_old = [st_a[h] for h in range(A_HEADS)]
        s16 = [s.astype(BF16) for s in s_old]
        ws = [_dot(sol_of[(c, h)][:, A_DV:].astype(BF16), s16[h]) for h in range(A_HEADS)]
        qs_ = [_dot(pre[(c, h)]["qd"], s16[h]) for h in range(A_HEADS)]
        v_new = [(sol_of[(c, h)][:, :A_DV] - ws[h]).astype(BF16) for h in range(A_HEADS)]
        for h in range(A_HEADS):
            o_ref[0, rs, h * A_DV:(h + 1) * A_DV] = qs_[h] + _dot(pre[(c, h)]["qk"], v_new[h])
            st_a[h] = s_old[h] * pre[(c, h)]["gl"] + _dot_tn(pre[(c, h)]["kd"], v_new[h])

    hq = _silu(qb_ref[0])
    hv = ib_ref[0]
    lb = lb_ref[...]
    fg = lb + (1.0 - lb) * jax.nn.sigmoid(fb_ref[0])
    lf = jnp.log(fg)
    hk = 1.0 - fg
    b_all = _dot_exact_lhs(tri_blk, lf)
    e_all = _dot_exact_lhs(ones_blk, lf)
    qs_all = (hq * jnp.exp(b_all)).astype(BF16)
    ks_all = (hk * jnp.exp(e_all - b_all)).astype(BF16)
    hv16 = hv.astype(BF16)
    row_in_blk = lax.broadcasted_iota(jnp.int32, (tile, HB), 0) % SUB
    od = [jnp.zeros((tile, B_DV), F32) for _ in range(B_HEADS)]
    for s_ in range(SUB):
        valid = (row_in_blk + s_ < SUB) if reverse else (row_in_blk >= s_)
        shift = ((tile - s_) if reverse else s_) % tile
        roll = (lambda x: x) if s_ == 0 else (lambda x: pltpu.roll(x, shift, axis=0))
        prod = jnp.where(valid, hq * roll(hk) * jnp.exp(jnp.where(valid, b_all - roll(b_all), 0.0)), 0.0)
        vv = roll(hv)
        for h in range(B_HEADS):
            hs = slice(h * B_DK, (h + 1) * B_DK)
            od[h] = od[h] + jnp.sum(prod[:, hs], axis=-1, keepdims=True) * vv[:, hs]
    for blk in block_order:
        bs = slice(blk * SUB, (blk + 1) * SUB)
        for h in range(B_HEADS):
            hs = slice(h * B_DK, (h + 1) * B_DK)
            g_scr[blk * B_HEADS + h] = _dot_tn(hv16[bs, hs], ks_all[bs, hs])

    st = [st_b[h] for h in range(B_HEADS)]
    for blk in block_order:
        bs = slice(blk * SUB, (blk + 1) * SUB)
        for h in range(B_HEADS):
            hs = slice(h * B_DK, (h + 1) * B_DK)
            o_ref[0, bs, HA + h * B_DV:HA + (h + 1) * B_DV] = _dot_nt(qs_all[bs, hs], st[h].astype(BF16)) + od[h][bs]
            st[h] = st[h] * jnp.exp(e_all[blk * SUB:blk * SUB + 1, hs]) + g_scr[blk * B_HEADS + h]
    for h in range(B_HEADS):
        st_b[h] = st[h]

    if combine:
        tot = of_ref[0] + o_ref[0]
        for h in range(A_HEADS + B_HEADS):
            hs = slice(h * A_DV, (h + 1) * A_DV)
            nw = anw_ref[...] if h < A_HEADS else bnw_ref[...]
            gate = ga_ref[0, :, hs] if h < A_HEADS else gb_ref[0, :, pl.ds((h - A_HEADS) * B_DV, B_DV)]
            o_ref[0, :, hs] = _rms(tot[:, hs]) * nw * _silu(gate)

    @pl.when(i == n_tiles - 1)
    def _():
        sa_ref[0] = st_a[...]
        sb_ref[0] = st_b[...]


def _even_sweep(outs, z, conv_w, a_log, dt_bias, lb, s_a, s_b, of=None, a_norm_w=None, b_norm_w=None):
    qkv, ga, ab, qb, ib, fb, gb = outs
    b_, n, _ = qkv.shape
    reverse = z == 1
    combine = of is not None
    tile = min(n, EVEN_TILE)
    n_tiles = n // tile
    bpt = tile // HALO
    nblk = n // HALO
    tidx = lambda i: (n_tiles - 1 - i) if reverse else i
    tmap = lambda b, i: (b, tidx(i), 0)
    pmap = lambda b, i: (b, jnp.maximum(tidx(i) * bpt - 1, 0), 0)
    nmap = lambda b, i: (b, jnp.minimum((tidx(i) + 1) * bpt, nblk - 1), 0)
    zmap = lambda b, i: (b, tidx(i), z)
    const2 = lambda b, i: (0, 0)
    smap = lambda b, i: (b, 0, 0, 0)
    src = lax.broadcasted_iota(jnp.int32, (AB_PAD, 2 * HA), 0)
    dst = lax.broadcasted_iota(jnp.int32, (AB_PAD, 2 * HA), 1)
    expand = (src == jnp.where(dst >= HA, 2 * A_HEADS, 0) + z * A_HEADS + (dst % HA) // A_DK).astype(F32)
    rep = lambda p_: jnp.broadcast_to(p_[:, None], (A_HEADS, A_DK)).reshape(1, HA)
    in_specs = [
        pl.BlockSpec((1, tile, A_QKV), tmap), pl.BlockSpec((1, HALO, A_QKV), pmap), pl.BlockSpec((1, HALO, A_QKV), nmap),
        pl.BlockSpec((1, tile, AB_PAD), tmap), pl.BlockSpec((1, tile, HB), tmap), pl.BlockSpec((1, tile, HB), tmap),
        pl.BlockSpec((1, tile, HB), zmap),
        pl.BlockSpec((A_CONV, A_QKV), const2), pl.BlockSpec((AB_PAD, 2 * HA), const2),
        pl.BlockSpec((1, HA), const2), pl.BlockSpec((1, HA), const2), pl.BlockSpec((1, HB), const2),
        pl.BlockSpec((1, A_HEADS, A_DK, A_DV), smap), pl.BlockSpec((1, B_HEADS, B_DV, B_DK), smap),
    ]
    args = [qkv, qkv, qkv, ab, qb, ib, fb, conv_w, expand, rep(a_log[z]), rep(dt_bias[z]), lb[z].reshape(1, HB), s_a, s_b]
    if combine:
        in_specs += [pl.BlockSpec((1, tile, HA + HB), tmap), pl.BlockSpec((1, tile, HA), tmap),
                     pl.BlockSpec((1, tile, HB), tmap), pl.BlockSpec((1, A_DV), const2), pl.BlockSpec((1, B_DV), const2)]
        args += [of, ga, gb, a_norm_w.reshape(1, A_DV), b_norm_w.reshape(1, B_DV)]
    return pl.pallas_call(
        functools.partial(_even_kernel, tile=tile, n_tiles=n_tiles, reverse=reverse, combine=combine),
        grid=(b_, n_tiles),
        in_specs=in_specs,
        out_specs=[pl.BlockSpec((1, tile, HA + HB), tmap),
                   pl.BlockSpec((1, A_HEADS, A_DK, A_DV), smap), pl.BlockSpec((1, B_HEADS, B_DV, B_DK), smap)],
        out_shape=[jax.ShapeDtypeStruct((b_, n, HA + HB), F32), jax.ShapeDtypeStruct(s_a.shape, F32),
                   jax.ShapeDtypeStruct(s_b.shape, F32)],
        scratch_shapes=[pltpu.VMEM((tile + 2 * HALO, A_QKV), F32),
                        pltpu.VMEM((tile // SUB * B_HEADS, B_DV, B_DK), F32),
                        pltpu.VMEM((A_HEADS, A_DK, A_DV), F32), pltpu.VMEM((B_HEADS, B_DV, B_DK), F32)],
        compiler_params=_params(2),
        name="even_bwd_combine" if combine else "even_fwd",
    )(*args)


def even_mixer_pallas(outs, conv_w, a_log, dt_bias, a_norm_w, lb, b_norm_w, states):
    (saf, sbf), (sab, sbb) = states
    o_f, saf, sbf = _even_sweep(outs, 0, conv_w, a_log, dt_bias, lb, saf, sbf)
    o, sab, sbb = _even_sweep(outs, 1, conv_w, a_log, dt_bias, lb, sab, sbb, o_f, a_norm_w, b_norm_w)
    return o, ((saf, sbf), (sab, sbb))


def _even_w_in(w):
    c = _cuts(EVEN_SIZES)
    ab = jnp.pad(w[:, c[1]:c[3]], ((0, 0), (0, AB_PAD - (c[3] - c[1]))))
    return jnp.concatenate([w[:, :c[1]], ab, w[:, c[3]:]], axis=1).astype(BF16)


ROPE_NF = C_HD // 4
NEG_BIG = -1e30


def _rope(t, cos, sin_signed):
    w = t.shape[-1]
    lane = lax.broadcasted_iota(jnp.int32, t.shape, 1)
    partner = jnp.where(lane % (2 * ROPE_NF) < ROPE_NF,
                        pltpu.roll(t, w - ROPE_NF, axis=1), pltpu.roll(t, ROPE_NF, axis=1))
    return t * cos + partner * sin_signed


def _dup_kv_head(t, kh):
    lane = lax.broadcasted_iota(jnp.int32, t.shape, 1)
    rolled = pltpu.roll(t, C_HD, axis=1)
    own_half = (lane < C_HD) if kh == 0 else (lane >= C_HD)
    return jnp.where(own_half, t, rolled)


def _attn_kernel(*refs, banded, n_blocks):
    if banded:
        (q_ref, kp_ref, k_ref, kn_ref, vp_ref, v_ref, vn_ref, cq_ref, sq_ref, cp_ref, sp_ref, cn_ref, sn_ref,
         kc_ref, vc_ref, sink_ref, o_ref) = refs
    else:
        q_ref, kc_ref, vc_ref, sink_ref, o_ref = refs
    nb = pl.program_id(1)
    blk = q_ref.shape[1]
    q = q_ref[0] * (C_HD ** -0.5)
    kc = kc_ref[0]
    vc = vc_ref[0]
    if banded:
        cq, sq = cq_ref[...], sq_ref[...]
        q = _rope(q, jnp.concatenate([cq] * (C_HEADS // C_KV_HEADS), axis=1),
                  jnp.concatenate([sq] * (C_HEADS // C_KV_HEADS), axis=1))
        kb = [_rope(kp_ref[0], cp_ref[...], sp_ref[...]), _rope(k_ref[0], cq, sq), _rope(kn_ref[0], cn_ref[...], sn_ref[...])]
        keys = jnp.concatenate(kb + [kc], axis=0)
        vals = jnp.concatenate([vp_ref[0], v_ref[0], vn_ref[0], vc], axis=0)
        qpos = lax.broadcasted_iota(jnp.int32, (blk, keys.shape[0]), 0)
        kcol = lax.broadcasted_iota(jnp.int32, (blk, keys.shape[0]), 1)
        rel = kcol - blk
        in_band = (jnp.abs(rel - qpos) <= C_WIN) & (rel + nb * blk >= 0) & (rel + nb * blk < n_blocks * blk)
        mask = in_band | (kcol >= 3 * blk)
    else:
        keys, vals, mask = kc, vc, None
    keys = keys.astype(BF16)
    vals = vals.astype(BF16)
    lane_q = lax.broadcasted_iota(jnp.int32, (blk, 2 * C_HD), 1)
    grp = C_HEADS // C_KV_HEADS
    heads = range(C_HEADS)
    k_dup = [_dup_kv_head(keys, kh) for kh in range(C_KV_HEADS)]
    v_dup = [_dup_kv_head(vals, kh) for kh in range(C_KV_HEADS)]
    qh = [jnp.where((lane_q < C_HD) if h % 2 == 0 else (lane_q >= C_HD),
                    q[:, (h // 2) * 2 * C_HD:(h // 2 + 1) * 2 * C_HD], 0.0).astype(BF16) for h in heads]
    s = [_dot_nt(qh[h], k_dup[h // grp]) for h in heads]
    if mask is not None:
        s = [jnp.where(mask, x, NEG_BIG) for x in s]
    m = [jnp.maximum(jnp.max(s[h], axis=-1, keepdims=True), sink_ref[h]) for h in heads]
    e = [jnp.exp(s[h] - m[h]) for h in heads]
    denom = [jnp.sum(e[h], axis=-1, keepdims=True) + jnp.exp(sink_ref[h] - m[h]) for h in heads]
    p = [(e[h] / denom[h]).astype(BF16) for h in heads]
    out = [_dot(p[h], v_dup[h // grp]) for h in heads]
    for pair in range(C_HEADS // 2):
        o_ref[0, :, pair * 2 * C_HD:(pair + 1) * 2 * C_HD] = jnp.where(lane_q < C_HD, out[2 * pair], out[2 * pair + 1])


def _rope_tables(n):
    tok = lax.iota(jnp.int32, n)
    row = (tok // GRID_W).astype(F32)
    col = (tok % GRID_W).astype(F32)
    inv = ROPE_THETA ** (-lax.iota(F32, ROPE_NF) / ROPE_NF)
    ang_r, ang_c = row[:, None] * inv, col[:, None] * inv
    cos = jnp.concatenate([jnp.cos(ang_r)] * 2 + [jnp.cos(ang_c)] * 2, axis=-1)
    sin = jnp.concatenate([-jnp.sin(ang_r), jnp.sin(ang_r), -jnp.sin(ang_c), jnp.sin(ang_c)], axis=-1)
    return jnp.concatenate([cos, cos], axis=-1), jnp.concatenate([sin, sin], axis=-1)


def window_attention_pallas(q, k, v, kc, vc, sink, cos, sin):
    b_, n, hq = q.shape
    hk = k.shape[-1]
    lc = kc.shape[1]
    blk = C_BLOCK
    nb_ = n // blk
    cur = lambda b, i: (b, i, 0)
    prv = lambda b, i: (b, jnp.maximum(i - 1, 0), 0)
    nxt = lambda b, i: (b, jnp.minimum(i + 1, nb_ - 1), 0)
    tcur = lambda b, i: (i, 0)
    tprv = lambda b, i: (jnp.maximum(i - 1, 0), 0)
    tnxt = lambda b, i: (jnp.minimum(i + 1, nb_ - 1), 0)
    kspec = lambda m: pl.BlockSpec((1, blk, hk), m)
    tspec = lambda m: pl.BlockSpec((blk, hk), m)
    cspec = pl.BlockSpec((1, lc, hk), lambda b, i: (b, 0, 0))
    return pl.pallas_call(
        functools.partial(_attn_kernel, banded=True, n_blocks=nb_),
        grid=(b_, nb_),
        in_specs=[pl.BlockSpec((1, blk, hq), cur), kspec(prv), kspec(cur), kspec(nxt),
                  kspec(prv), kspec(cur), kspec(nxt),
                  tspec(tcur), tspec(tcur), tspec(tprv), tspec(tprv), tspec(tnxt), tspec(tnxt),
                  cspec, cspec, pl.BlockSpec(memory_space=pltpu.SMEM)],
        out_specs=pl.BlockSpec((1, blk, hq), cur),
        out_shape=jax.ShapeDtypeStruct((b_, n, hq), F32),
        compiler_params=_params(2),
        name="window_attention",
    )(q, k, k, k, v, v, v, cos, sin, cos, sin, cos, sin, kc, vc, sink)


def context_attention_pallas(q, kc, vc, sink):
    b_, lc, hq = q.shape
    hk = kc.shape[-1]
    blk = min(lc, C_BLOCK)
    cspec = pl.BlockSpec((1, lc, hk), lambda b, i: (b, 0, 0))
    return pl.pallas_call(
        functools.partial(_attn_kernel, banded=False, n_blocks=lc // blk),
        grid=(b_, lc // blk),
        in_specs=[pl.BlockSpec((1, blk, hq), lambda b, i: (b, i, 0)), cspec, cspec,
                  pl.BlockSpec(memory_space=pltpu.SMEM)],
        out_specs=pl.BlockSpec((1, blk, hq), lambda b, i: (b, i, 0)),
        out_shape=jax.ShapeDtypeStruct((b_, lc, hq), F32),
        compiler_params=_params(2),
        name="context_attention",
    )(q, kc, vc, sink)


def _scan8(a, u, reverse):
    row = lax.broadcasted_iota(jnp.int32, a.shape, 0)
    for s in (1, 2, 4):
        shift = (SUBLANES - s) if reverse else s
        a_sh = pltpu.roll(a, shift, axis=0)
        u_sh = pltpu.roll(u, shift, axis=0)
        valid = (row < SUBLANES - s) if reverse else (row >= s)
        u = jnp.where(valid, a * u_sh + u, u)
        a = jnp.where(valid, a * a_sh, a)
    return a, u


def _gelu_tanh(x):
    return 0.5 * x * (1.0 + jnp.tanh(0.7978845608028654 * (x + 0.044715 * (x * x * x))))


def _rglru_kernel(*refs, tile, n_tiles, reverse, combine):
    if combine:
        (x_ref, xp_ref, xn_ref, cw_ref, cb_ref, wbd_ref, bias_ref, lam_ref, h0_ref, hf_ref, gd_ref,
         o_ref, hl_ref, xbuf, a_s, u_s, carry) = refs
    else:
        (x_ref, xp_ref, xn_ref, cw_ref, cb_ref, wbd_ref, bias_ref, lam_ref, h0_ref,
         o_ref, hl_ref, xbuf, a_s, u_s, carry) = refs
    i = pl.program_id(1)
    t = (n_tiles - 1 - i) if reverse else i
    w = x_ref.shape[-1]

    @pl.when(i == 0)
    def _():
        carry[...] = jnp.broadcast_to(h0_ref[0], (SUBLANES, w))

    xbuf[pl.ds(0, HALO), :] = jnp.where(t > 0, xp_ref[0], 0.0)
    xbuf[pl.ds(HALO, tile), :] = x_ref[0]
    xbuf[pl.ds(HALO + tile, HALO), :] = jnp.where(t < n_tiles - 1, xn_ref[0], 0.0)
    xc = cb_ref[...] + sum(cw_ref[pl.ds(k, 1), :] * xbuf[pl.ds(HALO - 1 + k, tile), :] for k in range(D_CONV))

    z = _dot(xc.astype(BF16), wbd_ref[...]) + bias_ref[...]
    r = jax.nn.sigmoid(z[:, :w])
    gi = jax.nn.sigmoid(z[:, w:])
    lam = lam_ref[...]
    log_a = (-D_C * jnp.log(1.0 + jnp.exp(-lam))) * r
    a = jnp.exp(log_a)
    a_s[...] = a
    u_s[...] = jnp.sqrt(1.0 - a * a) * gi * xc

    n_groups = tile // SUBLANES

    def step(g, c):
        gg = (n_groups - 1 - g) if reverse else g
        r0 = pl.multiple_of(gg * SUBLANES, SUBLANES)
        ac, uc = _scan8(a_s[pl.ds(r0, SUBLANES), :], u_s[pl.ds(r0, SUBLANES), :], reverse)
        h = ac * c + uc
        o_ref[0, pl.ds(r0, SUBLANES), :] = h
        last = h[0:1, :] if reverse else h[SUBLANES - 1:SUBLANES, :]
        return jnp.broadcast_to(last, (SUBLANES, w))

    c_fin = lax.fori_loop(0, n_groups, step, carry[...], unroll=4 if n_groups % 4 == 0 else 1)
    carry[...] = c_fin

    if combine:
        o_ref[0] = (hf_ref[0] + o_ref[0]) * _gelu_tanh(gd_ref[0])

    @pl.when(i == n_tiles - 1)
    def _():
        hl_ref[0] = c_fin[0:1, :]


def _rglru_sweep(xd, conv_w, conv_b, wbd, bias, lam, h0, hf=None, gd=None, *, reverse):
    b_, n, w = xd.shape
    combine = hf is not None
    tile = min(n, 512)
    n_tiles = n // tile
    blocks_per_tile = tile // HALO
    n_blocks = n // HALO

    def tmap(b, i):
        return (b, (n_tiles - 1 - i) if reverse else i, 0)

    def pmap(b, i):
        t = (n_tiles - 1 - i) if reverse else i
        return (b, jnp.maximum(t * blocks_per_tile - 1, 0), 0)

    def nmap(b, i):
        t = (n_tiles - 1 - i) if reverse else i
        return (b, jnp.minimum((t + 1) * blocks_per_tile, n_blocks - 1), 0)

    const2 = lambda b, i: (0, 0)
    in_specs = [
        pl.BlockSpec((1, tile, w), tmap),
        pl.BlockSpec((1, HALO, w), pmap),
        pl.BlockSpec((1, HALO, w), nmap),
        pl.BlockSpec((D_CONV, w), const2),
        pl.BlockSpec((1, w), const2),
        pl.BlockSpec((w, 2 * w), const2),
        pl.BlockSpec((1, 2 * w), const2),
        pl.BlockSpec((1, w), const2),
        pl.BlockSpec((1, 1, w), lambda b, i: (b, 0, 0)),
    ]
    args = [xd, xd, xd, conv_w, conv_b, wbd, bias, lam, h0]
    if combine:
        in_specs += [pl.BlockSpec((1, tile, w), tmap), pl.BlockSpec((1, tile, w), tmap)]
        args += [hf, gd]
    return pl.pallas_call(
        functools.partial(_rglru_kernel, tile=tile, n_tiles=n_tiles, reverse=reverse, combine=combine),
        grid=(b_, n_tiles),
        in_specs=in_specs,
        out_specs=[pl.BlockSpec((1, tile, w), tmap), pl.BlockSpec((1, 1, w), lambda b, i: (b, 0, 0))],
        out_shape=[jax.ShapeDtypeStruct((b_, n, w), F32), jax.ShapeDtypeStruct((b_, 1, w), F32)],
        scratch_shapes=[pltpu.VMEM((tile + 2 * HALO, w), F32), pltpu.VMEM((tile, w), F32),
                        pltpu.VMEM((tile, w), F32), pltpu.VMEM((SUBLANES, w), F32)],
        compiler_params=_params(2),
        name="rglru_bwd_combine" if combine else "rglru_fwd",
    )(*args)


def _block_diag(wz):
    eye = jnp.eye(D_BLOCKS, dtype=wz.dtype)
    return jnp.einsum('hij,hg->higj', wz, eye).reshape(D_WIDTH, D_WIDTH)


def rglru_pallas(xd, gd, conv_w, conv_b, w_r, b_r, w_i, b_i, lam, s0):
    cb = conv_b.reshape(1, D_WIDTH)
    outs = []
    for z in range(2):
        wbd = jnp.concatenate([_block_diag(w_r[z]), _block_diag(w_i[z])], axis=1).astype(BF16)
        bias = jnp.concatenate([b_r[z], b_i[z]]).reshape(1, 2 * D_WIDTH)
        outs.append((wbd, bias, lam[z].reshape(1, D_WIDTH)))
    hf, sf = _rglru_sweep(xd, conv_w, cb, *outs[0], s0[0][:, None, :], reverse=False)
    y, sb = _rglru_sweep(xd, conv_w, cb, *outs[1], s0[1][:, None, :], hf, gd, reverse=True)
    return y, (sf[:, 0], sb[:, 0])


AFF_BITS = 31


def _lane_cumsum_exclusive(x01):
    e_, n = x01.shape
    li = lax.broadcasted_iota(jnp.int32, (LANES, LANES), 0)
    lj = lax.broadcasted_iota(jnp.int32, (LANES, LANES), 1)
    strict_upper = (li < lj).astype(BF16)
    ones = jnp.ones((LANES, LANES), BF16)
    base = jnp.zeros((e_, LANES), F32)
    outs = []
    for g in range(n // LANES):
        xg = x01[:, g * LANES:(g + 1) * LANES].astype(BF16)
        outs.append(base + _dot(xg, strict_upper))
        base = base + _dot(xg, ones)
    return jnp.concatenate(outs, axis=1)


def _route_kernel(aff_ref, idx_ref, gate_ref, sp_s, af_s, ps_s, pe_s, *, cap):
    aff = aff_ref[0]
    e_, n = aff.shape
    n_groups = n // LANES
    rb_rows = min(cap, LANES)
    bits = pltpu.bitcast(aff, jnp.int32)
    thr = jnp.zeros((e_, 1), jnp.int32)
    for bit in range(AFF_BITS - 1, -1, -1):
        cand = thr | (1 << bit)
        cnt = jnp.sum((bits >= cand).astype(F32), axis=1, keepdims=True)
        thr = jnp.where(cnt >= cap, cand, thr)
    gt = bits > thr
    eq = bits == thr
    need = cap - jnp.sum(gt.astype(F32), axis=1, keepdims=True)
    sel = gt | (eq & (_lane_cumsum_exclusive(eq.astype(F32)) < need))
    pos = _lane_cumsum_exclusive(sel.astype(F32))
    selpos = jnp.where(sel, pos, -1.0)
    tok_g = lax.broadcasted_iota(jnp.int32, (n, n_groups), 0) // LANES
    grp = lax.broadcasted_iota(jnp.int32, (n, n_groups), 1)
    cnt_g = _dot(sel.astype(BF16), (tok_g == grp).astype(BF16))
    gi = lax.broadcasted_iota(jnp.int32, (n_groups, n_groups), 0)
    gj = lax.broadcasted_iota(jnp.int32, (n_groups, n_groups), 1)
    start_g = _dot(cnt_g, (gi < gj).astype(F32), HI)
    ps_s[...] = start_g
    pe_s[...] = start_g + cnt_g
    for g in range(n_groups):
        sp_s[g] = selpos[:, g * LANES:(g + 1) * LANES]
        af_s[g] = aff[:, g * LANES:(g + 1) * LANES]
    slot0 = lax.broadcasted_iota(jnp.int32, (rb_rows, LANES), 0).astype(F32)
    lane = lax.broadcasted_iota(jnp.int32, (1, LANES), 1).astype(F32)

    def per_block(k, carry):
        e = k // (cap // rb_rows)
        rb = k % (cap // rb_rows)
        first = lax.convert_element_type(rb * rb_rows, F32)
        slot = slot0 + first

        def per_group(g, acc):
            acc_i, acc_g = acc
            hit = sp_s[g, pl.ds(e, 1), :] == slot
            tok = lane + lax.convert_element_type(g * LANES, F32)
            return (acc_i + jnp.where(hit, tok, 0.0), acc_g + jnp.where(hit, af_s[g, pl.ds(e, 1), :], 0.0))

        g_lo = jnp.sum((pe_s[pl.ds(e, 1), :] <= first).astype(F32), axis=1, keepdims=True)[0, 0].astype(jnp.int32)
        g_hi = jnp.sum((ps_s[pl.ds(e, 1), :] < first + rb_rows).astype(F32), axis=1, keepdims=True)[0, 0].astype(jnp.int32)
        zero = jnp.zeros((rb_rows, LANES), F32)
        acc_i, acc_g = lax.fori_loop(g_lo, g_hi, per_group, (zero, zero))
        rows = pl.ds(pl.multiple_of(rb * rb_rows, rb_rows), rb_rows)
        idx_ref[0, e, rows, :] = jnp.sum(acc_i, axis=1, keepdims=True).astype(jnp.int32)
        gate_ref[0, e, rows, :] = jnp.sum(acc_g, axis=1, keepdims=True)
        return carry

    lax.fori_loop(0, e_ * (cap // rb_rows), per_block, 0)


def route(aff):
    b_, e_, n = aff.shape
    cap = max(1, EC_FACTOR * n // N_EXPERTS)
    grp_scr = lambda: pltpu.VMEM((n // LANES, e_, LANES), F32)
    cnt_scr = lambda: pltpu.VMEM((e_, n // LANES), F32)
    idx, gate = pl.pallas_call(
        functools.partial(_route_kernel, cap=cap),
        grid=(b_,),
        in_specs=[pl.BlockSpec((1, e_, n), lambda b: (b, 0, 0))],
        out_specs=[pl.BlockSpec((1, e_, cap, 1), lambda b: (b, 0, 0, 0)),
                   pl.BlockSpec((1, e_, cap, 1), lambda b: (b, 0, 0, 0))],
        out_shape=[jax.ShapeDtypeStruct((b_, e_, cap, 1), jnp.int32), jax.ShapeDtypeStruct((b_, e_, cap, 1), F32)],
        scratch_shapes=[grp_scr(), grp_scr(), cnt_scr(), cnt_scr()],
        compiler_params=_params(1),
        name="route",
    )(aff)
    return idx.reshape(b_, e_, 1, cap), gate


ROW_UNROLL = 8


def _gather_kernel(idx_ref, h_ref, o_ref, *, cap):
    def body(r, carry):
        o_ref[0, 0, pl.ds(r, 1), :] = h_ref[0, pl.ds(idx_ref[0, 0, 0, r], 1), :]
        return carry
    lax.fori_loop(0, cap, body, 0, unroll=ROW_UNROLL)


def gather_rows(h, idx):
    b_, n, d = h.shape
    _, e_, _, cap = idx.shape
    return pl.pallas_call(
        functools.partial(_gather_kernel, cap=cap),
        grid=(b_, e_),
        in_specs=[pl.BlockSpec((1, 1, 1, cap), lambda b, e: (b, e, 0, 0), memory_space=pltpu.SMEM),
                  pl.BlockSpec((1, n, d), lambda b, e: (b, 0, 0), pipeline_mode=pl.Buffered(1))],
        out_specs=pl.BlockSpec((1, 1, cap, d), lambda b, e: (b, e, 0, 0)),
        out_shape=jax.ShapeDtypeStruct((b_, e_, cap, d), F32),
        compiler_params=_params(2),
        name="gather_rows",
    )(idx, h)


def _expert_ffn_kernel(x_ref, g_ref, g2_ref, w1_ref, w3_ref, w2_ref, o_ref, w1b, w3b, w2b):
    @pl.when((pl.program_id(1) == 0) & (pl.program_id(2) == 0))
    def _():
        w1b[...] = w1_ref[0].astype(BF16)
        w3b[...] = w3_ref[0].astype(BF16)
        w2b[...] = w2_ref[0].astype(BF16)

    x = x_ref[0, 0].astype(BF16)
    hid = (_silu(_dot(x, w1b[...])) * _dot(x, w3b[...])).astype(BF16)
    o_ref[0, 0] = _dot(hid, w2b[...]) * g_ref[0, 0] * g2_ref[0]


def expert_ffn(xs, gate, g2, w1, w3, w2):
    b_, e_, c_, d = xs.shape
    f = w1.shape[-1]
    tm = min(c_, 512)
    return pl.pallas_call(
        _expert_ffn_kernel,
        grid=(e_, b_, c_ // tm),
        scratch_shapes=[pltpu.VMEM((d, f), BF16), pltpu.VMEM((d, f), BF16), pltpu.VMEM((f, d), BF16)],
        in_specs=[
            pl.BlockSpec((1, 1, tm, d), lambda e, b, m: (b, e, m, 0)),
            pl.BlockSpec((1, 1, tm, 1), lambda e, b, m: (b, e, m, 0)),
            pl.BlockSpec((1, 1, d), lambda e, b, m: (b, 0, 0)),
            pl.BlockSpec((1, d, f), lambda e, b, m: (e, 0, 0)),
            pl.BlockSpec((1, d, f), lambda e, b, m: (e, 0, 0)),
            pl.BlockSpec((1, f, d), lambda e, b, m: (e, 0, 0)),
        ],
        out_specs=pl.BlockSpec((1, 1, tm, d), lambda e, b, m: (b, e, m, 0)),
        out_shape=jax.ShapeDtypeStruct((b_, e_, c_, d), F32),
        compiler_params=_params(3),
        name="expert_ffn",
    )(xs, gate, g2[:, None, :], w1, w3, w2)


def _combine_kernel(idx_ref, y_ref, x1_hbm, o_hbm, acc, sem, *, cap, n_experts):
    b = pl.program_id(0)
    e = pl.program_id(1)

    @pl.when(e == 0)
    def _():
        cp = pltpu.make_async_copy(x1_hbm.at[b], acc, sem.at[0])
        cp.start()
        cp.wait()

    def body(r, carry):
        row = pl.ds(idx_ref[0, 0, 0, r], 1)
        acc[row, :] = acc[row, :] + y_ref[0, 0, pl.ds(r, 1), :]
        return carry
    lax.fori_loop(0, cap, body, 0, unroll=ROW_UNROLL)

    @pl.when(e == n_experts - 1)
    def _():
        cp = pltpu.make_async_copy(acc, o_hbm.at[b], sem.at[1])
        cp.start()
        cp.wait()


def combine_rows(x1, y, idx):
    b_, n, d = x1.shape
    _, e_, _, cap = idx.shape
    return pl.pallas_call(
        functools.partial(_combine_kernel, cap=cap, n_experts=e_),
        grid=(b_, e_),
        in_specs=[pl.BlockSpec((1, 1, 1, cap), lambda b, e: (b, e, 0, 0), memory_space=pltpu.SMEM),
                  pl.BlockSpec((1, 1, cap, d), lambda b, e: (b, e, 0, 0)),
                  pl.BlockSpec(memory_space=pl.ANY)],
        out_specs=pl.BlockSpec(memory_space=pl.ANY),
        out_shape=jax.ShapeDtypeStruct((b_, n, d), F32),
        scratch_shapes=[pltpu.VMEM((n, d), F32), pltpu.SemaphoreType.DMA((2,))],
        compiler_params=_params(2),
        name="combine_rows",
    )(idx, y, x1)


def moe_residual(x1, h2, aff, g2, w1, w3, w2):
    idx, gate = route(aff)
    y = expert_ffn(gather_rows(h2, idx), gate, g2, w1, w3, w2)
    return combine_rows(x1, y, idx)


def kernel(x, c, ctx, c_ctx, w_mod, b_mod, norm1_w, norm2_w, final_norm_w,
           ev_w_in, ev_w_out, a_conv_w, a_log, a_dt_bias, a_norm_w, b_lb_logits, b_norm_w,
           od_w_in, od_w_out, c_sink, d_conv_w, d_conv_b, d_w_r, d_b_r, d_w_i, d_b_i, d_lambda,
           moe_router, moe_w1, moe_w3, moe_w2):
    b_, n, d = x.shape
    cos, sin = _rope_tables(n)
    lb_all = jnp.cumsum(jax.nn.softmax(b_lb_logits.astype(F32), axis=0), axis=0)
    lb_all = lb_all - lb_all[0:1]
    c_rows = jnp.concatenate([c, c_ctx[None, :], jnp.zeros((-(b_ + 1) % SUBLANES, d), F32)], axis=0)
    mod_all = modulation(c_rows, w_mod, b_mod)
    for l in range(DEPTH):
        last = l == DEPTH - 1
        j = l // 2
        w1b, w3b, w2b = moe_w1[l], moe_w3[l], moe_w2[l]
        router_t = moe_router[l].T.astype(BF16)
        mod = mod_all[l, :b_]
        mod_c = jnp.broadcast_to(mod_all[l, b_], (b_, 6 * d))
        sh1, sc1, g1, sh2, sc2, g2 = jnp.split(mod, 6, axis=-1)
        csh1, csc1, cg1, csh2, csc2, cg2 = jnp.split(mod_c, 6, axis=-1)
        if l % 2 == 0:
            w_in, w_out = _even_w_in(ev_w_in[j]), ev_w_out[j].astype(BF16)
            zero = jnp.zeros((b_, A_HEADS, A_DK, A_DV), F32)
            pars = (a_conv_w[j], a_log[j], a_dt_bias[j], a_norm_w[j], lb_all[j].reshape(2, HB), b_norm_w[j])
            o_c, st = even_mixer_pallas(in_proj(ctx, norm1_w[l], csh1, csc1, w_in, EVEN_GROUPS), *pars,
                                        ((zero, zero), (zero, zero)))
            o_l, _ = even_mixer_pallas(in_proj(x, norm1_w[l], sh1, sc1, w_in, EVEN_GROUPS), *pars, st)
            parts_l, parts_c = [o_l], [o_c]
        else:
            w_in, w_out = od_w_in[j].astype(BF16), od_w_out[j].astype(BF16)
            qc, kc, vc, xdc, gdc = in_proj(ctx, norm1_w[l], csh1, csc1, w_in, ODD_SIZES)
            ql, kl, vl, xdl, gdl = in_proj(x, norm1_w[l], sh1, sc1, w_in, ODD_SIZES)
            att_l = window_attention_pallas(ql, kl, vl, kc, vc, c_sink[j], cos, sin)
            rg_pars = (d_conv_w[j], d_conv_b[j], d_w_r[j], d_b_r[j], d_w_i[j], d_b_i[j], d_lambda[j])
            zero = jnp.zeros((b_, D_WIDTH), F32)
            rg_c, st = rglru_pallas(xdc, gdc, *rg_pars, (zero, zero))
            rg_l, _ = rglru_pallas(xdl, gdl, *rg_pars, st)
            parts_l = [att_l, rg_l]
            if not last:
                parts_c = [context_attention_pallas(qc, kc, vc, c_sink[j]), rg_c]
        x1, h2, aff = post_mixer(parts_l, w_out, x, g1, norm2_w[l], sh2, sc2, router_t)
        x = moe_residual(x1, h2, aff, g2, w1b, w3b, w2b)
        if not last:
            c1, hc2, affc = post_mixer(parts_c, w_out, ctx, cg1, norm2_w[l], csh2, csc2, router_t)
            ctx = moe_residual(c1, hc2, affc, cg2, w1b, w3b, w2b)
    return final_norm(x, final_norm_w)
```

```python
import functools

import jax
import jax.numpy as jnp
from jax import lax
from jax.experimental import pallas as pl
from jax.experimental.pallas import tpu as pltpu

D_MODEL = 1024
DEPTH = 4
GRID_W = 64
EPS = 1e-6
F32 = jnp.float32
BF16 = jnp.bfloat16
HI = lax.Precision.HIGHEST

A_HEADS = 4
A_DK = 128
A_DV = 128
A_CONV = 4
A_CHUNK = 64
A_QKV = 2 * A_HEADS * A_DK + A_HEADS * A_DV
B_HEADS = 4
B_DK = 128
B_DV = 128
B_CHUNK = 64
C_HEADS = 8
C_KV_HEADS = 2
C_HD = 64
C_WIN = 128
C_BLOCK = 128
ROPE_THETA = 10000.0
D_WIDTH = 512
D_BLOCKS = 8
D_BW = D_WIDTH // D_BLOCKS
D_CONV = 4
D_C = 8.0
N_EXPERTS = 16
EXPERT_FF = 1024
EC_FACTOR = 2

EVEN_SIZES = (A_QKV, A_HEADS * A_DV, 2 * A_HEADS, 2 * A_HEADS,
              B_HEADS * B_DK, B_HEADS * B_DV, 2 * B_HEADS * B_DK, B_HEADS * B_DV)
ODD_SIZES = (C_HEADS * C_HD, C_KV_HEADS * C_HD, C_KV_HEADS * C_HD, D_WIDTH, D_WIDTH)

LANES = 128
SUBLANES = 8
VMEM_LIMIT_BYTES = 56 * 1024 * 1024

ROW_TILE = 512
HALO = SUBLANES


def _cuts(sizes):
    out, acc = [], 0
    for s in sizes[:-1]:
        acc += s
        out.append(acc)
    return out


def _params(n_axes):
    return pltpu.CompilerParams(dimension_semantics=("arbitrary",) * n_axes, vmem_limit_bytes=VMEM_LIMIT_BYTES)


def _dot(a, b, precision=None):
    return jnp.dot(a, b, preferred_element_type=F32, precision=precision)


def _dot_nt(a, b, precision=None):
    return lax.dot_general(a, b, (((1,), (1,)), ((), ())), preferred_element_type=F32, precision=precision)


def _dot_tn(a, b):
    return lax.dot_general(a, b, (((0,), (0,)), ((), ())), preferred_element_type=F32)


def _silu(x):
    return x * jax.nn.sigmoid(x)


def _rms(x):
    return x * lax.rsqrt(jnp.mean(x * x, axis=-1, keepdims=True) + EPS)


U32 = jnp.uint32
HIGH16 = 0xFFFF0000


def _pack_halves(h):
    half = h.shape[-1] // 2
    lo = pltpu.bitcast(h[:, :half].astype(BF16).astype(F32), U32)
    hi = pltpu.bitcast(h[:, half:].astype(BF16).astype(F32), U32)
    return (hi & jnp.uint32(HIGH16)) | lax.shift_right_logical(lo, jnp.uint32(16))


def _unpack_halves(p):
    lo = pltpu.bitcast(lax.shift_left(p, jnp.uint32(16)), F32).astype(BF16)
    hi = pltpu.bitcast(p & jnp.uint32(HIGH16), F32).astype(BF16)
    return jnp.concatenate([lo, hi], axis=1)


MOD_COLS = 1536


def _mod_kernel(c_ref, w_ref, b_ref, o_ref):
    o_ref[0] = _dot(_silu(c_ref[...]), w_ref[0]) + b_ref[0]


def modulation(c_rows, w_mod, b_mod):
    r, d = c_rows.shape
    depth, _, wide = w_mod.shape
    return pl.pallas_call(
        _mod_kernel,
        grid=(depth, wide // MOD_COLS),
        in_specs=[pl.BlockSpec((r, d), lambda l, j: (0, 0)),
                  pl.BlockSpec((1, d, MOD_COLS), lambda l, j: (l, 0, j)),
                  pl.BlockSpec((1, 1, MOD_COLS), lambda l, j: (l, 0, j))],
        out_specs=pl.BlockSpec((1, r, MOD_COLS), lambda l, j: (l, 0, j)),
        out_shape=jax.ShapeDtypeStruct((depth, r, wide), F32),
        compiler_params=_params(2),
        name="modulation",
    )(c_rows, w_mod, b_mod[:, None, :])


def _norm_mod(x, nw, shift, scale):
    return _rms(x) * nw * (1.0 + scale) + shift


def _in_proj_kernel(x_ref, nw_ref, sh_ref, sc_ref, w_ref, *o_refs, splits):
    h = _norm_mod(x_ref[0], nw_ref[...], sh_ref[0], sc_ref[0]).astype(BF16)
    off = 0
    for o_ref, s in zip(o_refs, splits):
        o_ref[0] = _dot(h, w_ref[:, off:off + s])
        off += s


def in_proj(x, norm_w, shift, scale, w, splits):
    b_, n, d = x.shape
    tm = min(n, ROW_TILE)
    ntot = sum(splits)
    vec = pl.BlockSpec((1, 1, d), lambda b, i: (b, 0, 0))
    return pl.pallas_call(
        functools.partial(_in_proj_kernel, splits=tuple(splits)),
        grid=(b_, n // tm),
        in_specs=[pl.BlockSpec((1, tm, d), lambda b, i: (b, i, 0)),
                  pl.BlockSpec((1, d), lambda b, i: (0, 0)), vec, vec,
                  pl.BlockSpec((d, ntot), lambda b, i: (0, 0))],
        out_specs=[pl.BlockSpec((1, tm, s), lambda b, i: (b, i, 0)) for s in splits],
        out_shape=[jax.ShapeDtypeStruct((b_, n, s), F32) for s in splits],
        compiler_params=_params(2),
        name="in_proj",
    )(x, norm_w.reshape(1, d), shift[:, None, :], scale[:, None, :], w)


def _post_mixer_kernel(*refs, n_parts):
    parts = refs[:n_parts]
    w_ref, x_ref, g1_ref, nw_ref, sh_ref, sc_ref, rt_ref, x1_ref, h2_ref, aff_ref = refs[n_parts:]
    y = None
    off = 0
    for p_ref in parts:
        k = p_ref.shape[-1]
        t = _dot(p_ref[0].astype(BF16), w_ref[off:off + k, :])
        y = t if y is None else y + t
        off += k
    x1 = x_ref[0] + g1_ref[0] * y
    x1_ref[0] = x1
    h2 = _norm_mod(x1, nw_ref[...], sh_ref[0], sc_ref[0])
    h2_ref[0] = _pack_halves(h2)
    logits = _dot_nt(rt_ref[...], h2.astype(BF16))
    e = jnp.exp(logits - jnp.max(logits, axis=0, keepdims=True))
    aff_ref[0] = e / jnp.sum(e, axis=0, keepdims=True)


def post_mixer(parts, w_out, x, g1, norm_w, shift, scale, router_t):
    b_, n, d = x.shape
    e_ = router_t.shape[0]
    tm = min(n, ROW_TILE)
    vec = pl.BlockSpec((1, 1, d), lambda b, i: (b, 0, 0))
    tok = lambda k: pl.BlockSpec((1, tm, k), lambda b, i: (b, i, 0))
    return pl.pallas_call(
        functools.partial(_post_mixer_kernel, n_parts=len(parts)),
        grid=(b_, n // tm),
        in_specs=[tok(p.shape[-1]) for p in parts] + [
            pl.BlockSpec(w_out.shape, lambda b, i: (0, 0)), tok(d), vec,
            pl.BlockSpec((1, d), lambda b, i: (0, 0)), vec, vec,
            pl.BlockSpec((e_, d), lambda b, i: (0, 0))],
        out_specs=[tok(d), tok(d // 2), pl.BlockSpec((1, e_, tm), lambda b, i: (b, 0, i))],
        out_shape=[jax.ShapeDtypeStruct((b_, n, d), F32), jax.ShapeDtypeStruct((b_, n, d // 2), U32),
                   jax.ShapeDtypeStruct((b_, e_, n), F32)],
        compiler_params=_params(2),
        name="post_mixer",
    )(*parts, w_out, x, g1[:, None, :], norm_w.reshape(1, d), shift[:, None, :], scale[:, None, :], router_t)


def _final_kernel(x_ref, nw_ref, o_ref):
    o_ref[0] = _rms(x_ref[0]) * nw_ref[...]


def final_norm(x, norm_w):
    b_, n, d = x.shape
    tm = min(n, 2 * ROW_TILE)
    tok = pl.BlockSpec((1, tm, d), lambda b, i: (b, i, 0))
    return pl.pallas_call(
        _final_kernel,
        grid=(b_, n // tm),
        in_specs=[tok, pl.BlockSpec((1, d), lambda b, i: (0, 0))],
        out_specs=tok,
        out_shape=jax.ShapeDtypeStruct((b_, n, d), F32),
        compiler_params=_params(2),
        name="final_norm",
    )(x, norm_w.reshape(1, d))


CHUNK = A_CHUNK
SUB = 16
EVEN_TILE = 256
HA = A_HEADS * A_DK
HB = B_HEADS * B_DK
AB_PAD = LANES
EVEN_GROUPS = (A_QKV, A_HEADS * A_DV, AB_PAD, HB, B_HEADS * B_DV, 2 * HB, B_HEADS * B_DV)
NEUMANN_STEPS = 5


def _split2(x):
    hi = x.astype(BF16)
    return hi, (x - hi.astype(F32)).astype(BF16)


def _dot_split(a, b):
    (ah, al), (bh, bl) = a, b
    return _dot(ah, bh) + _dot(ah, bl) + _dot(al, bh)


def _dot_exact_lhs(a_bf16, x):
    x1 = x.astype(BF16)
    r1 = x - x1.astype(F32)
    x2 = r1.astype(BF16)
    x3 = (r1 - x2.astype(F32)).astype(BF16)
    return _dot(a_bf16, x1) + _dot(a_bf16, x2) + _dot(a_bf16, x3)


def _even_kernel(*refs, tile, n_tiles, reverse, combine):
    if combine:
        (qkvn_ref, ab_ref, qb_ref, ib_ref, fb_ref, exp_ref, alog_ref, dtb_ref, lb_ref,
         sa0_ref, sb0_ref, of_ref, ga_ref, gb_ref, anw_ref, bnw_ref,
         o_ref, sa_ref, sb_ref, g_scr, st_a, st_b) = refs
    else:
        (qkv_ref, qkvp_ref, qkvnx_ref, ab_ref, qb_ref, ib_ref, fb_ref, cw_ref, exp_ref, alog_ref, dtb_ref, lb_ref,
         sa0_ref, sb0_ref,
         o_ref, qkvn_ref, sa_ref, sb_ref, xbuf, g_scr, st_a, st_b) = refs
    i = pl.program_id(1)
    t = (n_tiles - 1 - i) if reverse else i

    @pl.when(i == 0)
    def _():
        st_a[...] = sa0_ref[0]
        st_b[...] = sb0_ref[0]

    L = CHUNK
    n_chunks = tile // L
    n_blocks = tile // SUB
    chunk_order = range(n_chunks - 1, -1, -1) if reverse else range(n_chunks)
    block_order = range(n_blocks - 1, -1, -1) if reverse else range(n_blocks)

    ti = lax.broadcasted_iota(jnp.int32, (tile, tile), 0)
    tj = lax.broadcasted_iota(jnp.int32, (tile, tile), 1)
    t_incl = (tj >= ti) if reverse else (tj <= ti)
    tri_chunk = (t_incl & ((ti // L) == (tj // L))).astype(BF16)
    tri_blk = (t_incl & ((ti // SUB) == (tj // SUB))).astype(BF16)
    ones_blk = ((ti // SUB) == (tj // SUB)).astype(BF16)
    ii = lax.broadcasted_iota(jnp.int32, (L, L), 0)
    jj = lax.broadcasted_iota(jnp.int32, (L, L), 1)
    incl = (jj >= ii) if reverse else (jj <= ii)
    strict = (jj > ii) if reverse else (jj < ii)
    eye = (ii == jj).astype(F32)
    last_row = 0 if reverse else L - 1

    if not combine:
        xbuf[pl.ds(0, HALO), :] = jnp.where(t > 0, qkvp_ref[0], 0.0)
        xbuf[pl.ds(HALO, tile), :] = qkv_ref[0]
        xbuf[pl.ds(HALO + tile, HALO), :] = jnp.where(t < n_tiles - 1, qkvnx_ref[0], 0.0)
        conv = _silu(sum(cw_ref[pl.ds(k, 1), :] * xbuf[pl.ds(HALO - 1 + k, tile), :] for k in range(A_CONV)))
    ab_x = _dot(ab_ref[0], exp_ref[...], HI)
    g_all = -jnp.exp(alog_ref[...]) * jax.nn.softplus(ab_x[:, :HA] + dtb_ref[...])
    beta_all = jax.nn.sigmoid(ab_x[:, HA:])
    cum_all = _dot_exact_lhs(tri_chunk, g_all)
    ecum_all = jnp.exp(cum_all)

    chains = [(c, h) for c in chunk_order for h in range(A_HEADS)]
    pre = {}
    neg_m = []
    for c, h in chains:
        rs = slice(c * L, (c + 1) * L)
        hs = slice(h * A_DK, (h + 1) * A_DK)
        ks_, vs_ = slice(HA + h * A_DK, HA + (h + 1) * A_DK), slice(2 * HA + h * A_DV, 2 * HA + (h + 1) * A_DV)
        if combine:
            q, k, v = qkvn_ref[0, rs, hs], qkvn_ref[0, rs, ks_], qkvn_ref[0, rs, vs_]
        else:
            qh, kh, v = conv[rs, hs], conv[rs, ks_], conv[rs, vs_]
            q = qh * (lax.rsqrt(jnp.sum(qh * qh, axis=-1, keepdims=True) + EPS) * (A_DK ** -0.5))
            k = kh * lax.rsqrt(jnp.sum(kh * kh, axis=-1, keepdims=True) + EPS)
            qkvn_ref[0, rs, hs], qkvn_ref[0, rs, ks_], qkvn_ref[0, rs, vs_] = q, k, v
        beta, cum, ecum = beta_all[rs, hs], cum_all[rs, hs], ecum_all[rs, hs]
        r_ll = cum.T[:L, :]
        decay = jnp.where(incl, jnp.exp(jnp.where(incl, cum[:, :L] - r_ll, 0.0)), 0.0)
        kb = k * beta
        k16 = k.astype(BF16)
        neg_m.append(jnp.where(strict, -_dot_nt(kb.astype(BF16), k16) * decay, 0.0))
        qk = _dot_nt(q.astype(BF16), k16) * decay
        total = cum[last_row:last_row + 1, :]
        pre[(c, h)] = dict(rhs=jnp.concatenate([v * beta, kb * ecum], axis=1), qk=qk.astype(BF16),
                           qd=(q * ecum).astype(BF16), kd=(k * jnp.exp(total - cum)).astype(BF16),
                           gl=jnp.exp(total))
    p = [_split2(m) for m in neg_m]
    tinv = [eye + m for m in neg_m]
    for _ in range(NEUMANN_STEPS):
        p = [_split2(_dot_split(x, x)) for x in p]
        tinv = [tv + _dot_split(_split2(tv), x) for tv, x in zip(tinv, p)]
    sol_of = {ch: _dot_split(_split2(tv), _split2(pre[ch]["rhs"])) for tv, ch in zip(tinv, chains)}

    for c in chunk_order:
        rs = slice(c * L, (c + 1) * L)
        s_old = [st_a[h] for h in range(A_HEADS)]
        s16 = [s.astype(BF16) for s in s_old]
        ws = [_dot(sol_of[(c, h)][:, A_DV:].astype(BF16), s16[h]) for h in range(A_HEADS)]
        qs_ = [_dot(pre[(c, h)]["qd"], s16[h]) for h in range(A_HEADS)]
        v_new = [(sol_of[(c, h)][:, :A_DV] - ws[h]).astype(BF16) for h in range(A_HEADS)]
        for h in range(A_HEADS):
            o_ref[0, rs, h * A_DV:(h + 1) * A_DV] = qs_[h] + _dot(pre[(c, h)]["qk"], v_new[h])
            st_a[h] = s_old[h] * pre[(c, h)]["gl"] + _dot_tn(pre[(c, h)]["kd"], v_new[h])

    hq = _silu(qb_ref[0])
    hv = ib_ref[0]
    lb = lb_ref[...]
    fg = lb + (1.0 - lb) * jax.nn.sigmoid(fb_ref[0])
    lf = jnp.log(fg)
    hk = 1.0 - fg
    b_all = _dot_exact_lhs(tri_blk, lf)
    e_all = _dot_exact_lhs(ones_blk, lf)
    qs_all = (hq * jnp.exp(b_all)).astype(BF16)
    ks_all = (hk * jnp.exp(e_all - b_all)).astype(BF16)
    hv16 = hv.astype(BF16)
    row_in_blk = lax.broadcasted_iota(jnp.int32, (tile, HB), 0) % SUB
    od = [jnp.zeros((tile, B_DV), F32) for _ in range(B_HEADS)]
    for s_ in range(SUB):
        valid = (row_in_blk + s_ < SUB) if reverse else (row_in_blk >= s_)
        shift = ((tile - s_) if reverse else s_) % tile
        roll = (lambda x: x) if s_ == 0 else (lambda x: pltpu.roll(x, shift, axis=0))
        prod = jnp.where(valid, hq * roll(hk) * jnp.exp(jnp.where(valid, b_all - roll(b_all), 0.0)), 0.0)
        vv = roll(hv)
        for h in range(B_HEADS):
            hs = slice(h * B_DK, (h + 1) * B_DK)
            od[h] = od[h] + jnp.sum(prod[:, hs], axis=-1, keepdims=True) * vv[:, hs]
    for blk in block_order:
        bs = slice(blk * SUB, (blk + 1) * SUB)
        for h in range(B_HEADS):
            hs = slice(h * B_DK, (h + 1) * B_DK)
            g_scr[blk * B_HEADS + h] = _dot_tn(hv16[bs, hs], ks_all[bs, hs])

    st = [st_b[h] for h in range(B_HEADS)]
    for blk in block_order:
        bs = slice(blk * SUB, (blk + 1) * SUB)
        for h in range(B_HEADS):
            hs = slice(h * B_DK, (h + 1) * B_DK)
            o_ref[0, bs, HA + h * B_DV:HA + (h + 1) * B_DV] = _dot_nt(qs_all[bs, hs], st[h].astype(BF16)) + od[h][bs]
            st[h] = st[h] * jnp.exp(e_all[blk * SUB:blk * SUB + 1, hs]) + g_scr[blk * B_HEADS + h]
    for h in range(B_HEADS):
        st_b[h] = st[h]

    if combine:
        tot = of_ref[0] + o_ref[0]
        for h in range(A_HEADS + B_HEADS):
            hs = slice(h * A_DV, (h + 1) * A_DV)
            nw = anw_ref[...] if h < A_HEADS else bnw_ref[...]
            gate = ga_ref[0, :, hs] if h < A_HEADS else gb_ref[0, :, pl.ds((h - A_HEADS) * B_DV, B_DV)]
            o_ref[0, :, hs] = _rms(tot[:, hs]) * nw * _silu(gate)

    @pl.when(i == n_tiles - 1)
    def _():
        sa_ref[0] = st_a[...]
        sb_ref[0] = st_b[...]


def _even_sweep(outs, z, conv_w, a_log, dt_bias, lb, s_a, s_b, of=None, a_norm_w=None, b_norm_w=None):
    qkv, ga, ab, qb, ib, fb, gb = outs
    b_, n, _ = qkv.shape
    reverse = z == 1
    combine = of is not None
    tile = min(n, EVEN_TILE)
    n_tiles = n // tile
    bpt = tile // HALO
    nblk = n // HALO
    tidx = lambda i: (n_tiles - 1 - i) if reverse else i
    tmap = lambda b, i: (b, tidx(i), 0)
    pmap = lambda b, i: (b, jnp.maximum(tidx(i) * bpt - 1, 0), 0)
    nmap = lambda b, i: (b, jnp.minimum((tidx(i) + 1) * bpt, nblk - 1), 0)
    zmap = lambda b, i: (b, tidx(i), z)
    const2 = lambda b, i: (0, 0)
    smap = lambda b, i: (b, 0, 0, 0)
    src = lax.broadcasted_iota(jnp.int32, (AB_PAD, 2 * HA), 0)
    dst = lax.broadcasted_iota(jnp.int32, (AB_PAD, 2 * HA), 1)
    expand = (src == jnp.where(dst >= HA, 2 * A_HEADS, 0) + z * A_HEADS + (dst % HA) // A_DK).astype(F32)
    rep = lambda p_: jnp.broadcast_to(p_[:, None], (A_HEADS, A_DK)).reshape(1, HA)
    tok = lambda w: pl.BlockSpec((1, tile, w), tmap)
    common_specs = [tok(AB_PAD), tok(HB), tok(HB), pl.BlockSpec((1, tile, HB), zmap)]
    common_args = [ab, qb, ib, fb]
    par_specs = [pl.BlockSpec((AB_PAD, 2 * HA), const2),
                 pl.BlockSpec((1, HA), const2), pl.BlockSpec((1, HA), const2), pl.BlockSpec((1, HB), const2),
                 pl.BlockSpec((1, A_HEADS, A_DK, A_DV), smap), pl.BlockSpec((1, B_HEADS, B_DV, B_DK), smap)]
    par_args = [expand, rep(a_log[z]), rep(dt_bias[z]), lb[z].reshape(1, HB), s_a, s_b]
    state_specs = [pl.BlockSpec((1, A_HEADS, A_DK, A_DV), smap), pl.BlockSpec((1, B_HEADS, B_DV, B_DK), smap)]
    state_shapes = [jax.ShapeDtypeStruct(s_a.shape, F32), jax.ShapeDtypeStruct(s_b.shape, F32)]
    scratch = [pltpu.VMEM((tile // SUB * B_HEADS, B_DV, B_DK), F32),
               pltpu.VMEM((A_HEADS, A_DK, A_DV), F32), pltpu.VMEM((B_HEADS, B_DV, B_DK), F32)]
    if combine:
        in_specs = [tok(A_QKV)] + common_specs + par_specs + [
            tok(HA + HB), tok(HA), tok(HB), pl.BlockSpec((1, A_DV), const2), pl.BlockSpec((1, B_DV), const2)]
        args = [qkv] + common_args + par_args + [of, ga, gb, a_norm_w.reshape(1, A_DV), b_norm_w.reshape(1, B_DV)]
        out_specs = [tok(HA + HB)] + state_specs
        out_shape = [jax.ShapeDtypeStruct((b_, n, HA + HB), F32)] + state_shapes
    else:
        in_specs = [tok(A_QKV), pl.BlockSpec((1, HALO, A_QKV), pmap), pl.BlockSpec((1, HALO, A_QKV), nmap)] + \
            common_specs + [pl.BlockSpec((A_CONV, A_QKV), const2)] + par_specs
        args = [qkv, qkv, qkv] + common_args + [conv_w] + par_args
        out_specs = [tok(HA + HB), tok(A_QKV)] + state_specs
        out_shape = [jax.ShapeDtypeStruct((b_, n, HA + HB), F32), jax.ShapeDtypeStruct((b_, n, A_QKV), F32)] + state_shapes
        scratch = [pltpu.VMEM((tile + 2 * HALO, A_QKV), F32)] + scratch
    return pl.pallas_call(
        functools.partial(_even_kernel, tile=tile, n_tiles=n_tiles, reverse=reverse, combine=combine),
        grid=(b_, n_tiles),
        in_specs=in_specs,
        out_specs=out_specs,
        out_shape=out_shape,
        scratch_shapes=scratch,
        compiler_params=_params(2),
        name="even_bwd_combine" if combine else "even_fwd",
    )(*args)


def even_mixer_pallas(outs, conv_w, a_log, dt_bias, a_norm_w, lb, b_norm_w, states):
    (saf, sbf), (sab, sbb) = states
    o_f, qkv_n, saf, sbf = _even_sweep(outs, 0, conv_w, a_log, dt_bias, lb, saf, sbf)
    outs_b = [qkv_n] + list(outs[1:])
    o, sab, sbb = _even_sweep(outs_b, 1, conv_w, a_log, dt_bias, lb, sab, sbb, o_f, a_norm_w, b_norm_w)
    return o, ((saf, sbf), (sab, sbb))


def _even_w_in(w):
    c = _cuts(EVEN_SIZES)
    ab = jnp.pad(w[:, c[1]:c[3]], ((0, 0), (0, AB_PAD - (c[3] - c[1]))))
    return jnp.concatenate([w[:, :c[1]], ab, w[:, c[3]:]], axis=1).astype(BF16)


ROPE_NF = C_HD // 4
NEG_BIG = -1e30


def _rope(t, cos, sin_signed):
    w = t.shape[-1]
    lane = lax.broadcasted_iota(jnp.int32, t.shape, 1)
    partner = jnp.where(lane % (2 * ROPE_NF) < ROPE_NF,
                        pltpu.roll(t, w - ROPE_NF, axis=1), pltpu.roll(t, ROPE_NF, axis=1))
    return t * cos + partner * sin_signed


def _dup_kv_head(t, kh):
    lane = lax.broadcasted_iota(jnp.int32, t.shape, 1)
    rolled = pltpu.roll(t, C_HD, axis=1)
    own_half = (lane < C_HD) if kh == 0 else (lane >= C_HD)
    return jnp.where(own_half, t, rolled)


def _attn_kernel(*refs, banded, n_blocks):
    if banded:
        (q_ref, kp_ref, k_ref, kn_ref, vp_ref, v_ref, vn_ref, cq_ref, sq_ref, cp_ref, sp_ref, cn_ref, sn_ref,
         kc_ref, vc_ref, sink_ref, o_ref) = refs
    else:
        q_ref, kc_ref, vc_ref, sink_ref, o_ref = refs
    nb = pl.program_id(1)
    blk = q_ref.shape[1]
    q = q_ref[0] * (C_HD ** -0.5)
    kc = kc_ref[0]
    vc = vc_ref[0]
    if banded:
        cq, sq = cq_ref[...], sq_ref[...]
        q = _rope(q, jnp.concatenate([cq] * (C_HEADS // C_KV_HEADS), axis=1),
                  jnp.concatenate([sq] * (C_HEADS // C_KV_HEADS), axis=1))
        kb = [_rope(kp_ref[0], cp_ref[...], sp_ref[...]), _rope(k_ref[0], cq, sq), _rope(kn_ref[0], cn_ref[...], sn_ref[...])]
        keys = jnp.concatenate(kb + [kc], axis=0)
        vals = jnp.concatenate([vp_ref[0], v_ref[0], vn_ref[0], vc], axis=0)
        qpos = lax.broadcasted_iota(jnp.int32, (blk, keys.shape[0]), 0)
        kcol = lax.broadcasted_iota(jnp.int32, (blk, keys.shape[0]), 1)
        rel = kcol - blk
        in_band = (jnp.abs(rel - qpos) <= C_WIN) & (rel + nb * blk >= 0) & (rel + nb * blk < n_blocks * blk)
        mask = in_band | (kcol >= 3 * blk)
    else:
        keys, vals, mask = kc, vc, None
    keys = keys.astype(BF16)
    vals = vals.astype(BF16)
    lane_q = lax.broadcasted_iota(jnp.int32, (blk, 2 * C_HD), 1)
    grp = C_HEADS // C_KV_HEADS
    heads = range(C_HEADS)
    k_dup = [_dup_kv_head(keys, kh) for kh in range(C_KV_HEADS)]
    v_dup = [_dup_kv_head(vals, kh) for kh in range(C_KV_HEADS)]
    qh = [jnp.where((lane_q < C_HD) if h % 2 == 0 else (lane_q >= C_HD),
                    q[:, (h // 2) * 2 * C_HD:(h // 2 + 1) * 2 * C_HD], 0.0).astype(BF16) for h in heads]
    s = [_dot_nt(qh[h], k_dup[h // grp]) for h in heads]
    if mask is not None:
        s = [jnp.where(mask, x, NEG_BIG) for x in s]
    m = [jnp.maximum(jnp.max(s[h], axis=-1, keepdims=True), sink_ref[h]) for h in heads]
    e = [jnp.exp(s[h] - m[h]) for h in heads]
    denom = [jnp.sum(e[h], axis=-1, keepdims=True) + jnp.exp(sink_ref[h] - m[h]) for h in heads]
    p = [(e[h] / denom[h]).astype(BF16) for h in heads]
    out = [_dot(p[h], v_dup[h // grp]) for h in heads]
    for pair in range(C_HEADS // 2):
        o_ref[0, :, pair * 2 * C_HD:(pair + 1) * 2 * C_HD] = jnp.where(lane_q < C_HD, out[2 * pair], out[2 * pair + 1])


def _rope_tables(n):
    tok = lax.iota(jnp.int32, n)
    row = (tok // GRID_W).astype(F32)
    col = (tok % GRID_W).astype(F32)
    inv = ROPE_THETA ** (-lax.iota(F32, ROPE_NF) / ROPE_NF)
    ang_r, ang_c = row[:, None] * inv, col[:, None] * inv
    cos = jnp.concatenate([jnp.cos(ang_r)] * 2 + [jnp.cos(ang_c)] * 2, axis=-1)
    sin = jnp.concatenate([-jnp.sin(ang_r), jnp.sin(ang_r), -jnp.sin(ang_c), jnp.sin(ang_c)], axis=-1)
    return jnp.concatenate([cos, cos], axis=-1), jnp.concatenate([sin, sin], axis=-1)


def window_attention_pallas(q, k, v, kc, vc, sink, cos, sin):
    b_, n, hq = q.shape
    hk = k.shape[-1]
    lc = kc.shape[1]
    blk = C_BLOCK
    nb_ = n // blk
    cur = lambda b, i: (b, i, 0)
    prv = lambda b, i: (b, jnp.maximum(i - 1, 0), 0)
    nxt = lambda b, i: (b, jnp.minimum(i + 1, nb_ - 1), 0)
    tcur = lambda b, i: (i, 0)
    tprv = lambda b, i: (jnp.maximum(i - 1, 0), 0)
    tnxt = lambda b, i: (jnp.minimum(i + 1, nb_ - 1), 0)
    kspec = lambda m: pl.BlockSpec((1, blk, hk), m)
    tspec = lambda m: pl.BlockSpec((blk, hk), m)
    cspec = pl.BlockSpec((1, lc, hk), lambda b, i: (b, 0, 0))
    return pl.pallas_call(
        functools.partial(_attn_kernel, banded=True, n_blocks=nb_),
        grid=(b_, nb_),
        in_specs=[pl.BlockSpec((1, blk, hq), cur), kspec(prv), kspec(cur), kspec(nxt),
                  kspec(prv), kspec(cur), kspec(nxt),
                  tspec(tcur), tspec(tcur), tspec(tprv), tspec(tprv), tspec(tnxt), tspec(tnxt),
                  cspec, cspec, pl.BlockSpec(memory_space=pltpu.SMEM)],
        out_specs=pl.BlockSpec((1, blk, hq), cur),
        out_shape=jax.ShapeDtypeStruct((b_, n, hq), F32),
        compiler_params=_params(2),
        name="window_attention",
    )(q, k, k, k, v, v, v, cos, sin, cos, sin, cos, sin, kc, vc, sink)


def context_attention_pallas(q, kc, vc, sink):
    b_, lc, hq = q.shape
    hk = kc.shape[-1]
    blk = min(lc, C_BLOCK)
    cspec = pl.BlockSpec((1, lc, hk), lambda b, i: (b, 0, 0))
    return pl.pallas_call(
        functools.partial(_attn_kernel, banded=False, n_blocks=lc // blk),
        grid=(b_, lc // blk),
        in_specs=[pl.BlockSpec((1, blk, hq), lambda b, i: (b, i, 0)), cspec, cspec,
                  pl.BlockSpec(memory_space=pltpu.SMEM)],
        out_specs=pl.BlockSpec((1, blk, hq), lambda b, i: (b, i, 0)),
        out_shape=jax.ShapeDtypeStruct((b_, lc, hq), F32),
        compiler_params=_params(2),
        name="context_attention",
    )(q, kc, vc, sink)


def _scan8(a, u, reverse):
    row = lax.broadcasted_iota(jnp.int32, a.shape, 0)
    for s in (1, 2, 4):
        shift = (SUBLANES - s) if reverse else s
        a_sh = pltpu.roll(a, shift, axis=0)
        u_sh = pltpu.roll(u, shift, axis=0)
        valid = (row < SUBLANES - s) if reverse else (row >= s)
        u = jnp.where(valid, a * u_sh + u, u)
        a = jnp.where(valid, a * a_sh, a)
    return a, u


def _gelu_tanh(x):
    return 0.5 * x * (1.0 + jnp.tanh(0.7978845608028654 * (x + 0.044715 * (x * x * x))))


def _rglru_kernel(*refs, tile, n_tiles, reverse, combine):
    if combine:
        (x_ref, xp_ref, xn_ref, cw_ref, cb_ref, wbd_ref, bias_ref, lam_ref, h0_ref, hf_ref, gd_ref,
         o_ref, hl_ref, xbuf, a_s, u_s, carry) = refs
    else:
        (x_ref, xp_ref, xn_ref, cw_ref, cb_ref, wbd_ref, bias_ref, lam_ref, h0_ref,
         o_ref, hl_ref, xbuf, a_s, u_s, carry) = refs
    i = pl.program_id(1)
    t = (n_tiles - 1 - i) if reverse else i
    w = x_ref.shape[-1]

    @pl.when(i == 0)
    def _():
        carry[...] = jnp.broadcast_to(h0_ref[0], (SUBLANES, w))

    xbuf[pl.ds(0, HALO), :] = jnp.where(t > 0, xp_ref[0], 0.0)
    xbuf[pl.ds(HALO, tile), :] = x_ref[0]
    xbuf[pl.ds(HALO + tile, HALO), :] = jnp.where(t < n_tiles - 1, xn_ref[0], 0.0)
    xc = cb_ref[...] + sum(cw_ref[pl.ds(k, 1), :] * xbuf[pl.ds(HALO - 1 + k, tile), :] for k in range(D_CONV))

    z = _dot(xc.astype(BF16), wbd_ref[...]) + bias_ref[...]
    r = jax.nn.sigmoid(z[:, :w])
    gi = jax.nn.sigmoid(z[:, w:])
    lam = lam_ref[...]
    log_a = (-D_C * jnp.log(1.0 + jnp.exp(-lam))) * r
    a = jnp.exp(log_a)
    a_s[...] = a
    u_s[...] = jnp.sqrt(1.0 - a * a) * gi * xc

    n_groups = tile // SUBLANES

    def step(g, c):
        gg = (n_groups - 1 - g) if reverse else g
        r0 = pl.multiple_of(gg * SUBLANES, SUBLANES)
        ac, uc = _scan8(a_s[pl.ds(r0, SUBLANES), :], u_s[pl.ds(r0, SUBLANES), :], reverse)
        h = ac * c + uc
        o_ref[0, pl.ds(r0, SUBLANES), :] = h
        last = h[0:1, :] if reverse else h[SUBLANES - 1:SUBLANES, :]
        return jnp.broadcast_to(last, (SUBLANES, w))

    c_fin = lax.fori_loop(0, n_groups, step, carry[...], unroll=4 if n_groups % 4 == 0 else 1)
    carry[...] = c_fin

    if combine:
        o_ref[0] = (hf_ref[0] + o_ref[0]) * _gelu_tanh(gd_ref[0])

    @pl.when(i == n_tiles - 1)
    def _():
        hl_ref[0] = c_fin[0:1, :]


def _rglru_sweep(xd, conv_w, conv_b, wbd, bias, lam, h0, hf=None, gd=None, *, reverse):
    b_, n, w = xd.shape
    combine = hf is not None
    tile = min(n, 512)
    n_tiles = n // tile
    blocks_per_tile = tile // HALO
    n_blocks = n // HALO

    def tmap(b, i):
        return (b, (n_tiles - 1 - i) if reverse else i, 0)

    def pmap(b, i):
        t = (n_tiles - 1 - i) if reverse else i
        return (b, jnp.maximum(t * blocks_per_tile - 1, 0), 0)

    def nmap(b, i):
        t = (n_tiles - 1 - i) if reverse else i
        return (b, jnp.minimum((t + 1) * blocks_per_tile, n_blocks - 1), 0)

    const2 = lambda b, i: (0, 0)
    in_specs = [
        pl.BlockSpec((1, tile, w), tmap),
        pl.BlockSpec((1, HALO, w), pmap),
        pl.BlockSpec((1, HALO, w), nmap),
        pl.BlockSpec((D_CONV, w), const2),
        pl.BlockSpec((1, w), const2),
        pl.BlockSpec((w, 2 * w), const2),
        pl.BlockSpec((1, 2 * w), const2),
        pl.BlockSpec((1, w), const2),
        pl.BlockSpec((1, 1, w), lambda b, i: (b, 0, 0)),
    ]
    args = [xd, xd, xd, conv_w, conv_b, wbd, bias, lam, h0]
    if combine:
        in_specs += [pl.BlockSpec((1, tile, w), tmap), pl.BlockSpec((1, tile, w), tmap)]
        args += [hf, gd]
    return pl.pallas_call(
        functools.partial(_rglru_kernel, tile=tile, n_tiles=n_tiles, reverse=reverse, combine=combine),
        grid=(b_, n_tiles),
        in_specs=in_specs,
        out_specs=[pl.BlockSpec((1, tile, w), tmap), pl.BlockSpec((1, 1, w), lambda b, i: (b, 0, 0))],
        out_shape=[jax.ShapeDtypeStruct((b_, n, w), F32), jax.ShapeDtypeStruct((b_, 1, w), F32)],
        scratch_shapes=[pltpu.VMEM((tile + 2 * HALO, w), F32), pltpu.VMEM((tile, w), F32),
                        pltpu.VMEM((tile, w), F32), pltpu.VMEM((SUBLANES, w), F32)],
        compiler_params=_params(2),
        name="rglru_bwd_combine" if combine else "rglru_fwd",
    )(*args)


def _block_diag(wz):
    eye = jnp.eye(D_BLOCKS, dtype=wz.dtype)
    return jnp.einsum('hij,hg->higj', wz, eye).reshape(D_WIDTH, D_WIDTH)


def rglru_pallas(xd, gd, conv_w, conv_b, w_r, b_r, w_i, b_i, lam, s0):
    cb = conv_b.reshape(1, D_WIDTH)
    outs = []
    for z in range(2):
        wbd = jnp.concatenate([_block_diag(w_r[z]), _block_diag(w_i[z])], axis=1).astype(BF16)
        bias = jnp.concatenate([b_r[z], b_i[z]]).reshape(1, 2 * D_WIDTH)
        outs.append((wbd, bias, lam[z].reshape(1, D_WIDTH)))
    hf, sf = _rglru_sweep(xd, conv_w, cb, *outs[0], s0[0][:, None, :], reverse=False)
    y, sb = _rglru_sweep(xd, conv_w, cb, *outs[1], s0[1][:, None, :], hf, gd, reverse=True)
    return y, (sf[:, 0], sb[:, 0])


AFF_BITS = 31


def _lane_cumsum_exclusive(x01):
    e_, n = x01.shape
    li = lax.broadcasted_iota(jnp.int32, (LANES, LANES), 0)
    lj = lax.broadcasted_iota(jnp.int32, (LANES, LANES), 1)
    strict_upper = (li < lj).astype(BF16)
    ones = jnp.ones((LANES, LANES), BF16)
    base = jnp.zeros((e_, LANES), F32)
    outs = []
    for g in range(n // LANES):
        xg = x01[:, g * LANES:(g + 1) * LANES].astype(BF16)
        outs.append(base + _dot(xg, strict_upper))
        base = base + _dot(xg, ones)
    return jnp.concatenate(outs, axis=1)


def _route_kernel(aff_ref, idx_ref, gate_ref, sp_s, af_s, ps_s, pe_s, *, cap):
    aff = aff_ref[0]
    e_, n = aff.shape
    n_groups = n // LANES
    rb_rows = min(cap, LANES)
    bits = pltpu.bitcast(aff, jnp.int32)
    thr = jnp.zeros((e_, 1), jnp.int32)
    for bit in range(AFF_BITS - 1, -1, -1):
        cand = thr | (1 << bit)
        cnt = jnp.sum((bits >= cand).astype(F32), axis=1, keepdims=True)
        thr = jnp.where(cnt >= cap, cand, thr)
    gt = bits > thr
    eq = bits == thr
    need = cap - jnp.sum(gt.astype(F32), axis=1, keepdims=True)
    sel = gt | (eq & (_lane_cumsum_exclusive(eq.astype(F32)) < need))
    pos = _lane_cumsum_exclusive(sel.astype(F32))
    selpos = jnp.where(sel, pos, -1.0)
    tok_g = lax.broadcasted_iota(jnp.int32, (n, n_groups), 0) // LANES
    grp = lax.broadcasted_iota(jnp.int32, (n, n_groups), 1)
    cnt_g = _dot(sel.astype(BF16), (tok_g == grp).astype(BF16))
    gi = lax.broadcasted_iota(jnp.int32, (n_groups, n_groups), 0)
    gj = lax.broadcasted_iota(jnp.int32, (n_groups, n_groups), 1)
    start_g = _dot(cnt_g, (gi < gj).astype(F32), HI)
    ps_s[...] = start_g
    pe_s[...] = start_g + cnt_g
    for g in range(n_groups):
        sp_s[g] = selpos[:, g * LANES:(g + 1) * LANES]
        af_s[g] = aff[:, g * LANES:(g + 1) * LANES]
    slot0 = lax.broadcasted_iota(jnp.int32, (rb_rows, LANES), 0).astype(F32)
    lane = lax.broadcasted_iota(jnp.int32, (1, LANES), 1).astype(F32)

    def per_block(k, carry):
        e = k // (cap // rb_rows)
        rb = k % (cap // rb_rows)
        first = lax.convert_element_type(rb * rb_rows, F32)
        slot = slot0 + first

        def per_group(g, acc):
            acc_i, acc_g = acc
            hit = sp_s[g, pl.ds(e, 1), :] == slot
            tok = lane + lax.convert_element_type(g * LANES, F32)
            return (acc_i + jnp.where(hit, tok, 0.0), acc_g + jnp.where(hit, af_s[g, pl.ds(e, 1), :], 0.0))

        g_lo = jnp.sum((pe_s[pl.ds(e, 1), :] <= first).astype(F32), axis=1, keepdims=True)[0, 0].astype(jnp.int32)
        g_hi = jnp.sum((ps_s[pl.ds(e, 1), :] < first + rb_rows).astype(F32), axis=1, keepdims=True)[0, 0].astype(jnp.int32)
        zero = jnp.zeros((rb_rows, LANES), F32)
        acc_i, acc_g = lax.fori_loop(g_lo, g_hi, per_group, (zero, zero))
        rows = pl.ds(pl.multiple_of(rb * rb_rows, rb_rows), rb_rows)
        idx_ref[0, e, rows, :] = jnp.sum(acc_i, axis=1, keepdims=True).astype(jnp.int32)
        gate_ref[0, e, rows, :] = jnp.sum(acc_g, axis=1, keepdims=True)
        return carry

    lax.fori_loop(0, e_ * (cap // rb_rows), per_block, 0)


def route(aff):
    b_, e_, n = aff.shape
    cap = max(1, EC_FACTOR * n // N_EXPERTS)
    grp_scr = lambda: pltpu.VMEM((n // LANES, e_, LANES), F32)
    cnt_scr = lambda: pltpu.VMEM((e_, n // LANES), F32)
    idx, gate = pl.pallas_call(
        functools.partial(_route_kernel, cap=cap),
        grid=(b_,),
        in_specs=[pl.BlockSpec((1, e_, n), lambda b: (b, 0, 0))],
        out_specs=[pl.BlockSpec((1, e_, cap, 1), lambda b: (b, 0, 0, 0)),
                   pl.BlockSpec((1, e_, cap, 1), lambda b: (b, 0, 0, 0))],
        out_shape=[jax.ShapeDtypeStruct((b_, e_, cap, 1), jnp.int32), jax.ShapeDtypeStruct((b_, e_, cap, 1), F32)],
        scratch_shapes=[grp_scr(), grp_scr(), cnt_scr(), cnt_scr()],
        compiler_params=_params(1),
        name="route",
    )(aff)
    return idx.reshape(b_, e_, 1, cap), gate


ROW_UNROLL = 8


def _gather_kernel(idx_ref, h_ref, o_ref, *, cap):
    def body(r, carry):
        o_ref[0, 0, pl.ds(r, 1), :] = h_ref[0, pl.ds(idx_ref[0, 0, 0, r], 1), :]
        return carry
    lax.fori_loop(0, cap, body, 0, unroll=ROW_UNROLL)


def gather_rows(h, idx):
    b_, n, d = h.shape
    _, e_, _, cap = idx.shape
    return pl.pallas_call(
        functools.partial(_gather_kernel, cap=cap),
        grid=(b_, e_),
        in_specs=[pl.BlockSpec((1, 1, 1, cap), lambda b, e: (b, e, 0, 0), memory_space=pltpu.SMEM),
                  pl.BlockSpec((1, n, d), lambda b, e: (b, 0, 0), pipeline_mode=pl.Buffered(1))],
        out_specs=pl.BlockSpec((1, 1, cap, d), lambda b, e: (b, e, 0, 0)),
        out_shape=jax.ShapeDtypeStruct((b_, e_, cap, d), h.dtype),
        compiler_params=_params(2),
        name="gather_rows",
    )(idx, h)


def _expert_ffn_kernel(x_ref, g_ref, g2_ref, w1_ref, w3_ref, w2_ref, o_ref, w1b, w3b, w2b):
    @pl.when((pl.program_id(1) == 0) & (pl.program_id(2) == 0))
    def _():
        w1b[...] = w1_ref[0].astype(BF16)
        w3b[...] = w3_ref[0].astype(BF16)
        w2b[...] = w2_ref[0].astype(BF16)

    x = _unpack_halves(x_ref[0, 0])
    hid = (_silu(_dot(x, w1b[...])) * _dot(x, w3b[...])).astype(BF16)
    o_ref[0, 0] = _dot(hid, w2b[...]) * g_ref[0, 0] * g2_ref[0]


def expert_ffn(xs, gate, g2, w1, w3, w2, layer):
    b_, e_, c_, half = xs.shape
    d = 2 * half
    f = w1.shape[-1]
    tm = min(c_, 512)
    return pl.pallas_call(
        _expert_ffn_kernel,
        grid=(e_, b_, c_ // tm),
        scratch_shapes=[pltpu.VMEM((d, f), BF16), pltpu.VMEM((d, f), BF16), pltpu.VMEM((f, d), BF16)],
        in_specs=[
            pl.BlockSpec((1, 1, tm, half), lambda e, b, m: (b, e, m, 0)),
            pl.BlockSpec((1, 1, tm, 1), lambda e, b, m: (b, e, m, 0)),
            pl.BlockSpec((1, 1, d), lambda e, b, m: (b, 0, 0)),
            pl.BlockSpec((None, 1, d, f), lambda e, b, m: (layer, e, 0, 0)),
            pl.BlockSpec((None, 1, d, f), lambda e, b, m: (layer, e, 0, 0)),
            pl.BlockSpec((None, 1, f, d), lambda e, b, m: (layer, e, 0, 0)),
        ],
        out_specs=pl.BlockSpec((1, 1, tm, d), lambda e, b, m: (b, e, m, 0)),
        out_shape=jax.ShapeDtypeStruct((b_, e_, c_, d), F32),
        compiler_params=_params(3),
        name="expert_ffn",
    )(xs, gate, g2[:, None, :], w1, w3, w2)


def _combine_kernel(idx_ref, y_ref, x1_hbm, o_hbm, acc, sem, *, cap, n_experts):
    b = pl.program_id(0)
    e = pl.program_id(1)

    @pl.when(e == 0)
    def _():
        cp = pltpu.make_async_copy(x1_hbm.at[b], acc, sem.at[0])
        cp.start()
        cp.wait()

    def body(r, carry):
        row = pl.ds(idx_ref[0, 0, 0, r], 1)
        acc[row, :] = acc[row, :] + y_ref[0, 0, pl.ds(r, 1), :]
        return carry
    lax.fori_loop(0, cap, body, 0, unroll=ROW_UNROLL)

    @pl.when(e == n_experts - 1)
    def _():
        cp = pltpu.make_async_copy(acc, o_hbm.at[b], sem.at[1])
        cp.start()
        cp.wait()


def combine_rows(x1, y, idx):
    b_, n, d = x1.shape
    _, e_, _, cap = idx.shape
    return pl.pallas_call(
        functools.partial(_combine_kernel, cap=cap, n_experts=e_),
        grid=(b_, e_),
        in_specs=[pl.BlockSpec((1, 1, 1, cap), lambda b, e: (b, e, 0, 0), memory_space=pltpu.SMEM),
                  pl.BlockSpec((1, 1, cap, d), lambda b, e: (b, e, 0, 0)),
                  pl.BlockSpec(memory_space=pl.ANY)],
        out_specs=pl.BlockSpec(memory_space=pl.ANY),
        out_shape=jax.ShapeDtypeStruct((b_, n, d), F32),
        scratch_shapes=[pltpu.VMEM((n, d), F32), pltpu.SemaphoreType.DMA((2,))],
        compiler_params=_params(2),
        name="combine_rows",
    )(idx, y, x1)


def moe_residual(x1, h2, aff, g2, w1, w3, w2, layer):
    idx, gate = route(aff)
    y = expert_ffn(gather_rows(h2, idx), gate, g2, w1, w3, w2, layer)
    return combine_rows(x1, y, idx)


def kernel(x, c, ctx, c_ctx, w_mod, b_mod, norm1_w, norm2_w, final_norm_w,
           ev_w_in, ev_w_out, a_conv_w, a_log, a_dt_bias, a_norm_w, b_lb_logits, b_norm_w,
           od_w_in, od_w_out, c_sink, d_conv_w, d_conv_b, d_w_r, d_b_r, d_w_i, d_b_i, d_lambda,
           moe_router, moe_w1, moe_w3, moe_w2):
    b_, n, d = x.shape
    cos, sin = _rope_tables(n)
    lb_all = jnp.cumsum(jax.nn.softmax(b_lb_logits.astype(F32), axis=0), axis=0)
    lb_all = lb_all - lb_all[0:1]
    c_rows = jnp.concatenate([c, c_ctx[None, :], jnp.zeros((-(b_ + 1) % SUBLANES, d), F32)], axis=0)
    mod_all = modulation(c_rows, w_mod, b_mod)
    for l in range(DEPTH):
        last = l == DEPTH - 1
        j = l // 2
        router_t = moe_router[l].T.astype(BF16)
        mod = mod_all[l, :b_]
        mod_c = jnp.broadcast_to(mod_all[l, b_], (b_, 6 * d))
        sh1, sc1, g1, sh2, sc2, g2 = jnp.split(mod, 6, axis=-1)
        csh1, csc1, cg1, csh2, csc2, cg2 = jnp.split(mod_c, 6, axis=-1)
        if l % 2 == 0:
            w_in, w_out = _even_w_in(ev_w_in[j]), ev_w_out[j].astype(BF16)
            zero = jnp.zeros((b_, A_HEADS, A_DK, A_DV), F32)
            pars = (a_conv_w[j], a_log[j], a_dt_bias[j], a_norm_w[j], lb_all[j].reshape(2, HB), b_norm_w[j])
            o_c, st = even_mixer_pallas(in_proj(ctx, norm1_w[l], csh1, csc1, w_in, EVEN_GROUPS), *pars,
                                        ((zero, zero), (zero, zero)))
            o_l, _ = even_mixer_pallas(in_proj(x, norm1_w[l], sh1, sc1, w_in, EVEN_GROUPS), *pars, st)
            parts_l, parts_c = [o_l], [o_c]
        else:
            w_in, w_out = od_w_in[j].astype(BF16), od_w_out[j].astype(BF16)
            qc, kc, vc, xdc, gdc = in_proj(ctx, norm1_w[l], csh1, csc1, w_in, ODD_SIZES)
            ql, kl, vl, xdl, gdl = in_proj(x, norm1_w[l], sh1, sc1, w_in, ODD_SIZES)
            att_l = window_attention_pallas(ql, kl, vl, kc, vc, c_sink[j], cos, sin)
            rg_pars = (d_conv_w[j], d_conv_b[j], d_w_r[j], d_b_r[j], d_w_i[j], d_b_i[j], d_lambda[j])
            zero = jnp.zeros((b_, D_WIDTH), F32)
            rg_c, st = rglru_pallas(xdc, gdc, *rg_pars, (zero, zero))
            rg_l, _ = rglru_pallas(xdl, gdl, *rg_pars, st)
            parts_l = [att_l, rg_l]
            if not last:
                parts_c = [context_attention_pallas(qc, kc, vc, c_sink[j]), rg_c]
        x1, h2, aff = post_mixer(parts_l, w_out, x, g1, norm2_w[l], sh2, sc2, router_t)
        x = moe_residual(x1, h2, aff, g2, moe_w1, moe_w3, moe_w2, l)
        if not last:
            c1, hc2, affc = post_mixer(parts_c, w_out, ctx, cg1, norm2_w[l], csh2, csc2, router_t)
            ctx = moe_residual(c1, hc2, affc, cg2, moe_w1, moe_w3, moe_w2, l)
    return final_norm(x, final_norm_w)
```

```python
import functools

import jax
import jax.numpy as jnp
from jax import lax
from jax.experimental import pallas as pl
from jax.experimental.pallas import tpu as pltpu

D_MODEL = 1024
DEPTH = 4
GRID_W = 64
EPS = 1e-6
F32 = jnp.float32
BF16 = jnp.bfloat16
HI = lax.Precision.HIGHEST

A_HEADS = 4
A_DK = 128
A_DV = 128
A_CONV = 4
A_CHUNK = 64
A_QKV = 2 * A_HEADS * A_DK + A_HEADS * A_DV
B_HEADS = 4
B_DK = 128
B_DV = 128
B_CHUNK = 64
C_HEADS = 8
C_KV_HEADS = 2
C_HD = 64
C_WIN = 128
C_BLOCK = 128
ROPE_THETA = 10000.0
D_WIDTH = 512
D_BLOCKS = 8
D_BW = D_WIDTH // D_BLOCKS
D_CONV = 4
D_C = 8.0
N_EXPERTS = 16
EXPERT_FF = 1024
EC_FACTOR = 2

EVEN_SIZES = (A_QKV, A_HEADS * A_DV, 2 * A_HEADS, 2 * A_HEADS,
              B_HEADS * B_DK, B_HEADS * B_DV, 2 * B_HEADS * B_DK, B_HEADS * B_DV)
ODD_SIZES = (C_HEADS * C_HD, C_KV_HEADS * C_HD, C_KV_HEADS * C_HD, D_WIDTH, D_WIDTH)

LANES = 128
SUBLANES = 8
VMEM_LIMIT_BYTES = 56 * 1024 * 1024

ROW_TILE = 512
HALO = SUBLANES


def _cuts(sizes):
    out, acc = [], 0
    for s in sizes[:-1]:
        acc += s
        out.append(acc)
    return out


def _params(n_axes):
    return pltpu.CompilerParams(dimension_semantics=("arbitrary",) * n_axes, vmem_limit_bytes=VMEM_LIMIT_BYTES)


def _dot(a, b, precision=None):
    return jnp.dot(a, b, preferred_element_type=F32, precision=precision)


def _dot_nt(a, b, precision=None):
    return lax.dot_general(a, b, (((1,), (1,)), ((), ())), preferred_element_type=F32, precision=precision)


def _dot_tn(a, b):
    return lax.dot_general(a, b, (((0,), (0,)), ((), ())), preferred_element_type=F32)


def _silu(x):
    return x * jax.nn.sigmoid(x)


def _rms(x):
    return x * lax.rsqrt(jnp.mean(x * x, axis=-1, keepdims=True) + EPS)


U32 = jnp.uint32
HIGH16 = 0xFFFF0000


def _pack_halves(h):
    half = h.shape[-1] // 2
    lo = pltpu.bitcast(h[:, :half].astype(BF16).astype(F32), U32)
    hi = pltpu.bitcast(h[:, half:].astype(BF16).astype(F32), U32)
    return (hi & jnp.uint32(HIGH16)) | lax.shift_right_logical(lo, jnp.uint32(16))


def _unpack_halves(p):
    lo = pltpu.bitcast(lax.shift_left(p, jnp.uint32(16)), F32).astype(BF16)
    hi = pltpu.bitcast(p & jnp.uint32(HIGH16), F32).astype(BF16)
    return jnp.concatenate([lo, hi], axis=1)


MOD_COLS = 1536


def _mod_kernel(c_ref, w_ref, b_ref, o_ref):
    o_ref[0] = _dot(_silu(c_ref[...]), w_ref[0]) + b_ref[0]


def modulation(c_rows, w_mod, b_mod):
    r, d = c_rows.shape
    depth, _, wide = w_mod.shape
    return pl.pallas_call(
        _mod_kernel,
        grid=(depth, wide // MOD_COLS),
        in_specs=[pl.BlockSpec((r, d), lambda l, j: (0, 0)),
                  pl.BlockSpec((1, d, MOD_COLS), lambda l, j: (l, 0, j)),
                  pl.BlockSpec((1, 1, MOD_COLS), lambda l, j: (l, 0, j))],
        out_specs=pl.BlockSpec((1, r, MOD_COLS), lambda l, j: (l, 0, j)),
        out_shape=jax.ShapeDtypeStruct((depth, r, wide), F32),
        compiler_params=_params(2),
        name="modulation",
    )(c_rows, w_mod, b_mod[:, None, :])


def _norm_mod(x, nw, shift, scale):
    return _rms(x) * nw * (1.0 + scale) + shift


def _in_proj_kernel(x_ref, nw_ref, sh_ref, sc_ref, w_ref, *o_refs, splits):
    h = _norm_mod(x_ref[0], nw_ref[...], sh_ref[0], sc_ref[0]).astype(BF16)
    off = 0
    for o_ref, s in zip(o_refs, splits):
        o_ref[0] = _dot(h, w_ref[:, off:off + s])
        off += s


def in_proj(x, norm_w, shift, scale, w, splits):
    b_, n, d = x.shape
    tm = min(n, ROW_TILE)
    ntot = sum(splits)
    vec = pl.BlockSpec((1, 1, d), lambda b, i: (b, 0, 0))
    return pl.pallas_call(
        functools.partial(_in_proj_kernel, splits=tuple(splits)),
        grid=(b_, n // tm),
        in_specs=[pl.BlockSpec((1, tm, d), lambda b, i: (b, i, 0)),
                  pl.BlockSpec((1, d), lambda b, i: (0, 0)), vec, vec,
                  pl.BlockSpec((d, ntot), lambda b, i: (0, 0))],
        out_specs=[pl.BlockSpec((1, tm, s), lambda b, i: (b, i, 0)) for s in splits],
        out_shape=[jax.ShapeDtypeStruct((b_, n, s), F32) for s in splits],
        compiler_params=_params(2),
        name="in_proj",
    )(x, norm_w.reshape(1, d), shift[:, None, :], scale[:, None, :], w)


def _post_mixer_kernel(*refs, n_parts):
    parts = refs[:n_parts]
    w_ref, x_ref, g1_ref, nw_ref, sh_ref, sc_ref, rt_ref, x1_ref, h2_ref, aff_ref = refs[n_parts:]
    y = None
    off = 0
    for p_ref in parts:
        k = p_ref.shape[-1]
        t = _dot(p_ref[0].astype(BF16), w_ref[off:off + k, :])
        y = t if y is None else y + t
        off += k
    x1 = x_ref[0] + g1_ref[0] * y
    x1_ref[0] = x1
    h2 = _norm_mod(x1, nw_ref[...], sh_ref[0], sc_ref[0])
    h2_ref[0] = _pack_halves(h2)
    logits = _dot_nt(rt_ref[...], h2.astype(BF16))
    e = jnp.exp(logits - jnp.max(logits, axis=0, keepdims=True))
    aff_ref[0] = e / jnp.sum(e, axis=0, keepdims=True)


def post_mixer(parts, w_out, x, g1, norm_w, shift, scale, router_t):
    b_, n, d = x.shape
    e_ = router_t.shape[0]
    tm = min(n, ROW_TILE)
    vec = pl.BlockSpec((1, 1, d), lambda b, i: (b, 0, 0))
    tok = lambda k: pl.BlockSpec((1, tm, k), lambda b, i: (b, i, 0))
    return pl.pallas_call(
        functools.partial(_post_mixer_kernel, n_parts=len(parts)),
        grid=(b_, n // tm),
        in_specs=[tok(p.shape[-1]) for p in parts] + [
            pl.BlockSpec(w_out.shape, lambda b, i: (0, 0)), tok(d), vec,
            pl.BlockSpec((1, d), lambda b, i: (0, 0)), vec, vec,
            pl.BlockSpec((e_, d), lambda b, i: (0, 0))],
        out_specs=[tok(d), tok(d // 2), pl.BlockSpec((1, e_, tm), lambda b, i: (b, 0, i))],
        out_shape=[jax.ShapeDtypeStruct((b_, n, d), F32), jax.ShapeDtypeStruct((b_, n, d // 2), U32),
                   jax.ShapeDtypeStruct((b_, e_, n), F32)],
        compiler_params=_params(2),
        name="post_mixer",
    )(*parts, w_out, x, g1[:, None, :], norm_w.reshape(1, d), shift[:, None, :], scale[:, None, :], router_t)


def _final_kernel(x_ref, nw_ref, o_ref):
    o_ref[0] = _rms(x_ref[0]) * nw_ref[...]


def final_norm(x, norm_w):
    b_, n, d = x.shape
    tm = min(n, 2 * ROW_TILE)
    tok = pl.BlockSpec((1, tm, d), lambda b, i: (b, i, 0))
    return pl.pallas_call(
        _final_kernel,
        grid=(b_, n // tm),
        in_specs=[tok, pl.BlockSpec((1, d), lambda b, i: (0, 0))],
        out_specs=tok,
        out_shape=jax.ShapeDtypeStruct((b_, n, d), F32),
        compiler_params=_params(2),
        name="final_norm",
    )(x, norm_w.reshape(1, d))


CHUNK = A_CHUNK
SUB = 16
EVEN_TILE = 256
HA = A_HEADS * A_DK
HB = B_HEADS * B_DK
AB_PAD = LANES
EVEN_GROUPS = (A_QKV, A_HEADS * A_DV, AB_PAD, HB, B_HEADS * B_DV, 2 * HB, B_HEADS * B_DV)
NEUMANN_STEPS = 5


def _split2(x):
    hi = x.astype(BF16)
    return hi, (x - hi.astype(F32)).astype(BF16)


def _dot_split(a, b):
    (ah, al), (bh, bl) = a, b
    return _dot(ah, bh) + _dot(ah, bl) + _dot(al, bh)


def _dot_exact_lhs(a_bf16, x):
    x1 = x.astype(BF16)
    r1 = x - x1.astype(F32)
    x2 = r1.astype(BF16)
    x3 = (r1 - x2.astype(F32)).astype(BF16)
    return _dot(a_bf16, x1) + _dot(a_bf16, x2) + _dot(a_bf16, x3)


def _even_kernel(*refs, tile, n_tiles, reverse, combine):
    if combine:
        (qkvn_ref, ab_ref, qb_ref, ib_ref, fb_ref, exp_ref, alog_ref, dtb_ref, lb_ref,
         sa0_ref, sb0_ref, of_ref, ga_ref, gb_ref, anw_ref, bnw_ref,
         o_ref, sa_ref, sb_ref, g_scr, st_a, st_b) = refs
    else:
        (qkv_ref, qkvp_ref, qkvnx_ref, ab_ref, qb_ref, ib_ref, fb_ref, cw_ref, exp_ref, alog_ref, dtb_ref, lb_ref,
         sa0_ref, sb0_ref,
         o_ref, qkvn_ref, sa_ref, sb_ref, xbuf, g_scr, st_a, st_b) = refs
    i = pl.program_id(1)
    t = (n_tiles - 1 - i) if reverse else i

    @pl.when(i == 0)
    def _():
        st_a[...] = sa0_ref[0]
        st_b[...] = sb0_ref[0]

    L = CHUNK
    n_chunks = tile // L
    n_blocks = tile // SUB
    chunk_order = range(n_chunks - 1, -1, -1) if reverse else range(n_chunks)
    block_order = range(n_blocks - 1, -1, -1) if reverse else range(n_blocks)

    ti = lax.broadcasted_iota(jnp.int32, (tile, tile), 0)
    tj = lax.broadcasted_iota(jnp.int32, (tile, tile), 1)
    t_incl = (tj >= ti) if reverse else (tj <= ti)
    tri_chunk = (t_incl & ((ti // L) == (tj // L))).astype(BF16)
    tri_blk = (t_incl & ((ti // SUB) == (tj // SUB))).astype(BF16)
    ones_blk = ((ti // SUB) == (tj // SUB)).astype(BF16)
    ii = lax.broadcasted_iota(jnp.int32, (L, L), 0)
    jj = lax.broadcasted_iota(jnp.int32, (L, L), 1)
    incl = (jj >= ii) if reverse else (jj <= ii)
    strict = (jj > ii) if reverse else (jj < ii)
    eye = (ii == jj).astype(F32)
    last_row = 0 if reverse else L - 1

    if not combine:
        xbuf[pl.ds(0, HALO), :] = jnp.where(t > 0, qkvp_ref[0], 0.0)
        xbuf[pl.ds(HALO, tile), :] = qkv_ref[0]
        xbuf[pl.ds(HALO + tile, HALO), :] = jnp.where(t < n_tiles - 1, qkvnx_ref[0], 0.0)
        conv = _silu(sum(cw_ref[pl.ds(k, 1), :] * xbuf[pl.ds(HALO - 1 + k, tile), :] for k in range(A_CONV)))
    ab_x = _dot(ab_ref[0], exp_ref[...], HI)
    g_all = -jnp.exp(alog_ref[...]) * jax.nn.softplus(ab_x[:, :HA] + dtb_ref[...])
    beta_all = jax.nn.sigmoid(ab_x[:, HA:])
    cum_all = _dot_exact_lhs(tri_chunk, g_all)
    ecum_all = jnp.exp(cum_all)

    chains = [(c, h) for c in chunk_order for h in range(A_HEADS)]
    pre = {}
    neg_m = []
    for c, h in chains:
        rs = slice(c * L, (c + 1) * L)
        hs = slice(h * A_DK, (h + 1) * A_DK)
        ks_, vs_ = slice(HA + h * A_DK, HA + (h + 1) * A_DK), slice(2 * HA + h * A_DV, 2 * HA + (h + 1) * A_DV)
        if combine:
            q, k, v = qkvn_ref[0, rs, hs], qkvn_ref[0, rs, ks_], qkvn_ref[0, rs, vs_]
        else:
            qh, kh, v = conv[rs, hs], conv[rs, ks_], conv[rs, vs_]
            q = qh * (lax.rsqrt(jnp.sum(qh * qh, axis=-1, keepdims=True) + EPS) * (A_DK ** -0.5))
            k = kh * lax.rsqrt(jnp.sum(kh * kh, axis=-1, keepdims=True) + EPS)
            qkvn_ref[0, rs, hs], qkvn_ref[0, rs, ks_], qkvn_ref[0, rs, vs_] = q, k, v
        beta, cum, ecum = beta_all[rs, hs], cum_all[rs, hs], ecum_all[rs, hs]
        r_ll = cum.T[:L, :]
        decay = jnp.where(incl, jnp.exp(jnp.where(incl, cum[:, :L] - r_ll, 0.0)), 0.0)
        kb = k * beta
        k16 = k.astype(BF16)
        neg_m.append(jnp.where(strict, -_dot_nt(kb.astype(BF16), k16) * decay, 0.0))
        qk = _dot_nt(q.astype(BF16), k16) * decay
        total = cum[last_row:last_row + 1, :]
        pre[(c, h)] = dict(rhs=jnp.concatenate([v * beta, kb * ecum], axis=1), qk=qk.astype(BF16),
                           qd=(q * ecum).astype(BF16), kd=(k * jnp.exp(total - cum)).astype(BF16),
                           gl=jnp.exp(total))
    p = [_split2(m) for m in neg_m]
    tinv = [eye + m for m in neg_m]
    for _ in range(NEUMANN_STEPS):
        p = [_split2(_dot_split(x, x)) for x in p]
        tinv = [tv + _dot_split(_split2(tv), x) for tv, x in zip(tinv, p)]
    sol_of = {ch: _dot_split(_split2(tv), _split2(pre[ch]["rhs"])) for tv, ch in zip(tinv, chains)}

    for c in chunk_order:
        rs = slice(c * L, (c + 1) * L)
        s_old = [st_a[h] for h in range(A_HEADS)]
        s16 = [s.astype(BF16) for s in s_old]
        ws = [_dot(sol_of[(c, h)][:, A_DV:].astype(BF16), s16[h]) for h in range(A_HEADS)]
        qs_ = [_dot(pre[(c, h)]["qd"], s16[h]) for h in range(A_HEADS)]
        v_new = [(sol_of[(c, h)][:, :A_DV] - ws[h]).astype(BF16) for h in range(A_HEADS)]
        for h in range(A_HEADS):
            o_ref[0, rs, h * A_DV:(h + 1) * A_DV] = qs_[h] + _dot(pre[(c, h)]["qk"], v_new[h])
            st_a[h] = s_old[h] * pre[(c, h)]["gl"] + _dot_tn(pre[(c, h)]["kd"], v_new[h])

    hq = _silu(qb_ref[0])
    hv = ib_ref[0]
    lb = lb_ref[...]
    fg = lb + (1.0 - lb) * jax.nn.sigmoid(fb_ref[0])
    lf = jnp.log(fg)
    hk = 1.0 - fg
    b_all = _dot_exact_lhs(tri_blk, lf)
    e_all = _dot_exact_lhs(ones_blk, lf)
    qs_all = (hq * jnp.exp(b_all)).astype(BF16)
    ks_all = (hk * jnp.exp(e_all - b_all)).astype(BF16)
    hv16 = hv.astype(BF16)
    row_in_blk = lax.broadcasted_iota(jnp.int32, (tile, HB), 0) % SUB
    od = [jnp.zeros((tile, B_DV), F32) for _ in range(B_HEADS)]
    for s_ in range(SUB):
        valid = (row_in_blk + s_ < SUB) if reverse else (row_in_blk >= s_)
        shift = ((tile - s_) if reverse else s_) % tile
        roll = (lambda x: x) if s_ == 0 else (lambda x: pltpu.roll(x, shift, axis=0))
        prod = jnp.where(valid, hq * roll(hk) * jnp.exp(b_all - roll(b_all)), 0.0)
        vv = roll(hv)
        for h in range(B_HEADS):
            hs = slice(h * B_DK, (h + 1) * B_DK)
            od[h] = od[h] + jnp.sum(prod[:, hs], axis=-1, keepdims=True) * vv[:, hs]
    for blk in block_order:
        bs = slice(blk * SUB, (blk + 1) * SUB)
        for h in range(B_HEADS):
            hs = slice(h * B_DK, (h + 1) * B_DK)
            g_scr[blk * B_HEADS + h] = _dot_tn(hv16[bs, hs], ks_all[bs, hs])

    st = [st_b[h] for h in range(B_HEADS)]
    for blk in block_order:
        bs = slice(blk * SUB, (blk + 1) * SUB)
        for h in range(B_HEADS):
            hs = slice(h * B_DK, (h + 1) * B_DK)
            o_ref[0, bs, HA + h * B_DV:HA + (h + 1) * B_DV] = _dot_nt(qs_all[bs, hs], st[h].astype(BF16)) + od[h][bs]
            st[h] = st[h] * jnp.exp(e_all[blk * SUB:blk * SUB + 1, hs]) + g_scr[blk * B_HEADS + h]
    for h in range(B_HEADS):
        st_b[h] = st[h]

    if combine:
        tot = of_ref[0] + o_ref[0]
        for h in range(A_HEADS + B_HEADS):
            hs = slice(h * A_DV, (h + 1) * A_DV)
            nw = anw_ref[...] if h < A_HEADS else bnw_ref[...]
            gate = ga_ref[0, :, hs] if h < A_HEADS else gb_ref[0, :, pl.ds((h - A_HEADS) * B_DV, B_DV)]
            o_ref[0, :, hs] = _rms(tot[:, hs]) * nw * _silu(gate)

    @pl.when(i == n_tiles - 1)
    def _():
        sa_ref[0] = st_a[...]
        sb_ref[0] = st_b[...]


def _even_sweep(outs, z, conv_w, a_log, dt_bias, lb, s_a, s_b, of=None, a_norm_w=None, b_norm_w=None):
    qkv, ga, ab, qb, ib, fb, gb = outs
    b_, n, _ = qkv.shape
    reverse = z == 1
    combine = of is not None
    tile = min(n, EVEN_TILE)
    n_tiles = n // tile
    bpt = tile // HALO
    nblk = n // HALO
    tidx = lambda i: (n_tiles - 1 - i) if reverse else i
    tmap = lambda b, i: (b, tidx(i), 0)
    pmap = lambda b, i: (b, jnp.maximum(tidx(i) * bpt - 1, 0), 0)
    nmap = lambda b, i: (b, jnp.minimum((tidx(i) + 1) * bpt, nblk - 1), 0)
    zmap = lambda b, i: (b, tidx(i), z)
    const2 = lambda b, i: (0, 0)
    smap = lambda b, i: (b, 0, 0, 0)
    src = lax.broadcasted_iota(jnp.int32, (AB_PAD, 2 * HA), 0)
    dst = lax.broadcasted_iota(jnp.int32, (AB_PAD, 2 * HA), 1)
    expand = (src == jnp.where(dst >= HA, 2 * A_HEADS, 0) + z * A_HEADS + (dst % HA) // A_DK).astype(F32)
    rep = lambda p_: jnp.broadcast_to(p_[:, None], (A_HEADS, A_DK)).reshape(1, HA)
    tok = lambda w: pl.BlockSpec((1, tile, w), tmap)
    common_specs = [tok(AB_PAD), tok(HB), tok(HB), pl.BlockSpec((1, tile, HB), zmap)]
    common_args = [ab, qb, ib, fb]
    par_specs = [pl.BlockSpec((AB_PAD, 2 * HA), const2),
                 pl.BlockSpec((1, HA), const2), pl.BlockSpec((1, HA), const2), pl.BlockSpec((1, HB), const2),
                 pl.BlockSpec((1, A_HEADS, A_DK, A_DV), smap), pl.BlockSpec((1, B_HEADS, B_DV, B_DK), smap)]
    par_args = [expand, rep(a_log[z]), rep(dt_bias[z]), lb[z].reshape(1, HB), s_a, s_b]
    state_specs = [pl.BlockSpec((1, A_HEADS, A_DK, A_DV), smap), pl.BlockSpec((1, B_HEADS, B_DV, B_DK), smap)]
    state_shapes = [jax.ShapeDtypeStruct(s_a.shape, F32), jax.ShapeDtypeStruct(s_b.shape, F32)]
    scratch = [pltpu.VMEM((tile // SUB * B_HEADS, B_DV, B_DK), F32),
               pltpu.VMEM((A_HEADS, A_DK, A_DV), F32), pltpu.VMEM((B_HEADS, B_DV, B_DK), F32)]
    if combine:
        in_specs = [tok(A_QKV)] + common_specs + par_specs + [
            tok(HA + HB), tok(HA), tok(HB), pl.BlockSpec((1, A_DV), const2), pl.BlockSpec((1, B_DV), const2)]
        args = [qkv] + common_args + par_args + [of, ga, gb, a_norm_w.reshape(1, A_DV), b_norm_w.reshape(1, B_DV)]
        out_specs = [tok(HA + HB)] + state_specs
        out_shape = [jax.ShapeDtypeStruct((b_, n, HA + HB), F32)] + state_shapes
    else:
        in_specs = [tok(A_QKV), pl.BlockSpec((1, HALO, A_QKV), pmap), pl.BlockSpec((1, HALO, A_QKV), nmap)] + \
            common_specs + [pl.BlockSpec((A_CONV, A_QKV), const2)] + par_specs
        args = [qkv, qkv, qkv] + common_args + [conv_w] + par_args
        out_specs = [tok(HA + HB), tok(A_QKV)] + state_specs
        out_shape = [jax.ShapeDtypeStruct((b_, n, HA + HB), F32), jax.ShapeDtypeStruct((b_, n, A_QKV), F32)] + state_shapes
        scratch = [pltpu.VMEM((tile + 2 * HALO, A_QKV), F32)] + scratch
    return pl.pallas_call(
        functools.partial(_even_kernel, tile=tile, n_tiles=n_tiles, reverse=reverse, combine=combine),
        grid=(b_, n_tiles),
        in_specs=in_specs,
        out_specs=out_specs,
        out_shape=out_shape,
        scratch_shapes=scratch,
        compiler_params=_params(2),
        name="even_bwd_combine" if combine else "even_fwd",
    )(*args)


def even_mixer_pallas(outs, conv_w, a_log, dt_bias, a_norm_w, lb, b_norm_w, states):
    (saf, sbf), (sab, sbb) = states
    o_f, qkv_n, saf, sbf = _even_sweep(outs, 0, conv_w, a_log, dt_bias, lb, saf, sbf)
    outs_b = [qkv_n] + list(outs[1:])
    o, sab, sbb = _even_sweep(outs_b, 1, conv_w, a_log, dt_bias, lb, sab, sbb, o_f, a_norm_w, b_norm_w)
    return o, ((saf, sbf), (sab, sbb))


def _even_w_in(w):
    c = _cuts(EVEN_SIZES)
    ab = jnp.pad(w[:, c[1]:c[3]], ((0, 0), (0, AB_PAD - (c[3] - c[1]))))
    return jnp.concatenate([w[:, :c[1]], ab, w[:, c[3]:]], axis=1).astype(BF16)


ROPE_NF = C_HD // 4
NEG_BIG = -1e30


def _rope(t, cos, sin_signed):
    w = t.shape[-1]
    lane = lax.broadcasted_iota(jnp.int32, t.shape, 1)
    partner = jnp.where(lane % (2 * ROPE_NF) < ROPE_NF,
                        pltpu.roll(t, w - ROPE_NF, axis=1), pltpu.roll(t, ROPE_NF, axis=1))
    return t * cos + partner * sin_signed


def _dup_kv_head(t, kh):
    lane = lax.broadcasted_iota(jnp.int32, t.shape, 1)
    rolled = pltpu.roll(t, C_HD, axis=1)
    own_half = (lane < C_HD) if kh == 0 else (lane >= C_HD)
    return jnp.where(own_half, t, rolled)


def _attn_kernel(*refs, banded, n_blocks):
    if banded:
        (q_ref, kp_ref, k_ref, kn_ref, vp_ref, v_ref, vn_ref, cq_ref, sq_ref, cp_ref, sp_ref, cn_ref, sn_ref,
         kc_ref, vc_ref, sink_ref, o_ref) = refs
    else:
        q_ref, kc_ref, vc_ref, sink_ref, o_ref = refs
    nb = pl.program_id(1)
    blk = q_ref.shape[1]
    q = q_ref[0] * (C_HD ** -0.5)
    kc = kc_ref[0]
    vc = vc_ref[0]
    if banded:
        cq, sq = cq_ref[...], sq_ref[...]
        q = _rope(q, jnp.concatenate([cq] * (C_HEADS // C_KV_HEADS), axis=1),
                  jnp.concatenate([sq] * (C_HEADS // C_KV_HEADS), axis=1))
        kb = [_rope(kp_ref[0], cp_ref[...], sp_ref[...]), _rope(k_ref[0], cq, sq), _rope(kn_ref[0], cn_ref[...], sn_ref[...])]
        keys = jnp.concatenate(kb + [kc], axis=0)
        vals = jnp.concatenate([vp_ref[0], v_ref[0], vn_ref[0], vc], axis=0)
        qpos = lax.broadcasted_iota(jnp.int32, (blk, keys.shape[0]), 0)
        kcol = lax.broadcasted_iota(jnp.int32, (blk, keys.shape[0]), 1)
        rel = kcol - blk
        in_band = (jnp.abs(rel - qpos) <= C_WIN) & (rel + nb * blk >= 0) & (rel + nb * blk < n_blocks * blk)
        mask = in_band | (kcol >= 3 * blk)
    else:
        keys, vals, mask = kc, vc, None
    keys = keys.astype(BF16)
    vals = vals.astype(BF16)
    lane_q = lax.broadcasted_iota(jnp.int32, (blk, 2 * C_HD), 1)
    grp = C_HEADS // C_KV_HEADS
    heads = range(C_HEADS)
    k_dup = [_dup_kv_head(keys, kh) for kh in range(C_KV_HEADS)]
    v_dup = [_dup_kv_head(vals, kh) for kh in range(C_KV_HEADS)]
    qh = [jnp.where((lane_q < C_HD) if h % 2 == 0 else (lane_q >= C_HD),
                    q[:, (h // 2) * 2 * C_HD:(h // 2 + 1) * 2 * C_HD], 0.0).astype(BF16) for h in heads]
    s = [_dot_nt(qh[h], k_dup[h // grp]) for h in heads]
    if mask is not None:
        s = [jnp.where(mask, x, NEG_BIG) for x in s]
    m = [jnp.maximum(jnp.max(s[h], axis=-1, keepdims=True), sink_ref[h]) for h in heads]
    e = [jnp.exp(s[h] - m[h]) for h in heads]
    denom = [jnp.sum(e[h], axis=-1, keepdims=True) + jnp.exp(sink_ref[h] - m[h]) for h in heads]
    p = [(e[h] / denom[h]).astype(BF16) for h in heads]
    out = [_dot(p[h], v_dup[h // grp]) for h in heads]
    for pair in range(C_HEADS // 2):
        o_ref[0, :, pair * 2 * C_HD:(pair + 1) * 2 * C_HD] = jnp.where(lane_q < C_HD, out[2 * pair], out[2 * pair + 1])


def _rope_tables(n):
    tok = lax.iota(jnp.int32, n)
    row = (tok // GRID_W).astype(F32)
    col = (tok % GRID_W).astype(F32)
    inv = ROPE_THETA ** (-lax.iota(F32, ROPE_NF) / ROPE_NF)
    ang_r, ang_c = row[:, None] * inv, col[:, None] * inv
    cos = jnp.concatenate([jnp.cos(ang_r)] * 2 + [jnp.cos(ang_c)] * 2, axis=-1)
    sin = jnp.concatenate([-jnp.sin(ang_r), jnp.sin(ang_r), -jnp.sin(ang_c), jnp.sin(ang_c)], axis=-1)
    return jnp.concatenate([cos, cos], axis=-1), jnp.concatenate([sin, sin], axis=-1)


def window_attention_pallas(q, k, v, kc, vc, sink, cos, sin):
    b_, n, hq = q.shape
    hk = k.shape[-1]
    lc = kc.shape[1]
    blk = C_BLOCK
    nb_ = n // blk
    cur = lambda b, i: (b, i, 0)
    prv = lambda b, i: (b, jnp.maximum(i - 1, 0), 0)
    nxt = lambda b, i: (b, jnp.minimum(i + 1, nb_ - 1), 0)
    tcur = lambda b, i: (i, 0)
    tprv = lambda b, i: (jnp.maximum(i - 1, 0), 0)
    tnxt = lambda b, i: (jnp.minimum(i + 1, nb_ - 1), 0)
    kspec = lambda m: pl.BlockSpec((1, blk, hk), m)
    tspec = lambda m: pl.BlockSpec((blk, hk), m)
    cspec = pl.BlockSpec((1, lc, hk), lambda b, i: (b, 0, 0))
    return pl.pallas_call(
        functools.partial(_attn_kernel, banded=True, n_blocks=nb_),
        grid=(b_, nb_),
        in_specs=[pl.BlockSpec((1, blk, hq), cur), kspec(prv), kspec(cur), kspec(nxt),
                  kspec(prv), kspec(cur), kspec(nxt),
                  tspec(tcur), tspec(tcur), tspec(tprv), tspec(tprv), tspec(tnxt), tspec(tnxt),
                  cspec, cspec, pl.BlockSpec(memory_space=pltpu.SMEM)],
        out_specs=pl.BlockSpec((1, blk, hq), cur),
        out_shape=jax.ShapeDtypeStruct((b_, n, hq), F32),
        compiler_params=_params(2),
        name="window_attention",
    )(q, k, k, k, v, v, v, cos, sin, cos, sin, cos, sin, kc, vc, sink)


def context_attention_pallas(q, kc, vc, sink):
    b_, lc, hq = q.shape
    hk = kc.shape[-1]
    blk = min(lc, C_BLOCK)
    cspec = pl.BlockSpec((1, lc, hk), lambda b, i: (b, 0, 0))
    return pl.pallas_call(
        functools.partial(_attn_kernel, banded=False, n_blocks=lc // blk),
        grid=(b_, lc // blk),
        in_specs=[pl.BlockSpec((1, blk, hq), lambda b, i: (b, i, 0)), cspec, cspec,
                  pl.BlockSpec(memory_space=pltpu.SMEM)],
        out_specs=pl.BlockSpec((1, blk, hq), lambda b, i: (b, i, 0)),
        out_shape=jax.ShapeDtypeStruct((b_, lc, hq), F32),
        compiler_params=_params(2),
        name="context_attention",
    )(q, kc, vc, sink)


def _scan8(a, u, reverse):
    row = lax.broadcasted_iota(jnp.int32, a.shape, 0)
    for s in (1, 2, 4):
        shift = (SUBLANES - s) if reverse else s
        a_sh = pltpu.roll(a, shift, axis=0)
        u_sh = pltpu.roll(u, shift, axis=0)
        valid = (row < SUBLANES - s) if reverse else (row >= s)
        u = jnp.where(valid, a * u_sh + u, u)
        a = jnp.where(valid, a * a_sh, a)
    return a, u


def _gelu_tanh(x):
    return 0.5 * x * (1.0 + jnp.tanh(0.7978845608028654 * (x + 0.044715 * (x * x * x))))


def _rglru_kernel(*refs, tile, n_tiles, reverse, combine):
    if combine:
        (x_ref, xp_ref, xn_ref, cw_ref, cb_ref, wbd_ref, bias_ref, lam_ref, h0_ref, hf_ref, gd_ref,
         o_ref, hl_ref, xbuf, a_s, u_s, carry) = refs
    else:
        (x_ref, xp_ref, xn_ref, cw_ref, cb_ref, wbd_ref, bias_ref, lam_ref, h0_ref,
         o_ref, hl_ref, xbuf, a_s, u_s, carry) = refs
    i = pl.program_id(1)
    t = (n_tiles - 1 - i) if reverse else i
    w = x_ref.shape[-1]

    @pl.when(i == 0)
    def _():
        carry[...] = jnp.broadcast_to(h0_ref[0], (SUBLANES, w))

    xbuf[pl.ds(0, HALO), :] = jnp.where(t > 0, xp_ref[0], 0.0)
    xbuf[pl.ds(HALO, tile), :] = x_ref[0]
    xbuf[pl.ds(HALO + tile, HALO), :] = jnp.where(t < n_tiles - 1, xn_ref[0], 0.0)
    xc = cb_ref[...] + sum(cw_ref[pl.ds(k, 1), :] * xbuf[pl.ds(HALO - 1 + k, tile), :] for k in range(D_CONV))

    z = _dot(xc.astype(BF16), wbd_ref[...]) + bias_ref[...]
    r = jax.nn.sigmoid(z[:, :w])
    gi = jax.nn.sigmoid(z[:, w:])
    lam = lam_ref[...]
    log_a = (-D_C * jnp.log(1.0 + jnp.exp(-lam))) * r
    a = jnp.exp(log_a)
    a_s[...] = a
    u_s[...] = jnp.sqrt(1.0 - a * a) * gi * xc

    n_groups = tile // SUBLANES

    def step(g, c):
        gg = (n_groups - 1 - g) if reverse else g
        r0 = pl.multiple_of(gg * SUBLANES, SUBLANES)
        ac, uc = _scan8(a_s[pl.ds(r0, SUBLANES), :], u_s[pl.ds(r0, SUBLANES), :], reverse)
        h = ac * c + uc
        o_ref[0, pl.ds(r0, SUBLANES), :] = h
        last = h[0:1, :] if reverse else h[SUBLANES - 1:SUBLANES, :]
        return jnp.broadcast_to(last, (SUBLANES, w))

    c_fin = lax.fori_loop(0, n_groups, step, carry[...], unroll=4 if n_groups % 4 == 0 else 1)
    carry[...] = c_fin

    if combine:
        o_ref[0] = (hf_ref[0] + o_ref[0]) * _gelu_tanh(gd_ref[0])

    @pl.when(i == n_tiles - 1)
    def _():
        hl_ref[0] = c_fin[0:1, :]


def _rglru_sweep(xd, conv_w, conv_b, wbd, bias, lam, h0, hf=None, gd=None, *, reverse):
    b_, n, w = xd.shape
    combine = hf is not None
    tile = min(n, 512)
    n_tiles = n // tile
    blocks_per_tile = tile // HALO
    n_blocks = n // HALO

    def tmap(b, i):
        return (b, (n_tiles - 1 - i) if reverse else i, 0)

    def pmap(b, i):
        t = (n_tiles - 1 - i) if reverse else i
        return (b, jnp.maximum(t * blocks_per_tile - 1, 0), 0)

    def nmap(b, i):
        t = (n_tiles - 1 - i) if reverse else i
        return (b, jnp.minimum((t + 1) * blocks_per_tile, n_blocks - 1), 0)

    const2 = lambda b, i: (0, 0)
    in_specs = [
        pl.BlockSpec((1, tile, w), tmap),
        pl.BlockSpec((1, HALO, w), pmap),
        pl.BlockSpec((1, HALO, w), nmap),
        pl.BlockSpec((D_CONV, w), const2),
        pl.BlockSpec((1, w), const2),
        pl.BlockSpec((w, 2 * w), const2),
        pl.BlockSpec((1, 2 * w), const2),
        pl.BlockSpec((1, w), const2),
        pl.BlockSpec((1, 1, w), lambda b, i: (b, 0, 0)),
    ]
    args = [xd, xd, xd, conv_w, conv_b, wbd, bias, lam, h0]
    if combine:
        in_specs += [pl.BlockSpec((1, tile, w), tmap), pl.BlockSpec((1, tile, w), tmap)]
        args += [hf, gd]
    return pl.pallas_call(
        functools.partial(_rglru_kernel, tile=tile, n_tiles=n_tiles, reverse=reverse, combine=combine),
        grid=(b_, n_tiles),
        in_specs=in_specs,
        out_specs=[pl.BlockSpec((1, tile, w), tmap), pl.BlockSpec((1, 1, w), lambda b, i: (b, 0, 0))],
        out_shape=[jax.ShapeDtypeStruct((b_, n, w), F32), jax.ShapeDtypeStruct((b_, 1, w), F32)],
        scratch_shapes=[pltpu.VMEM((tile + 2 * HALO, w), F32), pltpu.VMEM((tile, w), F32),
                        pltpu.VMEM((tile, w), F32), pltpu.VMEM((SUBLANES, w), F32)],
        compiler_params=_params(2),
        name="rglru_bwd_combine" if combine else "rglru_fwd",
    )(*args)


def _block_diag(wz):
    eye = jnp.eye(D_BLOCKS, dtype=wz.dtype)
    return jnp.einsum('hij,hg->higj', wz, eye).reshape(D_WIDTH, D_WIDTH)


def rglru_pallas(xd, gd, conv_w, conv_b, w_r, b_r, w_i, b_i, lam, s0):
    cb = conv_b.reshape(1, D_WIDTH)
    outs = []
    for z in range(2):
        wbd = jnp.concatenate([_block_diag(w_r[z]), _block_diag(w_i[z])], axis=1).astype(BF16)
        bias = jnp.concatenate([b_r[z], b_i[z]]).reshape(1, 2 * D_WIDTH)
        outs.append((wbd, bias, lam[z].reshape(1, D_WIDTH)))
    hf, sf = _rglru_sweep(xd, conv_w, cb, *outs[0], s0[0][:, None, :], reverse=False)
    y, sb = _rglru_sweep(xd, conv_w, cb, *outs[1], s0[1][:, None, :], hf, gd, reverse=True)
    return y, (sf[:, 0], sb[:, 0])


AFF_BITS = 31


def _lane_cumsum_exclusive(x01):
    e_, n = x01.shape
    li = lax.broadcasted_iota(jnp.int32, (LANES, LANES), 0)
    lj = lax.broadcasted_iota(jnp.int32, (LANES, LANES), 1)
    strict_upper = (li < lj).astype(BF16)
    ones = jnp.ones((LANES, LANES), BF16)
    base = jnp.zeros((e_, LANES), F32)
    outs = []
    for g in range(n // LANES):
        xg = x01[:, g * LANES:(g + 1) * LANES].astype(BF16)
        outs.append(base + _dot(xg, strict_upper))
        base = base + _dot(xg, ones)
    return jnp.concatenate(outs, axis=1)


def _route_kernel(aff_ref, idx_ref, gate_ref, sp_s, af_s, ps_s, pe_s, *, cap):
    aff = aff_ref[0]
    e_, n = aff.shape
    n_groups = n // LANES
    rb_rows = min(cap, LANES)
    bits = pltpu.bitcast(aff, jnp.int32)
    thr = jnp.zeros((e_, 1), jnp.int32)
    for bit in range(AFF_BITS - 1, -1, -1):
        cand = thr | (1 << bit)
        cnt = jnp.sum((bits >= cand).astype(F32), axis=1, keepdims=True)
        thr = jnp.where(cnt >= cap, cand, thr)
    gt = bits > thr
    eq = bits == thr
    need = cap - jnp.sum(gt.astype(F32), axis=1, keepdims=True)
    sel = gt | (eq & (_lane_cumsum_exclusive(eq.astype(F32)) < need))
    pos = _lane_cumsum_exclusive(sel.astype(F32))
    selpos = jnp.where(sel, pos, -1.0)
    tok_g = lax.broadcasted_iota(jnp.int32, (n, n_groups), 0) // LANES
    grp = lax.broadcasted_iota(jnp.int32, (n, n_groups), 1)
    cnt_g = _dot(sel.astype(BF16), (tok_g == grp).astype(BF16))
    gi = lax.broadcasted_iota(jnp.int32, (n_groups, n_groups), 0)
    gj = lax.broadcasted_iota(jnp.int32, (n_groups, n_groups), 1)
    start_g = _dot(cnt_g, (gi < gj).astype(F32), HI)
    ps_s[...] = start_g
    pe_s[...] = start_g + cnt_g
    for g in range(n_groups):
        sp_s[g] = selpos[:, g * LANES:(g + 1) * LANES]
        af_s[g] = aff[:, g * LANES:(g + 1) * LANES]
    slot0 = lax.broadcasted_iota(jnp.int32, (rb_rows, LANES), 0).astype(F32)
    lane = lax.broadcasted_iota(jnp.int32, (1, LANES), 1).astype(F32)

    def per_block(k, carry):
        e = k // (cap // rb_rows)
        rb = k % (cap // rb_rows)
        first = lax.convert_element_type(rb * rb_rows, F32)
        slot = slot0 + first

        def per_group(g, acc):
            acc_i, acc_g = acc
            hit = sp_s[g, pl.ds(e, 1), :] == slot
            tok = lane + lax.convert_element_type(g * LANES, F32)
            return (acc_i + jnp.where(hit, tok, 0.0), acc_g + jnp.where(hit, af_s[g, pl.ds(e, 1), :], 0.0))

        g_lo = jnp.sum((pe_s[pl.ds(e, 1), :] <= first).astype(F32), axis=1, keepdims=True)[0, 0].astype(jnp.int32)
        g_hi = jnp.sum((ps_s[pl.ds(e, 1), :] < first + rb_rows).astype(F32), axis=1, keepdims=True)[0, 0].astype(jnp.int32)
        zero = jnp.zeros((rb_rows, LANES), F32)
        acc_i, acc_g = lax.fori_loop(g_lo, g_hi, per_group, (zero, zero))
        rows = pl.ds(pl.multiple_of(rb * rb_rows, rb_rows), rb_rows)
        idx_ref[0, e, rows, :] = jnp.sum(acc_i, axis=1, keepdims=True).astype(jnp.int32)
        gate_ref[0, e, rows, :] = jnp.sum(acc_g, axis=1, keepdims=True)
        return carry

    lax.fori_loop(0, e_ * (cap // rb_rows), per_block, 0)


def route(aff):
    b_, e_, n = aff.shape
    cap = max(1, EC_FACTOR * n // N_EXPERTS)
    grp_scr = lambda: pltpu.VMEM((n // LANES, e_, LANES), F32)
    cnt_scr = lambda: pltpu.VMEM((e_, n // LANES), F32)
    idx, gate = pl.pallas_call(
        functools.partial(_route_kernel, cap=cap),
        grid=(b_,),
        in_specs=[pl.BlockSpec((1, e_, n), lambda b: (b, 0, 0))],
        out_specs=[pl.BlockSpec((1, e_, cap, 1), lambda b: (b, 0, 0, 0)),
                   pl.BlockSpec((1, e_, cap, 1), lambda b: (b, 0, 0, 0))],
        out_shape=[jax.ShapeDtypeStruct((b_, e_, cap, 1), jnp.int32), jax.ShapeDtypeStruct((b_, e_, cap, 1), F32)],
        scratch_shapes=[grp_scr(), grp_scr(), cnt_scr(), cnt_scr()],
        compiler_params=_params(1),
        name="route",
    )(aff)
    return idx.reshape(b_, e_, 1, cap), gate


ROW_UNROLL = 8


def _gather_kernel(idx_ref, h_ref, o_ref, *, cap):
    def body(r, carry):
        o_ref[0, 0, pl.ds(r, 1), :] = h_ref[0, pl.ds(idx_ref[0, 0, 0, r], 1), :]
        return carry
    lax.fori_loop(0, cap, body, 0, unroll=ROW_UNROLL)


def gather_rows(h, idx):
    b_, n, d = h.shape
    _, e_, _, cap = idx.shape
    return pl.pallas_call(
        functools.partial(_gather_kernel, cap=cap),
        grid=(b_, e_),
        in_specs=[pl.BlockSpec((1, 1, 1, cap), lambda b, e: (b, e, 0, 0), memory_space=pltpu.SMEM),
                  pl.BlockSpec((1, n, d), lambda b, e: (b, 0, 0), pipeline_mode=pl.Buffered(1))],
        out_specs=pl.BlockSpec((1, 1, cap, d), lambda b, e: (b, e, 0, 0)),
        out_shape=jax.ShapeDtypeStruct((b_, e_, cap, d), h.dtype),
        compiler_params=_params(2),
        name="gather_rows",
    )(idx, h)


def _expert_ffn_kernel(x_ref, g_ref, g2_ref, w1_ref, w3_ref, w2_ref, o_ref, w1b, w3b, w2b):
    @pl.when((pl.program_id(1) == 0) & (pl.program_id(2) == 0))
    def _():
        w1b[...] = w1_ref[0].astype(BF16)
        w3b[...] = w3_ref[0].astype(BF16)
        w2b[...] = w2_ref[0].astype(BF16)

    x = _unpack_halves(x_ref[0, 0])
    hid = (_silu(_dot(x, w1b[...])) * _dot(x, w3b[...])).astype(BF16)
    o_ref[0, 0] = _dot(hid, w2b[...]) * g_ref[0, 0] * g2_ref[0]


def expert_ffn(xs, gate, g2, w1, w3, w2, layer):
    b_, e_, c_, half = xs.shape
    d = 2 * half
    f = w1.shape[-1]
    tm = min(c_, 512)
    return pl.pallas_call(
        _expert_ffn_kernel,
        grid=(e_, b_, c_ // tm),
        scratch_shapes=[pltpu.VMEM((d, f), BF16), pltpu.VMEM((d, f), BF16), pltpu.VMEM((f, d), BF16)],
        in_specs=[
            pl.BlockSpec((1, 1, tm, half), lambda e, b, m: (b, e, m, 0)),
            pl.BlockSpec((1, 1, tm, 1), lambda e, b, m: (b, e, m, 0)),
            pl.BlockSpec((1, 1, d), lambda e, b, m: (b, 0, 0)),
            pl.BlockSpec((None, 1, d, f), lambda e, b, m: (layer, e, 0, 0)),
            pl.BlockSpec((None, 1, d, f), lambda e, b, m: (layer, e, 0, 0)),
            pl.BlockSpec((None, 1, f, d), lambda e, b, m: (layer, e, 0, 0)),
        ],
        out_specs=pl.BlockSpec((1, 1, tm, d), lambda e, b, m: (b, e, m, 0)),
        out_shape=jax.ShapeDtypeStruct((b_, e_, c_, d), F32),
        compiler_params=_params(3),
        name="expert_ffn",
    )(xs, gate, g2[:, None, :], w1, w3, w2)


def _combine_kernel(idx_ref, y_ref, x1_hbm, o_hbm, acc, sem, *, cap, n_experts):
    b = pl.program_id(0)
    e = pl.program_id(1)

    @pl.when(e == 0)
    def _():
        cp = pltpu.make_async_copy(x1_hbm.at[b], acc, sem.at[0])
        cp.start()
        cp.wait()

    def body(g, carry):
        r0 = g * ROW_UNROLL
        rows = [pl.ds(idx_ref[0, 0, 0, r0 + j], 1) for j in range(ROW_UNROLL)]
        vals = [acc[row, :] + y_ref[0, 0, pl.ds(r0 + j, 1), :] for j, row in enumerate(rows)]
        for row, val in zip(rows, vals):
            acc[row, :] = val
        return carry
    lax.fori_loop(0, cap // ROW_UNROLL, body, 0)

    @pl.when(e == n_experts - 1)
    def _():
        cp = pltpu.make_async_copy(acc, o_hbm.at[b], sem.at[1])
        cp.start()
        cp.wait()


def combine_rows(x1, y, idx):
    b_, n, d = x1.shape
    _, e_, _, cap = idx.shape
    return pl.pallas_call(
        functools.partial(_combine_kernel, cap=cap, n_experts=e_),
        grid=(b_, e_),
        in_specs=[pl.BlockSpec((1, 1, 1, cap), lambda b, e: (b, e, 0, 0), memory_space=pltpu.SMEM),
                  pl.BlockSpec((1, 1, cap, d), lambda b, e: (b, e, 0, 0)),
                  pl.BlockSpec(memory_space=pl.ANY)],
        out_specs=pl.BlockSpec(memory_space=pl.ANY),
        out_shape=jax.ShapeDtypeStruct((b_, n, d), F32),
        scratch_shapes=[pltpu.VMEM((n, d), F32), pltpu.SemaphoreType.DMA((2,))],
        compiler_params=_params(2),
        name="combine_rows",
    )(idx, y, x1)


def moe_residual(x1, h2, aff, g2, w1, w3, w2, layer):
    idx, gate = route(aff)
    y = expert_ffn(gather_rows(h2, idx), gate, g2, w1, w3, w2, layer)
    return combine_rows(x1, y, idx)


def kernel(x, c, ctx, c_ctx, w_mod, b_mod, norm1_w, norm2_w, final_norm_w,
           ev_w_in, ev_w_out, a_conv_w, a_log, a_dt_bias, a_norm_w, b_lb_logits, b_norm_w,
           od_w_in, od_w_out, c_sink, d_conv_w, d_conv_b, d_w_r, d_b_r, d_w_i, d_b_i, d_lambda,
           moe_router, moe_w1, moe_w3, moe_w2):
    b_, n, d = x.shape
    cos, sin = _rope_tables(n)
    lb_all = jnp.cumsum(jax.nn.softmax(b_lb_logits.astype(F32), axis=0), axis=0)
    lb_all = lb_all - lb_all[0:1]
    c_rows = jnp.concatenate([c, c_ctx[None, :], jnp.zeros((-(b_ + 1) % SUBLANES, d), F32)], axis=0)
    mod_all = modulation(c_rows, w_mod, b_mod)
    for l in range(DEPTH):
        last = l == DEPTH - 1
        j = l // 2
        router_t = moe_router[l].T.astype(BF16)
        mod = mod_all[l, :b_]
        mod_c = jnp.broadcast_to(mod_all[l, b_], (b_, 6 * d))
        sh1, sc1, g1, sh2, sc2, g2 = jnp.split(mod, 6, axis=-1)
        csh1, csc1, cg1, csh2, csc2, cg2 = jnp.split(mod_c, 6, axis=-1)
        if l % 2 == 0:
            w_in, w_out = _even_w_in(ev_w_in[j]), ev_w_out[j].astype(BF16)
            zero = jnp.zeros((b_, A_HEADS, A_DK, A_DV), F32)
            pars = (a_conv_w[j], a_log[j], a_dt_bias[j], a_norm_w[j], lb_all[j].reshape(2, HB), b_norm_w[j])
            o_c, st = even_mixer_pallas(in_proj(ctx, norm1_w[l], csh1, csc1, w_in, EVEN_GROUPS), *pars,
                                        ((zero, zero), (zero, zero)))
            o_l, _ = even_mixer_pallas(in_proj(x, norm1_w[l], sh1, sc1, w_in, EVEN_GROUPS), *pars, st)
            parts_l, parts_c = [o_l], [o_c]
        else:
            w_in, w_out = od_w_in[j].astype(BF16), od_w_out[j].astype(BF16)
            qc, kc, vc, xdc, gdc = in_proj(ctx, norm1_w[l], csh1, csc1, w_in, ODD_SIZES)
            ql, kl, vl, xdl, gdl = in_proj(x, norm1_w[l], sh1, sc1, w_in, ODD_SIZES)
            att_l = window_attention_pallas(ql, kl, vl, kc, vc, c_sink[j], cos, sin)
            rg_pars = (d_conv_w[j], d_conv_b[j], d_w_r[j], d_b_r[j], d_w_i[j], d_b_i[j], d_lambda[j])
            zero = jnp.zeros((b_, D_WIDTH), F32)
            rg_c, st = rglru_pallas(xdc, gdc, *rg_pars, (zero, zero))
            rg_l, _ = rglru_pallas(xdl, gdl, *rg_pars, st)
            parts_l = [att_l, rg_l]
            if not last:
                parts_c = [context_attention_pallas(qc, kc, vc, c_sink[j]), rg_c]
        x1, h2, aff = post_mixer(parts_l, w_out, x, g1, norm2_w[l], sh2, sc2, router_t)
        x = moe_residual(x1, h2, aff, g2, moe_w1, moe_w3, moe_w2, l)
        if not last:
            c1, hc2, affc = post_mixer(parts_c, w_out, ctx, cg1, norm2_w[l], csh2, csc2, router_t)
            ctx = moe_residual(c1, hc2, affc, cg2, moe_w1, moe_w3, moe_w2, l)
    return final_norm(x, final_norm_w)
```

```python
import functools

import jax
import jax.numpy as jnp
from jax import lax
from jax.experimental import pallas as pl
from jax.experimental.pallas import tpu as pltpu

D_MODEL = 1024
DEPTH = 4
GRID_W = 64
EPS = 1e-6
F32 = jnp.float32
BF16 = jnp.bfloat16
HI = lax.Precision.HIGHEST

A_HEADS = 4
A_DK = 128
A_DV = 128
A_CONV = 4
A_CHUNK = 64
A_QKV = 2 * A_HEADS * A_DK + A_HEADS * A_DV
B_HEADS = 4
B_DK = 128
B_DV = 128
B_CHUNK = 64
C_HEADS = 8
C_KV_HEADS = 2
C_HD = 64
C_WIN = 128
C_BLOCK = 128
ROPE_THETA = 10000.0
D_WIDTH = 512
D_BLOCKS = 8
D_BW = D_WIDTH // D_BLOCKS
D_CONV = 4
D_C = 8.0
N_EXPERTS = 16
EXPERT_FF = 1024
EC_FACTOR = 2

EVEN_SIZES = (A_QKV, A_HEADS * A_DV, 2 * A_HEADS, 2 * A_HEADS,
              B_HEADS * B_DK, B_HEADS * B_DV, 2 * B_HEADS * B_DK, B_HEADS * B_DV)
ODD_SIZES = (C_HEADS * C_HD, C_KV_HEADS * C_HD, C_KV_HEADS * C_HD, D_WIDTH, D_WIDTH)

LANES = 128
SUBLANES = 8
VMEM_LIMIT_BYTES = 56 * 1024 * 1024

ROW_TILE = 512
HALO = SUBLANES


def _cuts(sizes):
    out, acc = [], 0
    for s in sizes[:-1]:
        acc += s
        out.append(acc)
    return out


def _params(n_axes):
    return pltpu.CompilerParams(dimension_semantics=("arbitrary",) * n_axes, vmem_limit_bytes=VMEM_LIMIT_BYTES)


def _dot(a, b, precision=None):
    return jnp.dot(a, b, preferred_element_type=F32, precision=precision)


def _dot_nt(a, b, precision=None):
    return lax.dot_general(a, b, (((1,), (1,)), ((), ())), preferred_element_type=F32, precision=precision)


def _dot_tn(a, b):
    return lax.dot_general(a, b, (((0,), (0,)), ((), ())), preferred_element_type=F32)


def _silu(x):
    return x * jax.nn.sigmoid(x)


def _rms(x):
    return x * lax.rsqrt(jnp.mean(x * x, axis=-1, keepdims=True) + EPS)


U32 = jnp.uint32
HIGH16 = 0xFFFF0000


def _pack_halves(h):
    half = h.shape[-1] // 2
    lo = pltpu.bitcast(h[:, :half].astype(BF16).astype(F32), U32)
    hi = pltpu.bitcast(h[:, half:].astype(BF16).astype(F32), U32)
    return (hi & jnp.uint32(HIGH16)) | lax.shift_right_logical(lo, jnp.uint32(16))


def _unpack_halves(p):
    lo = pltpu.bitcast(lax.shift_left(p, jnp.uint32(16)), F32).astype(BF16)
    hi = pltpu.bitcast(p & jnp.uint32(HIGH16), F32).astype(BF16)
    return jnp.concatenate([lo, hi], axis=1)


MOD_COLS = 1536


def _mod_kernel(c_ref, w_ref, b_ref, o_ref):
    o_ref[0] = _dot(_silu(c_ref[...]), w_ref[0]) + b_ref[0]


def modulation(c_rows, w_mod, b_mod):
    r, d = c_rows.shape
    depth, _, wide = w_mod.shape
    return pl.pallas_call(
        _mod_kernel,
        grid=(depth, wide // MOD_COLS),
        in_specs=[pl.BlockSpec((r, d), lambda l, j: (0, 0)),
                  pl.BlockSpec((1, d, MOD_COLS), lambda l, j: (l, 0, j)),
                  pl.BlockSpec((1, 1, MOD_COLS), lambda l, j: (l, 0, j))],
        out_specs=pl.BlockSpec((1, r, MOD_COLS), lambda l, j: (l, 0, j)),
        out_shape=jax.ShapeDtypeStruct((depth, r, wide), F32),
        compiler_params=_params(2),
        name="modulation",
    )(c_rows, w_mod, b_mod[:, None, :])


def _norm_mod(x, nw, shift, scale):
    return _rms(x) * nw * (1.0 + scale) + shift


def _in_proj_kernel(x_ref, nw_ref, sh_ref, sc_ref, w_ref, *o_refs, splits):
    h = _norm_mod(x_ref[0], nw_ref[...], sh_ref[0], sc_ref[0]).astype(BF16)
    off = 0
    for o_ref, s in zip(o_refs, splits):
        o_ref[0] = _dot(h, w_ref[:, off:off + s])
        off += s


def in_proj(x, norm_w, shift, scale, w, splits):
    b_, n, d = x.shape
    tm = min(n, ROW_TILE)
    ntot = sum(splits)
    vec = pl.BlockSpec((1, 1, d), lambda b, i: (b, 0, 0))
    return pl.pallas_call(
        functools.partial(_in_proj_kernel, splits=tuple(splits)),
        grid=(b_, n // tm),
        in_specs=[pl.BlockSpec((1, tm, d), lambda b, i: (b, i, 0)),
                  pl.BlockSpec((1, d), lambda b, i: (0, 0)), vec, vec,
                  pl.BlockSpec((d, ntot), lambda b, i: (0, 0))],
        out_specs=[pl.BlockSpec((1, tm, s), lambda b, i: (b, i, 0)) for s in splits],
        out_shape=[jax.ShapeDtypeStruct((b_, n, s), F32) for s in splits],
        compiler_params=_params(2),
        name="in_proj",
    )(x, norm_w.reshape(1, d), shift[:, None, :], scale[:, None, :], w)


def _post_mixer_kernel(*refs, n_parts):
    parts = refs[:n_parts]
    w_ref, x_ref, g1_ref, nw_ref, sh_ref, sc_ref, rt_ref, x1_ref, h2_ref, aff_ref = refs[n_parts:]
    y = None
    off = 0
    for p_ref in parts:
        k = p_ref.shape[-1]
        t = _dot(p_ref[0].astype(BF16), w_ref[off:off + k, :])
        y = t if y is None else y + t
        off += k
    x1 = x_ref[0] + g1_ref[0] * y
    x1_ref[0] = x1
    h2 = _norm_mod(x1, nw_ref[...], sh_ref[0], sc_ref[0])
    h2_ref[0] = _pack_halves(h2)
    logits = _dot_nt(rt_ref[...], h2.astype(BF16))
    e = jnp.exp(logits - jnp.max(logits, axis=0, keepdims=True))
    aff_ref[0] = e / jnp.sum(e, axis=0, keepdims=True)


def post_mixer(parts, w_out, x, g1, norm_w, shift, scale, router_t):
    b_, n, d = x.shape
    e_ = router_t.shape[0]
    tm = min(n, ROW_TILE)
    vec = pl.BlockSpec((1, 1, d), lambda b, i: (b, 0, 0))
    tok = lambda k: pl.BlockSpec((1, tm, k), lambda b, i: (b, i, 0))
    return pl.pallas_call(
        functools.partial(_post_mixer_kernel, n_parts=len(parts)),
        grid=(b_, n // tm),
        in_specs=[tok(p.shape[-1]) for p in parts] + [
            pl.BlockSpec(w_out.shape, lambda b, i: (0, 0)), tok(d), vec,
            pl.BlockSpec((1, d), lambda b, i: (0, 0)), vec, vec,
            pl.BlockSpec((e_, d), lambda b, i: (0, 0))],
        out_specs=[tok(d), tok(d // 2), pl.BlockSpec((1, e_, tm), lambda b, i: (b, 0, i))],
        out_shape=[jax.ShapeDtypeStruct((b_, n, d), F32), jax.ShapeDtypeStruct((b_, n, d // 2), U32),
                   jax.ShapeDtypeStruct((b_, e_, n), F32)],
        compiler_params=_params(2),
        name="post_mixer",
    )(*parts, w_out, x, g1[:, None, :], norm_w.reshape(1, d), shift[:, None, :], scale[:, None, :], router_t)


def _final_kernel(x_ref, nw_ref, o_ref):
    o_ref[0] = _rms(x_ref[0]) * nw_ref[...]


def final_norm(x, norm_w):
    b_, n, d = x.shape
    tm = min(n, 2 * ROW_TILE)
    tok = pl.BlockSpec((1, tm, d), lambda b, i: (b, i, 0))
    return pl.pallas_call(
        _final_kernel,
        grid=(b_, n // tm),
        in_specs=[tok, pl.BlockSpec((1, d), lambda b, i: (0, 0))],
        out_specs=tok,
        out_shape=jax.ShapeDtypeStruct((b_, n, d), F32),
        compiler_params=_params(2),
        name="final_norm",
    )(x, norm_w.reshape(1, d))


CHUNK = A_CHUNK
SUB = 16
EVEN_TILE = 256
HA = A_HEADS * A_DK
HB = B_HEADS * B_DK
AB_PAD = LANES
EVEN_GROUPS = (A_QKV, A_HEADS * A_DV, AB_PAD, HB, B_HEADS * B_DV, 2 * HB, B_HEADS * B_DV)
NEUMANN_STEPS = 5


def _split2(x):
    hi = x.astype(BF16)
    return hi, (x - hi.astype(F32)).astype(BF16)


def _dot_split(a, b):
    (ah, al), (bh, bl) = a, b
    return _dot(ah, bh) + _dot(ah, bl) + _dot(al, bh)


def _dot_exact_lhs(a_bf16, x):
    x1 = x.astype(BF16)
    r1 = x - x1.astype(F32)
    x2 = r1.astype(BF16)
    x3 = (r1 - x2.astype(F32)).astype(BF16)
    return _dot(a_bf16, x1) + _dot(a_bf16, x2) + _dot(a_bf16, x3)


def _even_kernel(*refs, tile, n_tiles, reverse, combine):
    if combine:
        (qkvn_ref, ab_ref, qb_ref, ib_ref, fb_ref, exp_ref, alog_ref, dtb_ref, lb_ref,
         sa0_ref, sb0_ref, of_ref, ga_ref, gb_ref, anw_ref, bnw_ref,
         o_ref, sa_ref, sb_ref, g_scr, st_a, st_b) = refs
    else:
        (qkv_ref, qkvp_ref, qkvnx_ref, ab_ref, qb_ref, ib_ref, fb_ref, cw_ref, exp_ref, alog_ref, dtb_ref, lb_ref,
         sa0_ref, sb0_ref,
         o_ref, qkvn_ref, sa_ref, sb_ref, xbuf, g_scr, st_a, st_b) = refs
    i = pl.program_id(1)
    t = (n_tiles - 1 - i) if reverse else i

    @pl.when(i == 0)
    def _():
        st_a[...] = sa0_ref[0]
        st_b[...] = sb0_ref[0]

    L = CHUNK
    n_chunks = tile // L
    n_blocks = tile // SUB
    chunk_order = range(n_chunks - 1, -1, -1) if reverse else range(n_chunks)
    block_order = range(n_blocks - 1, -1, -1) if reverse else range(n_blocks)

    ti = lax.broadcasted_iota(jnp.int32, (tile, tile), 0)
    tj = lax.broadcasted_iota(jnp.int32, (tile, tile), 1)
    t_incl = (tj >= ti) if reverse else (tj <= ti)
    tri_chunk = (t_incl & ((ti // L) == (tj // L))).astype(BF16)
    tri_blk = (t_incl & ((ti // SUB) == (tj // SUB))).astype(BF16)
    ones_blk = ((ti // SUB) == (tj // SUB)).astype(BF16)
    ii = lax.broadcasted_iota(jnp.int32, (L, L), 0)
    jj = lax.broadcasted_iota(jnp.int32, (L, L), 1)
    incl = (jj >= ii) if reverse else (jj <= ii)
    strict = (jj > ii) if reverse else (jj < ii)
    eye = (ii == jj).astype(F32)
    last_row = 0 if reverse else L - 1

    if not combine:
        xbuf[pl.ds(0, HALO), :] = jnp.where(t > 0, qkvp_ref[0], 0.0)
        xbuf[pl.ds(HALO, tile), :] = qkv_ref[0]
        xbuf[pl.ds(HALO + tile, HALO), :] = jnp.where(t < n_tiles - 1, qkvnx_ref[0], 0.0)
        conv = _silu(sum(cw_ref[pl.ds(k, 1), :] * xbuf[pl.ds(HALO - 1 + k, tile), :] for k in range(A_CONV)))
    ab_x = _dot(ab_ref[0], exp_ref[...], HI)
    g_all = -jnp.exp(alog_ref[...]) * jax.nn.softplus(ab_x[:, :HA] + dtb_ref[...])
    beta_all = jax.nn.sigmoid(ab_x[:, HA:])
    cum_all = _dot_exact_lhs(tri_chunk, g_all)
    ecum_all = jnp.exp(cum_all)

    chains = [(c, h) for c in chunk_order for h in range(A_HEADS)]
    pre = {}
    neg_m = []
    for c, h in chains:
        rs = slice(c * L, (c + 1) * L)
        hs = slice(h * A_DK, (h + 1) * A_DK)
        ks_, vs_ = slice(HA + h * A_DK, HA + (h + 1) * A_DK), slice(2 * HA + h * A_DV, 2 * HA + (h + 1) * A_DV)
        if combine:
            q, k, v = qkvn_ref[0, rs, hs], qkvn_ref[0, rs, ks_], qkvn_ref[0, rs, vs_]
        else:
            qh, kh, v = conv[rs, hs], conv[rs, ks_], conv[rs, vs_]
            q = qh * (lax.rsqrt(jnp.sum(qh * qh, axis=-1, keepdims=True) + EPS) * (A_DK ** -0.5))
            k = kh * lax.rsqrt(jnp.sum(kh * kh, axis=-1, keepdims=True) + EPS)
            qkvn_ref[0, rs, hs], qkvn_ref[0, rs, ks_], qkvn_ref[0, rs, vs_] = q, k, v
        beta, cum, ecum = beta_all[rs, hs], cum_all[rs, hs], ecum_all[rs, hs]
        r_ll = cum.T[:L, :]
        decay = jnp.where(incl, jnp.exp(jnp.where(incl, cum[:, :L] - r_ll, 0.0)), 0.0)
        kb = k * beta
        k16 = k.astype(BF16)
        neg_m.append(jnp.where(strict, -_dot_nt(kb.astype(BF16), k16) * decay, 0.0))
        qk = _dot_nt(q.astype(BF16), k16) * decay
        total = cum[last_row:last_row + 1, :]
        pre[(c, h)] = dict(rhs=jnp.concatenate([v * beta, kb * ecum], axis=1), qk=qk.astype(BF16),
                           qd=(q * ecum).astype(BF16), kd=(k * jnp.exp(total - cum)).astype(BF16),
                           gl=jnp.exp(total))
    p = [_split2(m) for m in neg_m]
    tinv = [eye + m for m in neg_m]
    for _ in range(NEUMANN_STEPS):
        p = [_split2(_dot_split(x, x)) for x in p]
        tinv = [tv + _dot_split(_split2(tv), x) for tv, x in zip(tinv, p)]
    sol_of = {ch: _dot_split(_split2(tv), _split2(pre[ch]["rhs"])) for tv, ch in zip(tinv, chains)}

    for c in chunk_order:
        rs = slice(c * L, (c + 1) * L)
        s_old = [st_a[h] for h in range(A_HEADS)]
        s16 = [s.astype(BF16) for s in s_old]
        ws = [_dot(sol_of[(c, h)][:, A_DV:].astype(BF16), s16[h]) for h in range(A_HEADS)]
        qs_ = [_dot(pre[(c, h)]["qd"], s16[h]) for h in range(A_HEADS)]
        v_new = [(sol_of[(c, h)][:, :A_DV] - ws[h]).astype(BF16) for h in range(A_HEADS)]
        for h in range(A_HEADS):
            o_ref[0, rs, h * A_DV:(h + 1) * A_DV] = qs_[h] + _dot(pre[(c, h)]["qk"], v_new[h])
            st_a[h] = s_old[h] * pre[(c, h)]["gl"] + _dot_tn(pre[(c, h)]["kd"], v_new[h])

    hq = _silu(qb_ref[0])
    hv = ib_ref[0]
    lb = lb_ref[...]
    fg = lb + (1.0 - lb) * jax.nn.sigmoid(fb_ref[0])
    lf = jnp.log(fg)
    hk = 1.0 - fg
    b_all = _dot_exact_lhs(tri_blk, lf)
    e_all = _dot_exact_lhs(ones_blk, lf)
    qs_all = (hq * jnp.exp(b_all)).astype(BF16)
    ks_all = (hk * jnp.exp(e_all - b_all)).astype(BF16)
    hv16 = hv.astype(BF16)
    row_in_blk = lax.broadcasted_iota(jnp.int32, (tile, HB), 0) % SUB
    od = [jnp.zeros((tile, B_DV), F32) for _ in range(B_HEADS)]
    for s_ in range(SUB):
        valid = (row_in_blk + s_ < SUB) if reverse else (row_in_blk >= s_)
        shift = ((tile - s_) if reverse else s_) % tile
        roll = (lambda x: x) if s_ == 0 else (lambda x: pltpu.roll(x, shift, axis=0))
        prod = jnp.where(valid, hq * roll(hk) * jnp.exp(b_all - roll(b_all)), 0.0)
        vv = roll(hv)
        for h in range(B_HEADS):
            hs = slice(h * B_DK, (h + 1) * B_DK)
            od[h] = od[h] + jnp.sum(prod[:, hs], axis=-1, keepdims=True) * vv[:, hs]
    for blk in block_order:
        bs = slice(blk * SUB, (blk + 1) * SUB)
        for h in range(B_HEADS):
            hs = slice(h * B_DK, (h + 1) * B_DK)
            g_scr[blk * B_HEADS + h] = _dot_tn(hv16[bs, hs], ks_all[bs, hs])

    st = [st_b[h] for h in range(B_HEADS)]
    for blk in block_order:
        bs = slice(blk * SUB, (blk + 1) * SUB)
        for h in range(B_HEADS):
            hs = slice(h * B_DK, (h + 1) * B_DK)
            o_ref[0, bs, HA + h * B_DV:HA + (h + 1) * B_DV] = _dot_nt(qs_all[bs, hs], st[h].astype(BF16)) + od[h][bs]
            st[h] = st[h] * jnp.exp(e_all[blk * SUB:blk * SUB + 1, hs]) + g_scr[blk * B_HEADS + h]
    for h in range(B_HEADS):
        st_b[h] = st[h]

    if combine:
        tot = of_ref[0] + o_ref[0]
        for h in range(A_HEADS + B_HEADS):
            hs = slice(h * A_DV, (h + 1) * A_DV)
            nw = anw_ref[...] if h < A_HEADS else bnw_ref[...]
            gate = ga_ref[0, :, hs] if h < A_HEADS else gb_ref[0, :, pl.ds((h - A_HEADS) * B_DV, B_DV)]
            o_ref[0, :, hs] = _rms(tot[:, hs]) * nw * _silu(gate)

    @pl.when(i == n_tiles - 1)
    def _():
        sa_ref[0] = st_a[...]
        sb_ref[0] = st_b[...]


def _even_sweep(outs, z, conv_w, a_log, dt_bias, lb, s_a, s_b, of=None, a_norm_w=None, b_norm_w=None):
    qkv, ga, ab, qb, ib, fb, gb = outs
    b_, n, _ = qkv.shape
    reverse = z == 1
    combine = of is not None
    tile = min(n, EVEN_TILE)
    n_tiles = n // tile
    bpt = tile // HALO
    nblk = n // HALO
    tidx = lambda i: (n_tiles - 1 - i) if reverse else i
    tmap = lambda b, i: (b, tidx(i), 0)
    pmap = lambda b, i: (b, jnp.maximum(tidx(i) * bpt - 1, 0), 0)
    nmap = lambda b, i: (b, jnp.minimum((tidx(i) + 1) * bpt, nblk - 1), 0)
    zmap = lambda b, i: (b, tidx(i), z)
    const2 = lambda b, i: (0, 0)
    smap = lambda b, i: (b, 0, 0, 0)
    src = lax.broadcasted_iota(jnp.int32, (AB_PAD, 2 * HA), 0)
    dst = lax.broadcasted_iota(jnp.int32, (AB_PAD, 2 * HA), 1)
    expand = (src == jnp.where(dst >= HA, 2 * A_HEADS, 0) + z * A_HEADS + (dst % HA) // A_DK).astype(F32)
    rep = lambda p_: jnp.broadcast_to(p_[:, None], (A_HEADS, A_DK)).reshape(1, HA)
    tok = lambda w: pl.BlockSpec((1, tile, w), tmap)
    common_specs = [tok(AB_PAD), tok(HB), tok(HB), pl.BlockSpec((1, tile, HB), zmap)]
    common_args = [ab, qb, ib, fb]
    par_specs = [pl.BlockSpec((AB_PAD, 2 * HA), const2),
                 pl.BlockSpec((1, HA), const2), pl.BlockSpec((1, HA), const2), pl.BlockSpec((1, HB), const2),
                 pl.BlockSpec((1, A_HEADS, A_DK, A_DV), smap), pl.BlockSpec((1, B_HEADS, B_DV, B_DK), smap)]
    par_args = [expand, rep(a_log[z]), rep(dt_bias[z]), lb[z].reshape(1, HB), s_a, s_b]
    state_specs = [pl.BlockSpec((1, A_HEADS, A_DK, A_DV), smap), pl.BlockSpec((1, B_HEADS, B_DV, B_DK), smap)]
    state_shapes = [jax.ShapeDtypeStruct(s_a.shape, F32), jax.ShapeDtypeStruct(s_b.shape, F32)]
    scratch = [pltpu.VMEM((tile // SUB * B_HEADS, B_DV, B_DK), F32),
               pltpu.VMEM((A_HEADS, A_DK, A_DV), F32), pltpu.VMEM((B_HEADS, B_DV, B_DK), F32)]
    if combine:
        in_specs = [tok(A_QKV)] + common_specs + par_specs + [
            tok(HA + HB), tok(HA), tok(HB), pl.BlockSpec((1, A_DV), const2), pl.BlockSpec((1, B_DV), const2)]
        args = [qkv] + common_args + par_args + [of, ga, gb, a_norm_w.reshape(1, A_DV), b_norm_w.reshape(1, B_DV)]
        out_specs = [tok(HA + HB)] + state_specs
        out_shape = [jax.ShapeDtypeStruct((b_, n, HA + HB), F32)] + state_shapes
    else:
        in_specs = [tok(A_QKV), pl.BlockSpec((1, HALO, A_QKV), pmap), pl.BlockSpec((1, HALO, A_QKV), nmap)] + \
            common_specs + [pl.BlockSpec((A_CONV, A_QKV), const2)] + par_specs
        args = [qkv, qkv, qkv] + common_args + [conv_w] + par_args
        out_specs = [tok(HA + HB), tok(A_QKV)] + state_specs
        out_shape = [jax.ShapeDtypeStruct((b_, n, HA + HB), F32), jax.ShapeDtypeStruct((b_, n, A_QKV), F32)] + state_shapes
        scratch = [pltpu.VMEM((tile + 2 * HALO, A_QKV), F32)] + scratch
    return pl.pallas_call(
        functools.partial(_even_kernel, tile=tile, n_tiles=n_tiles, reverse=reverse, combine=combine),
        grid=(b_, n_tiles),
        in_specs=in_specs,
        out_specs=out_specs,
        out_shape=out_shape,
        scratch_shapes=scratch,
        compiler_params=_params(2),
        name="even_bwd_combine" if combine else "even_fwd",
    )(*args)


def even_mixer_pallas(outs, conv_w, a_log, dt_bias, a_norm_w, lb, b_norm_w, states):
    (saf, sbf), (sab, sbb) = states
    o_f, qkv_n, saf, sbf = _even_sweep(outs, 0, conv_w, a_log, dt_bias, lb, saf, sbf)
    outs_b = [qkv_n] + list(outs[1:])
    o, sab, sbb = _even_sweep(outs_b, 1, conv_w, a_log, dt_bias, lb, sab, sbb, o_f, a_norm_w, b_norm_w)
    return o, ((saf, sbf), (sab, sbb))


def _even_w_in(w):
    c = _cuts(EVEN_SIZES)
    ab = jnp.pad(w[:, c[1]:c[3]], ((0, 0), (0, AB_PAD - (c[3] - c[1]))))
    return jnp.concatenate([w[:, :c[1]], ab, w[:, c[3]:]], axis=1).astype(BF16)


ROPE_NF = C_HD // 4
NEG_BIG = -1e30


def _rope(t, cos, sin_signed):
    w = t.shape[-1]
    lane = lax.broadcasted_iota(jnp.int32, t.shape, 1)
    partner = jnp.where(lane % (2 * ROPE_NF) < ROPE_NF,
                        pltpu.roll(t, w - ROPE_NF, axis=1), pltpu.roll(t, ROPE_NF, axis=1))
    return t * cos + partner * sin_signed


def _dup_kv_head(t, kh):
    lane = lax.broadcasted_iota(jnp.int32, t.shape, 1)
    rolled = pltpu.roll(t, C_HD, axis=1)
    own_half = (lane < C_HD) if kh == 0 else (lane >= C_HD)
    return jnp.where(own_half, t, rolled)


def _attn_kernel(*refs, banded, n_blocks):
    if banded:
        (q_ref, kp_ref, k_ref, kn_ref, vp_ref, v_ref, vn_ref, cq_ref, sq_ref, cp_ref, sp_ref, cn_ref, sn_ref,
         kc_ref, vc_ref, sink_ref, o_ref) = refs
    else:
        q_ref, kc_ref, vc_ref, sink_ref, o_ref = refs
    nb = pl.program_id(1)
    blk = q_ref.shape[1]
    q = q_ref[0] * (C_HD ** -0.5)
    kc = kc_ref[0]
    vc = vc_ref[0]
    if banded:
        cq, sq = cq_ref[...], sq_ref[...]
        q = _rope(q, jnp.concatenate([cq] * (C_HEADS // C_KV_HEADS), axis=1),
                  jnp.concatenate([sq] * (C_HEADS // C_KV_HEADS), axis=1))
        kb = [_rope(kp_ref[0], cp_ref[...], sp_ref[...]), _rope(k_ref[0], cq, sq), _rope(kn_ref[0], cn_ref[...], sn_ref[...])]
        keys = jnp.concatenate(kb + [kc], axis=0)
        vals = jnp.concatenate([vp_ref[0], v_ref[0], vn_ref[0], vc], axis=0)
        qpos = lax.broadcasted_iota(jnp.int32, (blk, keys.shape[0]), 0)
        kcol = lax.broadcasted_iota(jnp.int32, (blk, keys.shape[0]), 1)
        rel = kcol - blk
        in_band = (jnp.abs(rel - qpos) <= C_WIN) & (rel + nb * blk >= 0) & (rel + nb * blk < n_blocks * blk)
        mask = in_band | (kcol >= 3 * blk)
    else:
        keys, vals, mask = kc, vc, None
    keys = keys.astype(BF16)
    vals = vals.astype(BF16)
    lane_q = lax.broadcasted_iota(jnp.int32, (blk, 2 * C_HD), 1)
    grp = C_HEADS // C_KV_HEADS
    heads = range(C_HEADS)
    k_dup = [_dup_kv_head(keys, kh) for kh in range(C_KV_HEADS)]
    v_dup = [_dup_kv_head(vals, kh) for kh in range(C_KV_HEADS)]
    qh = [jnp.where((lane_q < C_HD) if h % 2 == 0 else (lane_q >= C_HD),
                    q[:, (h // 2) * 2 * C_HD:(h // 2 + 1) * 2 * C_HD], 0.0).astype(BF16) for h in heads]
    s = [_dot_nt(qh[h], k_dup[h // grp]) for h in heads]
    if mask is not None:
        s = [jnp.where(mask, x, NEG_BIG) for x in s]
    m = [jnp.maximum(jnp.max(s[h], axis=-1, keepdims=True), sink_ref[h]) for h in heads]
    e = [jnp.exp(s[h] - m[h]) for h in heads]
    denom = [jnp.sum(e[h], axis=-1, keepdims=True) + jnp.exp(sink_ref[h] - m[h]) for h in heads]
    p = [(e[h] / denom[h]).astype(BF16) for h in heads]
    out = [_dot(p[h], v_dup[h // grp]) for h in heads]
    for pair in range(C_HEADS // 2):
        o_ref[0, :, pair * 2 * C_HD:(pair + 1) * 2 * C_HD] = jnp.where(lane_q < C_HD, out[2 * pair], out[2 * pair + 1])


def _rope_tables(n):
    tok = lax.iota(jnp.int32, n)
    row = (tok // GRID_W).astype(F32)
    col = (tok % GRID_W).astype(F32)
    inv = ROPE_THETA ** (-lax.iota(F32, ROPE_NF) / ROPE_NF)
    ang_r, ang_c = row[:, None] * inv, col[:, None] * inv
    cos = jnp.concatenate([jnp.cos(ang_r)] * 2 + [jnp.cos(ang_c)] * 2, axis=-1)
    sin = jnp.concatenate([-jnp.sin(ang_r), jnp.sin(ang_r), -jnp.sin(ang_c), jnp.sin(ang_c)], axis=-1)
    return jnp.concatenate([cos, cos], axis=-1), jnp.concatenate([sin, sin], axis=-1)


def window_attention_pallas(q, k, v, kc, vc, sink, cos, sin):
    b_, n, hq = q.shape
    hk = k.shape[-1]
    lc = kc.shape[1]
    blk = C_BLOCK
    nb_ = n // blk
    cur = lambda b, i: (b, i, 0)
    prv = lambda b, i: (b, jnp.maximum(i - 1, 0), 0)
    nxt = lambda b, i: (b, jnp.minimum(i + 1, nb_ - 1), 0)
    tcur = lambda b, i: (i, 0)
    tprv = lambda b, i: (jnp.maximum(i - 1, 0), 0)
    tnxt = lambda b, i: (jnp.minimum(i + 1, nb_ - 1), 0)
    kspec = lambda m: pl.BlockSpec((1, blk, hk), m)
    tspec = lambda m: pl.BlockSpec((blk, hk), m)
    cspec = pl.BlockSpec((1, lc, hk), lambda b, i: (b, 0, 0))
    return pl.pallas_call(
        functools.partial(_attn_kernel, banded=True, n_blocks=nb_),
        grid=(b_, nb_),
        in_specs=[pl.BlockSpec((1, blk, hq), cur), kspec(prv), kspec(cur), kspec(nxt),
                  kspec(prv), kspec(cur), kspec(nxt),
                  tspec(tcur), tspec(tcur), tspec(tprv), tspec(tprv), tspec(tnxt), tspec(tnxt),
                  cspec, cspec, pl.BlockSpec(memory_space=pltpu.SMEM)],
        out_specs=pl.BlockSpec((1, blk, hq), cur),
        out_shape=jax.ShapeDtypeStruct((b_, n, hq), F32),
        compiler_params=_params(2),
        name="window_attention",
    )(q, k, k, k, v, v, v, cos, sin, cos, sin, cos, sin, kc, vc, sink)


def context_attention_pallas(q, kc, vc, sink):
    b_, lc, hq = q.shape
    hk = kc.shape[-1]
    blk = min(lc, C_BLOCK)
    cspec = pl.BlockSpec((1, lc, hk), lambda b, i: (b, 0, 0))
    return pl.pallas_call(
        functools.partial(_attn_kernel, banded=False, n_blocks=lc // blk),
        grid=(b_, lc // blk),
        in_specs=[pl.BlockSpec((1, blk, hq), lambda b, i: (b, i, 0)), cspec, cspec,
                  pl.BlockSpec(memory_space=pltpu.SMEM)],
        out_specs=pl.BlockSpec((1, blk, hq), lambda b, i: (b, i, 0)),
        out_shape=jax.ShapeDtypeStruct((b_, lc, hq), F32),
        compiler_params=_params(2),
        name="context_attention",
    )(q, kc, vc, sink)


def _scan8(a, u, reverse):
    row = lax.broadcasted_iota(jnp.int32, a.shape, 0)
    for s in (1, 2, 4):
        shift = (SUBLANES - s) if reverse else s
        a_sh = pltpu.roll(a, shift, axis=0)
        u_sh = pltpu.roll(u, shift, axis=0)
        valid = (row < SUBLANES - s) if reverse else (row >= s)
        u = jnp.where(valid, a * u_sh + u, u)
        a = jnp.where(valid, a * a_sh, a)
    return a, u


def _gelu_tanh(x):
    return 0.5 * x * (1.0 + jnp.tanh(0.7978845608028654 * (x + 0.044715 * (x * x * x))))


def _rglru_kernel(*refs, tile, n_tiles, reverse, combine):
    if combine:
        (x_ref, xp_ref, xn_ref, cw_ref, cb_ref, wbd_ref, bias_ref, lam_ref, h0_ref, hf_ref, gd_ref,
         o_ref, hl_ref, xbuf, a_s, u_s, carry) = refs
    else:
        (x_ref, xp_ref, xn_ref, cw_ref, cb_ref, wbd_ref, bias_ref, lam_ref, h0_ref,
         o_ref, hl_ref, xbuf, a_s, u_s, carry) = refs
    i = pl.program_id(1)
    t = (n_tiles - 1 - i) if reverse else i
    w = x_ref.shape[-1]

    @pl.when(i == 0)
    def _():
        carry[...] = jnp.broadcast_to(h0_ref[0], (SUBLANES, w))

    xbuf[pl.ds(0, HALO), :] = jnp.where(t > 0, xp_ref[0], 0.0)
    xbuf[pl.ds(HALO, tile), :] = x_ref[0]
    xbuf[pl.ds(HALO + tile, HALO), :] = jnp.where(t < n_tiles - 1, xn_ref[0], 0.0)
    xc = cb_ref[...] + sum(cw_ref[pl.ds(k, 1), :] * xbuf[pl.ds(HALO - 1 + k, tile), :] for k in range(D_CONV))

    z = _dot(xc.astype(BF16), wbd_ref[...]) + bias_ref[...]
    r = jax.nn.sigmoid(z[:, :w])
    gi = jax.nn.sigmoid(z[:, w:])
    lam = lam_ref[...]
    log_a = (-D_C * jnp.log(1.0 + jnp.exp(-lam))) * r
    a = jnp.exp(log_a)
    a_s[...] = a
    u_s[...] = jnp.sqrt(1.0 - a * a) * gi * xc

    n_groups = tile // SUBLANES

    def step(g, c):
        gg = (n_groups - 1 - g) if reverse else g
        r0 = pl.multiple_of(gg * SUBLANES, SUBLANES)
        ac, uc = _scan8(a_s[pl.ds(r0, SUBLANES), :], u_s[pl.ds(r0, SUBLANES), :], reverse)
        h = ac * c + uc
        o_ref[0, pl.ds(r0, SUBLANES), :] = h
        last = h[0:1, :] if reverse else h[SUBLANES - 1:SUBLANES, :]
        return jnp.broadcast_to(last, (SUBLANES, w))

    c_fin = lax.fori_loop(0, n_groups, step, carry[...], unroll=4 if n_groups % 4 == 0 else 1)
    carry[...] = c_fin

    if combine:
        o_ref[0] = (hf_ref[0] + o_ref[0]) * _gelu_tanh(gd_ref[0])

    @pl.when(i == n_tiles - 1)
    def _():
        hl_ref[0] = c_fin[0:1, :]


def _rglru_sweep(xd, conv_w, conv_b, wbd, bias, lam, h0, hf=None, gd=None, *, reverse):
    b_, n, w = xd.shape
    combine = hf is not None
    tile = min(n, 512)
    n_tiles = n // tile
    blocks_per_tile = tile // HALO
    n_blocks = n // HALO

    def tmap(b, i):
        return (b, (n_tiles - 1 - i) if reverse else i, 0)

    def pmap(b, i):
        t = (n_tiles - 1 - i) if reverse else i
        return (b, jnp.maximum(t * blocks_per_tile - 1, 0), 0)

    def nmap(b, i):
        t = (n_tiles - 1 - i) if reverse else i
        return (b, jnp.minimum((t + 1) * blocks_per_tile, n_blocks - 1), 0)

    const2 = lambda b, i: (0, 0)
    in_specs = [
        pl.BlockSpec((1, tile, w), tmap),
        pl.BlockSpec((1, HALO, w), pmap),
        pl.BlockSpec((1, HALO, w), nmap),
        pl.BlockSpec((D_CONV, w), const2),
        pl.BlockSpec((1, w), const2),
        pl.BlockSpec((w, 2 * w), const2),
        pl.BlockSpec((1, 2 * w), const2),
        pl.BlockSpec((1, w), const2),
        pl.BlockSpec((1, 1, w), lambda b, i: (b, 0, 0)),
    ]
    args = [xd, xd, xd, conv_w, conv_b, wbd, bias, lam, h0]
    if combine:
        in_specs += [pl.BlockSpec((1, tile, w), tmap), pl.BlockSpec((1, tile, w), tmap)]
        args += [hf, gd]
    return pl.pallas_call(
        functools.partial(_rglru_kernel, tile=tile, n_tiles=n_tiles, reverse=reverse, combine=combine),
        grid=(b_, n_tiles),
        in_specs=in_specs,
        out_specs=[pl.BlockSpec((1, tile, w), tmap), pl.BlockSpec((1, 1, w), lambda b, i: (b, 0, 0))],
        out_shape=[jax.ShapeDtypeStruct((b_, n, w), F32), jax.ShapeDtypeStruct((b_, 1, w), F32)],
        scratch_shapes=[pltpu.VMEM((tile + 2 * HALO, w), F32), pltpu.VMEM((tile, w), F32),
                        pltpu.VMEM((tile, w), F32), pltpu.VMEM((SUBLANES, w), F32)],
        compiler_params=_params(2),
        name="rglru_bwd_combine" if combine else "rglru_fwd",
    )(*args)


def _block_diag(wz):
    eye = jnp.eye(D_BLOCKS, dtype=wz.dtype)
    return jnp.einsum('hij,hg->higj', wz, eye).reshape(D_WIDTH, D_WIDTH)


def rglru_pallas(xd, gd, conv_w, conv_b, w_r, b_r, w_i, b_i, lam, s0):
    cb = conv_b.reshape(1, D_WIDTH)
    outs = []
    for z in range(2):
        wbd = jnp.concatenate([_block_diag(w_r[z]), _block_diag(w_i[z])], axis=1).astype(BF16)
        bias = jnp.concatenate([b_r[z], b_i[z]]).reshape(1, 2 * D_WIDTH)
        outs.append((wbd, bias, lam[z].reshape(1, D_WIDTH)))
    hf, sf = _rglru_sweep(xd, conv_w, cb, *outs[0], s0[0][:, None, :], reverse=False)
    y, sb = _rglru_sweep(xd, conv_w, cb, *outs[1], s0[1][:, None, :], hf, gd, reverse=True)
    return y, (sf[:, 0], sb[:, 0])


AFF_BITS = 31


def _lane_cumsum_exclusive(x01):
    e_, n = x01.shape
    li = lax.broadcasted_iota(jnp.int32, (LANES, LANES), 0)
    lj = lax.broadcasted_iota(jnp.int32, (LANES, LANES), 1)
    strict_upper = (li < lj).astype(BF16)
    ones = jnp.ones((LANES, LANES), BF16)
    base = jnp.zeros((e_, LANES), F32)
    outs = []
    for g in range(n // LANES):
        xg = x01[:, g * LANES:(g + 1) * LANES].astype(BF16)
        outs.append(base + _dot(xg, strict_upper))
        base = base + _dot(xg, ones)
    return jnp.concatenate(outs, axis=1)


def _route_kernel(aff_ref, idx_ref, gate_ref, sp_s, af_s, ps_s, pe_s, *, cap):
    aff = aff_ref[0]
    e_, n = aff.shape
    n_groups = n // LANES
    rb_rows = min(cap, LANES)
    bits = pltpu.bitcast(aff, jnp.int32)
    thr = jnp.zeros((e_, 1), jnp.int32)
    for bit in range(AFF_BITS - 1, -1, -1):
        cand = thr | (1 << bit)
        cnt = jnp.sum((bits >= cand).astype(F32), axis=1, keepdims=True)
        thr = jnp.where(cnt >= cap, cand, thr)
    gt = bits > thr
    eq = bits == thr
    need = cap - jnp.sum(gt.astype(F32), axis=1, keepdims=True)
    sel = gt | (eq & (_lane_cumsum_exclusive(eq.astype(F32)) < need))
    pos = _lane_cumsum_exclusive(sel.astype(F32))
    selpos = jnp.where(sel, pos, -1.0)
    tok_g = lax.broadcasted_iota(jnp.int32, (n, n_groups), 0) // LANES
    grp = lax.broadcasted_iota(jnp.int32, (n, n_groups), 1)
    cnt_g = _dot(sel.astype(BF16), (tok_g == grp).astype(BF16))
    gi = lax.broadcasted_iota(jnp.int32, (n_groups, n_groups), 0)
    gj = lax.broadcasted_iota(jnp.int32, (n_groups, n_groups), 1)
    start_g = _dot(cnt_g, (gi < gj).astype(F32), HI)
    ps_s[...] = start_g
    pe_s[...] = start_g + cnt_g
    for g in range(n_groups):
        sp_s[g] = selpos[:, g * LANES:(g + 1) * LANES]
        af_s[g] = aff[:, g * LANES:(g + 1) * LANES]
    slot0 = lax.broadcasted_iota(jnp.int32, (rb_rows, LANES), 0).astype(F32)
    lane = lax.broadcasted_iota(jnp.int32, (1, LANES), 1).astype(F32)

    def per_block(k, carry):
        e = k // (cap // rb_rows)
        rb = k % (cap // rb_rows)
        first = lax.convert_element_type(rb * rb_rows, F32)
        slot = slot0 + first

        def per_group(g, acc):
            acc_i, acc_g = acc
            hit = sp_s[g, pl.ds(e, 1), :] == slot
            tok = lane + lax.convert_element_type(g * LANES, F32)
            return (acc_i + jnp.where(hit, tok, 0.0), acc_g + jnp.where(hit, af_s[g, pl.ds(e, 1), :], 0.0))

        g_lo = jnp.sum((pe_s[pl.ds(e, 1), :] <= first).astype(F32), axis=1, keepdims=True)[0, 0].astype(jnp.int32)
        g_hi = jnp.sum((ps_s[pl.ds(e, 1), :] < first + rb_rows).astype(F32), axis=1, keepdims=True)[0, 0].astype(jnp.int32)
        zero = jnp.zeros((rb_rows, LANES), F32)
        acc_i, acc_g = lax.fori_loop(g_lo, g_hi, per_group, (zero, zero))
        rows = pl.ds(pl.multiple_of(rb * rb_rows, rb_rows), rb_rows)
        idx_ref[0, e, rows, :] = jnp.sum(acc_i, axis=1, keepdims=True).astype(jnp.int32)
        gate_ref[0, e, rows, :] = jnp.sum(acc_g, axis=1, keepdims=True)
        return carry

    lax.fori_loop(0, e_ * (cap // rb_rows), per_block, 0)


def route(aff):
    b_, e_, n = aff.shape
    cap = max(1, EC_FACTOR * n // N_EXPERTS)
    grp_scr = lambda: pltpu.VMEM((n // LANES, e_, LANES), F32)
    cnt_scr = lambda: pltpu.VMEM((e_, n // LANES), F32)
    idx, gate = pl.pallas_call(
        functools.partial(_route_kernel, cap=cap),
        grid=(b_,),
        in_specs=[pl.BlockSpec((1, e_, n), lambda b: (b, 0, 0))],
        out_specs=[pl.BlockSpec((1, e_, cap, 1), lambda b: (b, 0, 0, 0)),
                   pl.BlockSpec((1, e_, cap, 1), lambda b: (b, 0, 0, 0))],
        out_shape=[jax.ShapeDtypeStruct((b_, e_, cap, 1), jnp.int32), jax.ShapeDtypeStruct((b_, e_, cap, 1), F32)],
        scratch_shapes=[grp_scr(), grp_scr(), cnt_scr(), cnt_scr()],
        compiler_params=_params(1),
        name="route",
    )(aff)
    return idx.reshape(b_, e_, 1, cap), gate


ROW_UNROLL = 8


def _gather_kernel(idx_ref, h_ref, o_ref, *, cap):
    def body(r, carry):
        o_ref[0, 0, pl.ds(r, 1), :] = h_ref[0, pl.ds(idx_ref[0, 0, 0, r], 1), :]
        return carry
    lax.fori_loop(0, cap, body, 0, unroll=ROW_UNROLL)


def gather_rows(h, idx):
    b_, n, d = h.shape
    _, e_, _, cap = idx.shape
    return pl.pallas_call(
        functools.partial(_gather_kernel, cap=cap),
        grid=(b_, e_),
        in_specs=[pl.BlockSpec((1, 1, 1, cap), lambda b, e: (b, e, 0, 0), memory_space=pltpu.SMEM),
                  pl.BlockSpec((1, n, d), lambda b, e: (b, 0, 0), pipeline_mode=pl.Buffered(1))],
        out_specs=pl.BlockSpec((1, 1, cap, d), lambda b, e: (b, e, 0, 0)),
        out_shape=jax.ShapeDtypeStruct((b_, e_, cap, d), h.dtype),
        compiler_params=_params(2),
        name="gather_rows",
    )(idx, h)


def _expert_ffn_kernel(*refs, n_lat, with_ctx):
    if with_ctx:
        (x_ref, g_ref, g2_ref, xc_ref, gc_ref, g2c_ref, w1_ref, w3_ref, w2_ref, o_ref, oc_ref, w1b, w3b, w2b) = refs
    else:
        (x_ref, g_ref, g2_ref, w1_ref, w3_ref, w2_ref, o_ref, w1b, w3b, w2b) = refs
    m = pl.program_id(2)

    @pl.when((pl.program_id(1) == 0) & (m == 0))
    def _():
        w1b[...] = w1_ref[0].astype(BF16)
        w3b[...] = w3_ref[0].astype(BF16)
        w2b[...] = w2_ref[0].astype(BF16)

    def ffn(xp_ref, gate_ref, mod_ref, out_ref):
        x = _unpack_halves(xp_ref[0, 0])
        hid = (_silu(_dot(x, w1b[...])) * _dot(x, w3b[...])).astype(BF16)
        out_ref[0, 0] = _dot(hid, w2b[...]) * gate_ref[0, 0] * mod_ref[0]

    if with_ctx:
        pl.when(m < n_lat)(lambda: ffn(x_ref, g_ref, g2_ref, o_ref))
        pl.when(m == n_lat)(lambda: ffn(xc_ref, gc_ref, g2c_ref, oc_ref))
    else:
        ffn(x_ref, g_ref, g2_ref, o_ref)


def expert_ffn(xs, gate, g2, w1, w3, w2, layer, ctx=None):
    b_, e_, c_, half = xs.shape
    d = 2 * half
    f = w1.shape[-1]
    tm = min(c_, 512)
    n_lat = c_ // tm
    lat = lambda e, b, m: (b, e, jnp.minimum(m, n_lat - 1), 0)
    vec = pl.BlockSpec((1, 1, d), lambda e, b, m: (b, 0, 0))
    in_specs = [pl.BlockSpec((1, 1, tm, half), lat), pl.BlockSpec((1, 1, tm, 1), lat), vec]
    args = [xs, gate, g2[:, None, :]]
    out_specs = [pl.BlockSpec((1, 1, tm, d), lat)]
    out_shape = [jax.ShapeDtypeStruct((b_, e_, c_, d), F32)]
    if ctx is not None:
        xs_c, gate_c, g2_c = ctx
        cc = xs_c.shape[2]
        one = lambda e, b, m: (b, e, 0, 0)
        in_specs += [pl.BlockSpec((1, 1, cc, half), one), pl.BlockSpec((1, 1, cc, 1), one), vec]
        args += [xs_c, gate_c, g2_c[:, None, :]]
        out_specs.append(pl.BlockSpec((1, 1, cc, d), one))
        out_shape.append(jax.ShapeDtypeStruct((b_, e_, cc, d), F32))
    wspec = lambda r, c: pl.BlockSpec((None, 1, r, c), lambda e, b, m: (layer, e, 0, 0))
    outs = pl.pallas_call(
        functools.partial(_expert_ffn_kernel, n_lat=n_lat, with_ctx=ctx is not None),
        grid=(e_, b_, n_lat + (ctx is not None)),
        scratch_shapes=[pltpu.VMEM((d, f), BF16), pltpu.VMEM((d, f), BF16), pltpu.VMEM((f, d), BF16)],
        in_specs=in_specs + [wspec(d, f), wspec(d, f), wspec(f, d)],
        out_specs=out_specs,
        out_shape=out_shape,
        compiler_params=_params(3),
        name="expert_ffn",
    )(*args, w1, w3, w2)
    return (outs[0], outs[1]) if ctx is not None else (outs[0], None)


def _combine_kernel(idx_ref, y_ref, x1_hbm, o_hbm, acc, sem, *, cap, n_experts):
    b = pl.program_id(0)
    e = pl.program_id(1)

    @pl.when(e == 0)
    def _():
        cp = pltpu.make_async_copy(x1_hbm.at[b], acc, sem.at[0])
        cp.start()
        cp.wait()

    def body(g, carry):
        r0 = g * ROW_UNROLL
        rows = [pl.ds(idx_ref[0, 0, 0, r0 + j], 1) for j in range(ROW_UNROLL)]
        vals = [acc[row, :] + y_ref[0, 0, pl.ds(r0 + j, 1), :] for j, row in enumerate(rows)]
        for row, val in zip(rows, vals):
            acc[row, :] = val
        return carry
    lax.fori_loop(0, cap // ROW_UNROLL, body, 0)

    @pl.when(e == n_experts - 1)
    def _():
        cp = pltpu.make_async_copy(acc, o_hbm.at[b], sem.at[1])
        cp.start()
        cp.wait()


def combine_rows(x1, y, idx):
    b_, n, d = x1.shape
    _, e_, _, cap = idx.shape
    return pl.pallas_call(
        functools.partial(_combine_kernel, cap=cap, n_experts=e_),
        grid=(b_, e_),
        in_specs=[pl.BlockSpec((1, 1, 1, cap), lambda b, e: (b, e, 0, 0), memory_space=pltpu.SMEM),
                  pl.BlockSpec((1, 1, cap, d), lambda b, e: (b, e, 0, 0)),
                  pl.BlockSpec(memory_space=pl.ANY)],
        out_specs=pl.BlockSpec(memory_space=pl.ANY),
        out_shape=jax.ShapeDtypeStruct((b_, n, d), F32),
        scratch_shapes=[pltpu.VMEM((n, d), F32), pltpu.SemaphoreType.DMA((2,))],
        compiler_params=_params(2),
        name="combine_rows",
    )(idx, y, x1)


def moe_residual(lat, ctx, w1, w3, w2, layer):
    routed = []
    for x1, h2, aff, g2 in (lat,) if ctx is None else (lat, ctx):
        idx, gate = route(aff)
        routed.append((x1, idx, gather_rows(h2, idx), gate, g2))
    (x1, idx, xs, gate, g2) = routed[0]
    y, y_c = expert_ffn(xs, gate, g2, w1, w3, w2, layer, ctx=None if ctx is None else routed[1][2:])
    out = combine_rows(x1, y, idx)
    return out, (None if ctx is None else combine_rows(routed[1][0], y_c, routed[1][1]))


def kernel(x, c, ctx, c_ctx, w_mod, b_mod, norm1_w, norm2_w, final_norm_w,
           ev_w_in, ev_w_out, a_conv_w, a_log, a_dt_bias, a_norm_w, b_lb_logits, b_norm_w,
           od_w_in, od_w_out, c_sink, d_conv_w, d_conv_b, d_w_r, d_b_r, d_w_i, d_b_i, d_lambda,
           moe_router, moe_w1, moe_w3, moe_w2):
    b_, n, d = x.shape
    cos, sin = _rope_tables(n)
    lb_all = jnp.cumsum(jax.nn.softmax(b_lb_logits.astype(F32), axis=0), axis=0)
    lb_all = lb_all - lb_all[0:1]
    c_rows = jnp.concatenate([c, c_ctx[None, :], jnp.zeros((-(b_ + 1) % SUBLANES, d), F32)], axis=0)
    mod_all = modulation(c_rows, w_mod, b_mod)
    for l in range(DEPTH):
        last = l == DEPTH - 1
        j = l // 2
        router_t = moe_router[l].T.astype(BF16)
        mod = mod_all[l, :b_]
        mod_c = jnp.broadcast_to(mod_all[l, b_], (b_, 6 * d))
        sh1, sc1, g1, sh2, sc2, g2 = jnp.split(mod, 6, axis=-1)
        csh1, csc1, cg1, csh2, csc2, cg2 = jnp.split(mod_c, 6, axis=-1)
        if l % 2 == 0:
            w_in, w_out = _even_w_in(ev_w_in[j]), ev_w_out[j].astype(BF16)
            zero = jnp.zeros((b_, A_HEADS, A_DK, A_DV), F32)
            pars = (a_conv_w[j], a_log[j], a_dt_bias[j], a_norm_w[j], lb_all[j].reshape(2, HB), b_norm_w[j])
            o_c, st = even_mixer_pallas(in_proj(ctx, norm1_w[l], csh1, csc1, w_in, EVEN_GROUPS), *pars,
                                        ((zero, zero), (zero, zero)))
            o_l, _ = even_mixer_pallas(in_proj(x, norm1_w[l], sh1, sc1, w_in, EVEN_GROUPS), *pars, st)
            parts_l, parts_c = [o_l], [o_c]
        else:
            w_in, w_out = od_w_in[j].astype(BF16), od_w_out[j].astype(BF16)
            qc, kc, vc, xdc, gdc = in_proj(ctx, norm1_w[l], csh1, csc1, w_in, ODD_SIZES)
            ql, kl, vl, xdl, gdl = in_proj(x, norm1_w[l], sh1, sc1, w_in, ODD_SIZES)
            att_l = window_attention_pallas(ql, kl, vl, kc, vc, c_sink[j], cos, sin)
            rg_pars = (d_conv_w[j], d_conv_b[j], d_w_r[j], d_b_r[j], d_w_i[j], d_b_i[j], d_lambda[j])
            zero = jnp.zeros((b_, D_WIDTH), F32)
            rg_c, st = rglru_pallas(xdc, gdc, *rg_pars, (zero, zero))
            rg_l, _ = rglru_pallas(xdl, gdl, *rg_pars, st)
            parts_l = [att_l, rg_l]
            if not last:
                parts_c = [context_attention_pallas(qc, kc, vc, c_sink[j]), rg_c]
        lat = tuple(post_mixer(parts_l, w_out, x, g1, norm2_w[l], sh2, sc2, router_t)) + (g2,)
        con = None if last else tuple(post_mixer(parts_c, w_out, ctx, cg1, norm2_w[l], csh2, csc2, router_t)) + (cg2,)
        x, ctx = moe_residual(lat, con, moe_w1, moe_w3, moe_w2, l)
    return final_norm(x, final_norm_w)
```

```python
import functools

import jax
import jax.numpy as jnp
from jax import lax
from jax.experimental import pallas as pl
from jax.experimental.pallas import tpu as pltpu

D_MODEL = 1024
DEPTH = 4
GRID_W = 64
EPS = 1e-6
F32 = jnp.float32
BF16 = jnp.bfloat16
HI = lax.Precision.HIGHEST

A_HEADS = 4
A_DK = 128
A_DV = 128
A_CONV = 4
A_CHUNK = 64
A_QKV = 2 * A_HEADS * A_DK + A_HEADS * A_DV
B_HEADS = 4
B_DK = 128
B_DV = 128
B_CHUNK = 64
C_HEADS = 8
C_KV_HEADS = 2
C_HD = 64
C_WIN = 128
C_BLOCK = 128
ROPE_THETA = 10000.0
D_WIDTH = 512
D_BLOCKS = 8
D_BW = D_WIDTH // D_BLOCKS
D_CONV = 4
D_C = 8.0
N_EXPERTS = 16
EXPERT_FF = 1024
EC_FACTOR = 2

EVEN_SIZES = (A_QKV, A_HEADS * A_DV, 2 * A_HEADS, 2 * A_HEADS,
              B_HEADS * B_DK, B_HEADS * B_DV, 2 * B_HEADS * B_DK, B_HEADS * B_DV)
ODD_SIZES = (C_HEADS * C_HD, C_KV_HEADS * C_HD, C_KV_HEADS * C_HD, D_WIDTH, D_WIDTH)

LANES = 128
SUBLANES = 8
VMEM_LIMIT_BYTES = 56 * 1024 * 1024

ROW_TILE = 512
HALO = SUBLANES


def _cuts(sizes):
    out, acc = [], 0
    for s in sizes[:-1]:
        acc += s
        out.append(acc)
    return out


def _params(n_axes):
    return pltpu.CompilerParams(dimension_semantics=("arbitrary",) * n_axes, vmem_limit_bytes=VMEM_LIMIT_BYTES)


def _dot(a, b, precision=None):
    return jnp.dot(a, b, preferred_element_type=F32, precision=precision)


def _dot_nt(a, b, precision=None):
    return lax.dot_general(a, b, (((1,), (1,)), ((), ())), preferred_element_type=F32, precision=precision)


def _dot_tn(a, b):
    return lax.dot_general(a, b, (((0,), (0,)), ((), ())), preferred_element_type=F32)


def _silu(x):
    return x * jax.nn.sigmoid(x)


def _rms(x):
    return x * lax.rsqrt(jnp.mean(x * x, axis=-1, keepdims=True) + EPS)


U32 = jnp.uint32
HIGH16 = 0xFFFF0000


def _pack_halves(h):
    half = h.shape[-1] // 2
    lo = pltpu.bitcast(h[:, :half].astype(BF16).astype(F32), U32)
    hi = pltpu.bitcast(h[:, half:].astype(BF16).astype(F32), U32)
    return (hi & jnp.uint32(HIGH16)) | lax.shift_right_logical(lo, jnp.uint32(16))


def _unpack_halves(p):
    lo = pltpu.bitcast(lax.shift_left(p, jnp.uint32(16)), F32).astype(BF16)
    hi = pltpu.bitcast(p & jnp.uint32(HIGH16), F32).astype(BF16)
    return jnp.concatenate([lo, hi], axis=1)


MOD_COLS = 1536


def _mod_kernel(c_ref, w_ref, b_ref, o_ref):
    o_ref[0] = _dot(_silu(c_ref[...]), w_ref[0]) + b_ref[0]


def modulation(c_rows, w_mod, b_mod):
    r, d = c_rows.shape
    depth, _, wide = w_mod.shape
    return pl.pallas_call(
        _mod_kernel,
        grid=(depth, wide // MOD_COLS),
        in_specs=[pl.BlockSpec((r, d), lambda l, j: (0, 0)),
                  pl.BlockSpec((1, d, MOD_COLS), lambda l, j: (l, 0, j)),
                  pl.BlockSpec((1, 1, MOD_COLS), lambda l, j: (l, 0, j))],
        out_specs=pl.BlockSpec((1, r, MOD_COLS), lambda l, j: (l, 0, j)),
        out_shape=jax.ShapeDtypeStruct((depth, r, wide), F32),
        compiler_params=_params(2),
        name="modulation",
    )(c_rows, w_mod, b_mod[:, None, :])


def _norm_mod(x, nw, shift, scale):
    return _rms(x) * nw * (1.0 + scale) + shift


def _in_proj_kernel(x_ref, nw_ref, sh_ref, sc_ref, w_ref, *o_refs, splits):
    h = _norm_mod(x_ref[0], nw_ref[...], sh_ref[0], sc_ref[0]).astype(BF16)
    off = 0
    for o_ref, s in zip(o_refs, splits):
        o_ref[0] = _dot(h, w_ref[:, off:off + s])
        off += s


def in_proj(x, norm_w, shift, scale, w, splits):
    b_, n, d = x.shape
    tm = min(n, ROW_TILE)
    ntot = sum(splits)
    vec = pl.BlockSpec((1, 1, d), lambda b, i: (b, 0, 0))
    return pl.pallas_call(
        functools.partial(_in_proj_kernel, splits=tuple(splits)),
        grid=(b_, n // tm),
        in_specs=[pl.BlockSpec((1, tm, d), lambda b, i: (b, i, 0)),
                  pl.BlockSpec((1, d), lambda b, i: (0, 0)), vec, vec,
                  pl.BlockSpec((d, ntot), lambda b, i: (0, 0))],
        out_specs=[pl.BlockSpec((1, tm, s), lambda b, i: (b, i, 0)) for s in splits],
        out_shape=[jax.ShapeDtypeStruct((b_, n, s), F32) for s in splits],
        compiler_params=_params(2),
        name="in_proj",
    )(x, norm_w.reshape(1, d), shift[:, None, :], scale[:, None, :], w)


def _post_mixer_kernel(*refs, n_parts):
    parts = refs[:n_parts]
    w_ref, x_ref, g1_ref, nw_ref, sh_ref, sc_ref, rt_ref, x1_ref, h2_ref, aff_ref = refs[n_parts:]
    y = None
    off = 0
    for p_ref in parts:
        k = p_ref.shape[-1]
        t = _dot(p_ref[0].astype(BF16), w_ref[off:off + k, :])
        y = t if y is None else y + t
        off += k
    x1 = x_ref[0] + g1_ref[0] * y
    x1_ref[0] = x1
    h2 = _norm_mod(x1, nw_ref[...], sh_ref[0], sc_ref[0])
    h2_ref[0] = _pack_halves(h2)
    logits = _dot_nt(rt_ref[...], h2.astype(BF16))
    e = jnp.exp(logits - jnp.max(logits, axis=0, keepdims=True))
    aff_ref[0] = e / jnp.sum(e, axis=0, keepdims=True)


def post_mixer(parts, w_out, x, g1, norm_w, shift, scale, router_t):
    b_, n, d = x.shape
    e_ = router_t.shape[0]
    tm = min(n, ROW_TILE)
    vec = pl.BlockSpec((1, 1, d), lambda b, i: (b, 0, 0))
    tok = lambda k: pl.BlockSpec((1, tm, k), lambda b, i: (b, i, 0))
    return pl.pallas_call(
        functools.partial(_post_mixer_kernel, n_parts=len(parts)),
        grid=(b_, n // tm),
        in_specs=[tok(p.shape[-1]) for p in parts] + [
            pl.BlockSpec(w_out.shape, lambda b, i: (0, 0)), tok(d), vec,
            pl.BlockSpec((1, d), lambda b, i: (0, 0)), vec, vec,
            pl.BlockSpec((e_, d), lambda b, i: (0, 0))],
        out_specs=[tok(d), tok(d // 2), pl.BlockSpec((1, e_, tm), lambda b, i: (b, 0, i))],
        out_shape=[jax.ShapeDtypeStruct((b_, n, d), F32), jax.ShapeDtypeStruct((b_, n, d // 2), U32),
                   jax.ShapeDtypeStruct((b_, e_, n), F32)],
        compiler_params=_params(2),
        name="post_mixer",
    )(*parts, w_out, x, g1[:, None, :], norm_w.reshape(1, d), shift[:, None, :], scale[:, None, :], router_t)


def _final_kernel(x_ref, nw_ref, o_ref):
    o_ref[0] = _rms(x_ref[0]) * nw_ref[...]


def final_norm(x, norm_w):
    b_, n, d = x.shape
    tm = min(n, 2 * ROW_TILE)
    tok = pl.BlockSpec((1, tm, d), lambda b, i: (b, i, 0))
    return pl.pallas_call(
        _final_kernel,
        grid=(b_, n // tm),
        in_specs=[tok, pl.BlockSpec((1, d), lambda b, i: (0, 0))],
        out_specs=tok,
        out_shape=jax.ShapeDtypeStruct((b_, n, d), F32),
        compiler_params=_params(2),
        name="final_norm",
    )(x, norm_w.reshape(1, d))


CHUNK = A_CHUNK
SUB = 16
EVEN_TILE = 256
HA = A_HEADS * A_DK
HB = B_HEADS * B_DK
AB_PAD = LANES
EVEN_GROUPS = (A_QKV, A_HEADS * A_DV, AB_PAD, HB, B_HEADS * B_DV, 2 * HB, B_HEADS * B_DV)
NEUMANN_STEPS = 5


def _split2(x):
    hi = x.astype(BF16)
    return hi, (x - hi.astype(F32)).astype(BF16)


def _dot_split(a, b):
    (ah, al), (bh, bl) = a, b
    return _dot(ah, bh) + _dot(ah, bl) + _dot(al, bh)


def _dot_exact_lhs(a_bf16, x):
    x1 = x.astype(BF16)
    r1 = x - x1.astype(F32)
    x2 = r1.astype(BF16)
    x3 = (r1 - x2.astype(F32)).astype(BF16)
    return _dot(a_bf16, x1) + _dot(a_bf16, x2) + _dot(a_bf16, x3)


def _even_kernel(*refs, tile, n_tiles, reverse, combine):
    if combine:
        (qkvn_ref, ab_ref, qb_ref, ib_ref, fb_ref, exp_ref, alog_ref, dtb_ref, lb_ref,
         sa0_ref, sb0_ref, of_ref, ga_ref, gb_ref, anw_ref, bnw_ref,
         o_ref, sa_ref, sb_ref, g_scr, st_a, st_b) = refs
    else:
        (qkv_ref, qkvp_ref, qkvnx_ref, ab_ref, qb_ref, ib_ref, fb_ref, cw_ref, exp_ref, alog_ref, dtb_ref, lb_ref,
         sa0_ref, sb0_ref,
         o_ref, qkvn_ref, sa_ref, sb_ref, xbuf, g_scr, st_a, st_b) = refs
    i = pl.program_id(1)
    t = (n_tiles - 1 - i) if reverse else i

    @pl.when(i == 0)
    def _():
        st_a[...] = sa0_ref[0]
        st_b[...] = sb0_ref[0]

    L = CHUNK
    n_chunks = tile // L
    n_blocks = tile // SUB
    chunk_order = range(n_chunks - 1, -1, -1) if reverse else range(n_chunks)
    block_order = range(n_blocks - 1, -1, -1) if reverse else range(n_blocks)

    ti = lax.broadcasted_iota(jnp.int32, (tile, tile), 0)
    tj = lax.broadcasted_iota(jnp.int32, (tile, tile), 1)
    t_incl = (tj >= ti) if reverse else (tj <= ti)
    tri_chunk = (t_incl & ((ti // L) == (tj // L))).astype(BF16)
    tri_blk = (t_incl & ((ti // SUB) == (tj // SUB))).astype(BF16)
    ones_blk = ((ti // SUB) == (tj // SUB)).astype(BF16)
    ii = lax.broadcasted_iota(jnp.int32, (L, L), 0)
    jj = lax.broadcasted_iota(jnp.int32, (L, L), 1)
    incl = (jj >= ii) if reverse else (jj <= ii)
    strict = (jj > ii) if reverse else (jj < ii)
    eye = (ii == jj).astype(F32)
    last_row = 0 if reverse else L - 1

    if not combine:
        xbuf[pl.ds(0, HALO), :] = jnp.where(t > 0, qkvp_ref[0], 0.0)
        xbuf[pl.ds(HALO, tile), :] = qkv_ref[0]
        xbuf[pl.ds(HALO + tile, HALO), :] = jnp.where(t < n_tiles - 1, qkvnx_ref[0], 0.0)
        conv = _silu(sum(cw_ref[pl.ds(k, 1), :] * xbuf[pl.ds(HALO - 1 + k, tile), :] for k in range(A_CONV)))
    ab_x = _dot(ab_ref[0], exp_ref[...], HI)
    g_all = -jnp.exp(alog_ref[...]) * jax.nn.softplus(ab_x[:, :HA] + dtb_ref[...])
    beta_all = jax.nn.sigmoid(ab_x[:, HA:])
    cum_all = _dot_exact_lhs(tri_chunk, g_all)
    ecum_all = jnp.exp(cum_all)

    chains = [(c, h) for c in chunk_order for h in range(A_HEADS)]
    pre = {}
    neg_m = []
    for c, h in chains:
        rs = slice(c * L, (c + 1) * L)
        hs = slice(h * A_DK, (h + 1) * A_DK)
        ks_, vs_ = slice(HA + h * A_DK, HA + (h + 1) * A_DK), slice(2 * HA + h * A_DV, 2 * HA + (h + 1) * A_DV)
        if combine:
            q, k, v = qkvn_ref[0, rs, hs], qkvn_ref[0, rs, ks_], qkvn_ref[0, rs, vs_]
        else:
            qh, kh, v = conv[rs, hs], conv[rs, ks_], conv[rs, vs_]
            q = qh * (lax.rsqrt(jnp.sum(qh * qh, axis=-1, keepdims=True) + EPS) * (A_DK ** -0.5))
            k = kh * lax.rsqrt(jnp.sum(kh * kh, axis=-1, keepdims=True) + EPS)
            qkvn_ref[0, rs, hs], qkvn_ref[0, rs, ks_], qkvn_ref[0, rs, vs_] = q, k, v
        beta, cum, ecum = beta_all[rs, hs], cum_all[rs, hs], ecum_all[rs, hs]
        r_ll = cum.T[:L, :]
        decay = jnp.where(incl, jnp.exp(jnp.where(incl, cum[:, :L] - r_ll, 0.0)), 0.0)
        kb = k * beta
        k_t = k.T.astype(BF16)
        neg_m.append(jnp.where(strict, -_dot(kb.astype(BF16), k_t) * decay, 0.0))
        qk = _dot(q.astype(BF16), k_t) * decay
        total = cum[last_row:last_row + 1, :]
        pre[(c, h)] = dict(rhs=jnp.concatenate([v * beta, kb * ecum], axis=1), qk=qk.astype(BF16),
                           qd=(q * ecum).astype(BF16), kd=(k * jnp.exp(total - cum)).astype(BF16),
                           gl=jnp.exp(total))
    p = [_split2(m) for m in neg_m]
    tinv = [eye + m for m in neg_m]
    for _ in range(NEUMANN_STEPS):
        p = [_split2(_dot_split(x, x)) for x in p]
        tinv = [tv + _dot_split(_split2(tv), x) for tv, x in zip(tinv, p)]
    sol_of = {ch: _dot_split(_split2(tv), _split2(pre[ch]["rhs"])) for tv, ch in zip(tinv, chains)}

    for c in chunk_order:
        rs = slice(c * L, (c + 1) * L)
        s_old = [st_a[h] for h in range(A_HEADS)]
        s16 = [s.astype(BF16) for s in s_old]
        ws = [_dot(sol_of[(c, h)][:, A_DV:].astype(BF16), s16[h]) for h in range(A_HEADS)]
        qs_ = [_dot(pre[(c, h)]["qd"], s16[h]) for h in range(A_HEADS)]
        v_new = [(sol_of[(c, h)][:, :A_DV] - ws[h]).astype(BF16) for h in range(A_HEADS)]
        for h in range(A_HEADS):
            o_ref[0, rs, h * A_DV:(h + 1) * A_DV] = qs_[h] + _dot(pre[(c, h)]["qk"], v_new[h])
            st_a[h] = s_old[h] * pre[(c, h)]["gl"] + _dot_tn(pre[(c, h)]["kd"], v_new[h])

    hq = _silu(qb_ref[0])
    hv = ib_ref[0]
    lb = lb_ref[...]
    fg = lb + (1.0 - lb) * jax.nn.sigmoid(fb_ref[0])
    lf = jnp.log(fg)
    hk = 1.0 - fg
    b_all = _dot_exact_lhs(tri_blk, lf)
    e_all = _dot_exact_lhs(ones_blk, lf)
    qs_all = (hq * jnp.exp(b_all)).astype(BF16)
    ks_all = (hk * jnp.exp(e_all - b_all)).astype(BF16)
    hv16 = hv.astype(BF16)
    row_in_blk = lax.broadcasted_iota(jnp.int32, (tile, HB), 0) % SUB
    od = [jnp.zeros((tile, B_DV), F32) for _ in range(B_HEADS)]
    for s_ in range(SUB):
        valid = (row_in_blk + s_ < SUB) if reverse else (row_in_blk >= s_)
        shift = ((tile - s_) if reverse else s_) % tile
        roll = (lambda x: x) if s_ == 0 else (lambda x: pltpu.roll(x, shift, axis=0))
        prod = jnp.where(valid, hq * roll(hk) * jnp.exp(b_all - roll(b_all)), 0.0)
        vv = roll(hv)
        for h in range(B_HEADS):
            hs = slice(h * B_DK, (h + 1) * B_DK)
            od[h] = od[h] + jnp.sum(prod[:, hs], axis=-1, keepdims=True) * vv[:, hs]
    for blk in block_order:
        bs = slice(blk * SUB, (blk + 1) * SUB)
        for h in range(B_HEADS):
            hs = slice(h * B_DK, (h + 1) * B_DK)
            g_scr[blk * B_HEADS + h] = _dot_tn(hv16[bs, hs], ks_all[bs, hs])

    st = [st_b[h] for h in range(B_HEADS)]
    for blk in block_order:
        bs = slice(blk * SUB, (blk + 1) * SUB)
        for h in range(B_HEADS):
            hs = slice(h * B_DK, (h + 1) * B_DK)
            o_ref[0, bs, HA + h * B_DV:HA + (h + 1) * B_DV] = _dot_nt(qs_all[bs, hs], st[h].astype(BF16)) + od[h][bs]
            st[h] = st[h] * jnp.exp(e_all[blk * SUB:blk * SUB + 1, hs]) + g_scr[blk * B_HEADS + h]
    for h in range(B_HEADS):
        st_b[h] = st[h]

    if combine:
        tot = of_ref[0] + o_ref[0]
        for h in range(A_HEADS + B_HEADS):
            hs = slice(h * A_DV, (h + 1) * A_DV)
            nw = anw_ref[...] if h < A_HEADS else bnw_ref[...]
            gate = ga_ref[0, :, hs] if h < A_HEADS else gb_ref[0, :, pl.ds((h - A_HEADS) * B_DV, B_DV)]
            o_ref[0, :, hs] = _rms(tot[:, hs]) * nw * _silu(gate)

    @pl.when(i == n_tiles - 1)
    def _():
        sa_ref[0] = st_a[...]
        sb_ref[0] = st_b[...]


def _even_sweep(outs, z, conv_w, a_log, dt_bias, lb, s_a, s_b, of=None, a_norm_w=None, b_norm_w=None):
    qkv, ga, ab, qb, ib, fb, gb = outs
    b_, n, _ = qkv.shape
    reverse = z == 1
    combine = of is not None
    tile = min(n, EVEN_TILE)
    n_tiles = n // tile
    bpt = tile // HALO
    nblk = n // HALO
    tidx = lambda i: (n_tiles - 1 - i) if reverse else i
    tmap = lambda b, i: (b, tidx(i), 0)
    pmap = lambda b, i: (b, jnp.maximum(tidx(i) * bpt - 1, 0), 0)
    nmap = lambda b, i: (b, jnp.minimum((tidx(i) + 1) * bpt, nblk - 1), 0)
    zmap = lambda b, i: (b, tidx(i), z)
    const2 = lambda b, i: (0, 0)
    smap = lambda b, i: (b, 0, 0, 0)
    src = lax.broadcasted_iota(jnp.int32, (AB_PAD, 2 * HA), 0)
    dst = lax.broadcasted_iota(jnp.int32, (AB_PAD, 2 * HA), 1)
    expand = (src == jnp.where(dst >= HA, 2 * A_HEADS, 0) + z * A_HEADS + (dst % HA) // A_DK).astype(F32)
    rep = lambda p_: jnp.broadcast_to(p_[:, None], (A_HEADS, A_DK)).reshape(1, HA)
    tok = lambda w: pl.BlockSpec((1, tile, w), tmap)
    common_specs = [tok(AB_PAD), tok(HB), tok(HB), pl.BlockSpec((1, tile, HB), zmap)]
    common_args = [ab, qb, ib, fb]
    par_specs = [pl.BlockSpec((AB_PAD, 2 * HA), const2),
                 pl.BlockSpec((1, HA), const2), pl.BlockSpec((1, HA), const2), pl.BlockSpec((1, HB), const2),
                 pl.BlockSpec((1, A_HEADS, A_DK, A_DV), smap), pl.BlockSpec((1, B_HEADS, B_DV, B_DK), smap)]
    par_args = [expand, rep(a_log[z]), rep(dt_bias[z]), lb[z].reshape(1, HB), s_a, s_b]
    state_specs = [pl.BlockSpec((1, A_HEADS, A_DK, A_DV), smap), pl.BlockSpec((1, B_HEADS, B_DV, B_DK), smap)]
    state_shapes = [jax.ShapeDtypeStruct(s_a.shape, F32), jax.ShapeDtypeStruct(s_b.shape, F32)]
    scratch = [pltpu.VMEM((tile // SUB * B_HEADS, B_DV, B_DK), F32),
               pltpu.VMEM((A_HEADS, A_DK, A_DV), F32), pltpu.VMEM((B_HEADS, B_DV, B_DK), F32)]
    if combine:
        in_specs = [tok(A_QKV)] + common_specs + par_specs + [
            tok(HA + HB), tok(HA), tok(HB), pl.BlockSpec((1, A_DV), const2), pl.BlockSpec((1, B_DV), const2)]
        args = [qkv] + common_args + par_args + [of, ga, gb, a_norm_w.reshape(1, A_DV), b_norm_w.reshape(1, B_DV)]
        out_specs = [tok(HA + HB)] + state_specs
        out_shape = [jax.ShapeDtypeStruct((b_, n, HA + HB), F32)] + state_shapes
    else:
        in_specs = [tok(A_QKV), pl.BlockSpec((1, HALO, A_QKV), pmap), pl.BlockSpec((1, HALO, A_QKV), nmap)] + \
            common_specs + [pl.BlockSpec((A_CONV, A_QKV), const2)] + par_specs
        args = [qkv, qkv, qkv] + common_args + [conv_w] + par_args
        out_specs = [tok(HA + HB), tok(A_QKV)] + state_specs
        out_shape = [jax.ShapeDtypeStruct((b_, n, HA + HB), F32), jax.ShapeDtypeStruct((b_, n, A_QKV), F32)] + state_shapes
        scratch = [pltpu.VMEM((tile + 2 * HALO, A_QKV), F32)] + scratch
    return pl.pallas_call(
        functools.partial(_even_kernel, tile=tile, n_tiles=n_tiles, reverse=reverse, combine=combine),
        grid=(b_, n_tiles),
        in_specs=in_specs,
        out_specs=out_specs,
        out_shape=out_shape,
        scratch_shapes=scratch,
        compiler_params=_params(2),
        name="even_bwd_combine" if combine else "even_fwd",
    )(*args)


def even_mixer_pallas(outs, conv_w, a_log, dt_bias, a_norm_w, lb, b_norm_w, states):
    (saf, sbf), (sab, sbb) = states
    o_f, qkv_n, saf, sbf = _even_sweep(outs, 0, conv_w, a_log, dt_bias, lb, saf, sbf)
    outs_b = [qkv_n] + list(outs[1:])
    o, sab, sbb = _even_sweep(outs_b, 1, conv_w, a_log, dt_bias, lb, sab, sbb, o_f, a_norm_w, b_norm_w)
    return o, ((saf, sbf), (sab, sbb))


def _even_w_in(w):
    c = _cuts(EVEN_SIZES)
    ab = jnp.pad(w[:, c[1]:c[3]], ((0, 0), (0, AB_PAD - (c[3] - c[1]))))
    return jnp.concatenate([w[:, :c[1]], ab, w[:, c[3]:]], axis=1).astype(BF16)


ROPE_NF = C_HD // 4
NEG_BIG = -1e30


def _rope(t, cos, sin_signed):
    w = t.shape[-1]
    lane = lax.broadcasted_iota(jnp.int32, t.shape, 1)
    partner = jnp.where(lane % (2 * ROPE_NF) < ROPE_NF,
                        pltpu.roll(t, w - ROPE_NF, axis=1), pltpu.roll(t, ROPE_NF, axis=1))
    return t * cos + partner * sin_signed


def _dup_kv_head(t, kh):
    lane = lax.broadcasted_iota(jnp.int32, t.shape, 1)
    rolled = pltpu.roll(t, C_HD, axis=1)
    own_half = (lane < C_HD) if kh == 0 else (lane >= C_HD)
    return jnp.where(own_half, t, rolled)


def _attn_kernel(*refs, banded, n_blocks):
    if banded:
        (q_ref, kp_ref, k_ref, kn_ref, vp_ref, v_ref, vn_ref, cq_ref, sq_ref, cp_ref, sp_ref, cn_ref, sn_ref,
         kc_ref, vc_ref, sink_ref, o_ref) = refs
    else:
        q_ref, kc_ref, vc_ref, sink_ref, o_ref = refs
    nb = pl.program_id(1)
    blk = q_ref.shape[1]
    q = q_ref[0] * (C_HD ** -0.5)
    kc = kc_ref[0]
    vc = vc_ref[0]
    if banded:
        cq, sq = cq_ref[...], sq_ref[...]
        q = _rope(q, jnp.concatenate([cq] * (C_HEADS // C_KV_HEADS), axis=1),
                  jnp.concatenate([sq] * (C_HEADS // C_KV_HEADS), axis=1))
        kb = [_rope(kp_ref[0], cp_ref[...], sp_ref[...]), _rope(k_ref[0], cq, sq), _rope(kn_ref[0], cn_ref[...], sn_ref[...])]
        keys = jnp.concatenate(kb + [kc], axis=0)
        vals = jnp.concatenate([vp_ref[0], v_ref[0], vn_ref[0], vc], axis=0)
        qpos = lax.broadcasted_iota(jnp.int32, (blk, keys.shape[0]), 0)
        kcol = lax.broadcasted_iota(jnp.int32, (blk, keys.shape[0]), 1)
        rel = kcol - blk
        in_band = (jnp.abs(rel - qpos) <= C_WIN) & (rel + nb * blk >= 0) & (rel + nb * blk < n_blocks * blk)
        mask = in_band | (kcol >= 3 * blk)
    else:
        keys, vals, mask = kc, vc, None
    keys = keys.astype(BF16)
    vals = vals.astype(BF16)
    lane_q = lax.broadcasted_iota(jnp.int32, (blk, 2 * C_HD), 1)
    grp = C_HEADS // C_KV_HEADS
    heads = range(C_HEADS)
    k_dup = [_dup_kv_head(keys, kh) for kh in range(C_KV_HEADS)]
    v_dup = [_dup_kv_head(vals, kh) for kh in range(C_KV_HEADS)]
    qh = [jnp.where((lane_q < C_HD) if h % 2 == 0 else (lane_q >= C_HD),
                    q[:, (h // 2) * 2 * C_HD:(h // 2 + 1) * 2 * C_HD], 0.0).astype(BF16) for h in heads]
    s = [_dot_nt(qh[h], k_dup[h // grp]) for h in heads]
    if mask is not None:
        s = [jnp.where(mask, x, NEG_BIG) for x in s]
    m = [jnp.maximum(jnp.max(s[h], axis=-1, keepdims=True), sink_ref[h]) for h in heads]
    e = [jnp.exp(s[h] - m[h]) for h in heads]
    denom = [jnp.sum(e[h], axis=-1, keepdims=True) + jnp.exp(sink_ref[h] - m[h]) for h in heads]
    p = [(e[h] / denom[h]).astype(BF16) for h in heads]
    out = [_dot(p[h], v_dup[h // grp]) for h in heads]
    for pair in range(C_HEADS // 2):
        o_ref[0, :, pair * 2 * C_HD:(pair + 1) * 2 * C_HD] = jnp.where(lane_q < C_HD, out[2 * pair], out[2 * pair + 1])


def _rope_tables(n):
    tok = lax.iota(jnp.int32, n)
    row = (tok // GRID_W).astype(F32)
    col = (tok % GRID_W).astype(F32)
    inv = ROPE_THETA ** (-lax.iota(F32, ROPE_NF) / ROPE_NF)
    ang_r, ang_c = row[:, None] * inv, col[:, None] * inv
    cos = jnp.concatenate([jnp.cos(ang_r)] * 2 + [jnp.cos(ang_c)] * 2, axis=-1)
    sin = jnp.concatenate([-jnp.sin(ang_r), jnp.sin(ang_r), -jnp.sin(ang_c), jnp.sin(ang_c)], axis=-1)
    return jnp.concatenate([cos, cos], axis=-1), jnp.concatenate([sin, sin], axis=-1)


def window_attention_pallas(q, k, v, kc, vc, sink, cos, sin):
    b_, n, hq = q.shape
    hk = k.shape[-1]
    lc = kc.shape[1]
    blk = C_BLOCK
    nb_ = n // blk
    cur = lambda b, i: (b, i, 0)
    prv = lambda b, i: (b, jnp.maximum(i - 1, 0), 0)
    nxt = lambda b, i: (b, jnp.minimum(i + 1, nb_ - 1), 0)
    tcur = lambda b, i: (i, 0)
    tprv = lambda b, i: (jnp.maximum(i - 1, 0), 0)
    tnxt = lambda b, i: (jnp.minimum(i + 1, nb_ - 1), 0)
    kspec = lambda m: pl.BlockSpec((1, blk, hk), m)
    tspec = lambda m: pl.BlockSpec((blk, hk), m)
    cspec = pl.BlockSpec((1, lc, hk), lambda b, i: (b, 0, 0))
    return pl.pallas_call(
        functools.partial(_attn_kernel, banded=True, n_blocks=nb_),
        grid=(b_, nb_),
        in_specs=[pl.BlockSpec((1, blk, hq), cur), kspec(prv), kspec(cur), kspec(nxt),
                  kspec(prv), kspec(cur), kspec(nxt),
                  tspec(tcur), tspec(tcur), tspec(tprv), tspec(tprv), tspec(tnxt), tspec(tnxt),
                  cspec, cspec, pl.BlockSpec(memory_space=pltpu.SMEM)],
        out_specs=pl.BlockSpec((1, blk, hq), cur),
        out_shape=jax.ShapeDtypeStruct((b_, n, hq), F32),
        compiler_params=_params(2),
        name="window_attention",
    )(q, k, k, k, v, v, v, cos, sin, cos, sin, cos, sin, kc, vc, sink)


def context_attention_pallas(q, kc, vc, sink):
    b_, lc, hq = q.shape
    hk = kc.shape[-1]
    blk = min(lc, C_BLOCK)
    cspec = pl.BlockSpec((1, lc, hk), lambda b, i: (b, 0, 0))
    return pl.pallas_call(
        functools.partial(_attn_kernel, banded=False, n_blocks=lc // blk),
        grid=(b_, lc // blk),
        in_specs=[pl.BlockSpec((1, blk, hq), lambda b, i: (b, i, 0)), cspec, cspec,
                  pl.BlockSpec(memory_space=pltpu.SMEM)],
        out_specs=pl.BlockSpec((1, blk, hq), lambda b, i: (b, i, 0)),
        out_shape=jax.ShapeDtypeStruct((b_, lc, hq), F32),
        compiler_params=_params(2),
        name="context_attention",
    )(q, kc, vc, sink)


def _scan8(a, u, reverse):
    row = lax.broadcasted_iota(jnp.int32, a.shape, 0)
    for s in (1, 2, 4):
        shift = (SUBLANES - s) if reverse else s
        a_sh = pltpu.roll(a, shift, axis=0)
        u_sh = pltpu.roll(u, shift, axis=0)
        valid = (row < SUBLANES - s) if reverse else (row >= s)
        u = jnp.where(valid, a * u_sh + u, u)
        a = jnp.where(valid, a * a_sh, a)
    return a, u


def _gelu_tanh(x):
    return 0.5 * x * (1.0 + jnp.tanh(0.7978845608028654 * (x + 0.044715 * (x * x * x))))


def _rglru_kernel(*refs, tile, n_tiles, reverse, combine):
    if combine:
        (x_ref, xp_ref, xn_ref, cw_ref, cb_ref, wbd_ref, bias_ref, lam_ref, h0_ref, hf_ref, gd_ref,
         o_ref, hl_ref, xbuf, a_s, u_s, carry) = refs
    else:
        (x_ref, xp_ref, xn_ref, cw_ref, cb_ref, wbd_ref, bias_ref, lam_ref, h0_ref,
         o_ref, hl_ref, xbuf, a_s, u_s, carry) = refs
    i = pl.program_id(1)
    t = (n_tiles - 1 - i) if reverse else i
    w = x_ref.shape[-1]

    @pl.when(i == 0)
    def _():
        carry[...] = jnp.broadcast_to(h0_ref[0], (SUBLANES, w))

    xbuf[pl.ds(0, HALO), :] = jnp.where(t > 0, xp_ref[0], 0.0)
    xbuf[pl.ds(HALO, tile), :] = x_ref[0]
    xbuf[pl.ds(HALO + tile, HALO), :] = jnp.where(t < n_tiles - 1, xn_ref[0], 0.0)
    xc = cb_ref[...] + sum(cw_ref[pl.ds(k, 1), :] * xbuf[pl.ds(HALO - 1 + k, tile), :] for k in range(D_CONV))

    z = _dot(xc.astype(BF16), wbd_ref[...]) + bias_ref[...]
    r = jax.nn.sigmoid(z[:, :w])
    gi = jax.nn.sigmoid(z[:, w:])
    lam = lam_ref[...]
    log_a = (-D_C * jnp.log(1.0 + jnp.exp(-lam))) * r
    a = jnp.exp(log_a)
    a_s[...] = a
    u_s[...] = jnp.sqrt(1.0 - a * a) * gi * xc

    n_groups = tile // SUBLANES

    def step(g, c):
        gg = (n_groups - 1 - g) if reverse else g
        r0 = pl.multiple_of(gg * SUBLANES, SUBLANES)
        ac, uc = _scan8(a_s[pl.ds(r0, SUBLANES), :], u_s[pl.ds(r0, SUBLANES), :], reverse)
        h = ac * c + uc
        o_ref[0, pl.ds(r0, SUBLANES), :] = h
        last = h[0:1, :] if reverse else h[SUBLANES - 1:SUBLANES, :]
        return jnp.broadcast_to(last, (SUBLANES, w))

    c_fin = lax.fori_loop(0, n_groups, step, carry[...], unroll=4 if n_groups % 4 == 0 else 1)
    carry[...] = c_fin

    if combine:
        o_ref[0] = (hf_ref[0] + o_ref[0]) * _gelu_tanh(gd_ref[0])

    @pl.when(i == n_tiles - 1)
    def _():
        hl_ref[0] = c_fin[0:1, :]


def _rglru_sweep(xd, conv_w, conv_b, wbd, bias, lam, h0, hf=None, gd=None, *, reverse):
    b_, n, w = xd.shape
    combine = hf is not None
    tile = min(n, 512)
    n_tiles = n // tile
    blocks_per_tile = tile // HALO
    n_blocks = n // HALO

    def tmap(b, i):
        return (b, (n_tiles - 1 - i) if reverse else i, 0)

    def pmap(b, i):
        t = (n_tiles - 1 - i) if reverse else i
        return (b, jnp.maximum(t * blocks_per_tile - 1, 0), 0)

    def nmap(b, i):
        t = (n_tiles - 1 - i) if reverse else i
        return (b, jnp.minimum((t + 1) * blocks_per_tile, n_blocks - 1), 0)

    const2 = lambda b, i: (0, 0)
    in_specs = [
        pl.BlockSpec((1, tile, w), tmap),
        pl.BlockSpec((1, HALO, w), pmap),
        pl.BlockSpec((1, HALO, w), nmap),
        pl.BlockSpec((D_CONV, w), const2),
        pl.BlockSpec((1, w), const2),
        pl.BlockSpec((w, 2 * w), const2),
        pl.BlockSpec((1, 2 * w), const2),
        pl.BlockSpec((1, w), const2),
        pl.BlockSpec((1, 1, w), lambda b, i: (b, 0, 0)),
    ]
    args = [xd, xd, xd, conv_w, conv_b, wbd, bias, lam, h0]
    if combine:
        in_specs += [pl.BlockSpec((1, tile, w), tmap), pl.BlockSpec((1, tile, w), tmap)]
        args += [hf, gd]
    return pl.pallas_call(
        functools.partial(_rglru_kernel, tile=tile, n_tiles=n_tiles, reverse=reverse, combine=combine),
        grid=(b_, n_tiles),
        in_specs=in_specs,
        out_specs=[pl.BlockSpec((1, tile, w), tmap), pl.BlockSpec((1, 1, w), lambda b, i: (b, 0, 0))],
        out_shape=[jax.ShapeDtypeStruct((b_, n, w), F32), jax.ShapeDtypeStruct((b_, 1, w), F32)],
        scratch_shapes=[pltpu.VMEM((tile + 2 * HALO, w), F32), pltpu.VMEM((tile, w), F32),
                        pltpu.VMEM((tile, w), F32), pltpu.VMEM((SUBLANES, w), F32)],
        compiler_params=_params(2),
        name="rglru_bwd_combine" if combine else "rglru_fwd",
    )(*args)


def _block_diag(wz):
    eye = jnp.eye(D_BLOCKS, dtype=wz.dtype)
    return jnp.einsum('hij,hg->higj', wz, eye).reshape(D_WIDTH, D_WIDTH)


def rglru_pallas(xd, gd, conv_w, conv_b, w_r, b_r, w_i, b_i, lam, s0):
    cb = conv_b.reshape(1, D_WIDTH)
    outs = []
    for z in range(2):
        wbd = jnp.concatenate([_block_diag(w_r[z]), _block_diag(w_i[z])], axis=1).astype(BF16)
        bias = jnp.concatenate([b_r[z], b_i[z]]).reshape(1, 2 * D_WIDTH)
        outs.append((wbd, bias, lam[z].reshape(1, D_WIDTH)))
    hf, sf = _rglru_sweep(xd, conv_w, cb, *outs[0], s0[0][:, None, :], reverse=False)
    y, sb = _rglru_sweep(xd, conv_w, cb, *outs[1], s0[1][:, None, :], hf, gd, reverse=True)
    return y, (sf[:, 0], sb[:, 0])


AFF_BITS = 31


def _lane_cumsum_exclusive(x01):
    e_, n = x01.shape
    li = lax.broadcasted_iota(jnp.int32, (LANES, LANES), 0)
    lj = lax.broadcasted_iota(jnp.int32, (LANES, LANES), 1)
    strict_upper = (li < lj).astype(BF16)
    ones = jnp.ones((LANES, LANES), BF16)
    base = jnp.zeros((e_, LANES), F32)
    outs = []
    for g in range(n // LANES):
        xg = x01[:, g * LANES:(g + 1) * LANES].astype(BF16)
        outs.append(base + _dot(xg, strict_upper))
        base = base + _dot(xg, ones)
    return jnp.concatenate(outs, axis=1)


def _route_kernel(aff_ref, idx_ref, gate_ref, sp_s, af_s, ps_s, pe_s, *, cap):
    aff = aff_ref[0]
    e_, n = aff.shape
    n_groups = n // LANES
    rb_rows = min(cap, LANES)
    bits = pltpu.bitcast(aff, jnp.int32)
    thr = jnp.zeros((e_, 1), jnp.int32)
    for bit in range(AFF_BITS - 1, -1, -1):
        cand = thr | (1 << bit)
        cnt = jnp.sum((bits >= cand).astype(F32), axis=1, keepdims=True)
        thr = jnp.where(cnt >= cap, cand, thr)
    gt = bits > thr
    eq = bits == thr
    need = cap - jnp.sum(gt.astype(F32), axis=1, keepdims=True)
    sel = gt | (eq & (_lane_cumsum_exclusive(eq.astype(F32)) < need))
    pos = _lane_cumsum_exclusive(sel.astype(F32))
    selpos = jnp.where(sel, pos, -1.0)
    tok_g = lax.broadcasted_iota(jnp.int32, (n, n_groups), 0) // LANES
    grp = lax.broadcasted_iota(jnp.int32, (n, n_groups), 1)
    cnt_g = _dot(sel.astype(BF16), (tok_g == grp).astype(BF16))
    gi = lax.broadcasted_iota(jnp.int32, (n_groups, n_groups), 0)
    gj = lax.broadcasted_iota(jnp.int32, (n_groups, n_groups), 1)
    start_g = _dot(cnt_g, (gi < gj).astype(F32), HI)
    ps_s[...] = start_g
    pe_s[...] = start_g + cnt_g
    for g in range(n_groups):
        sp_s[g] = selpos[:, g * LANES:(g + 1) * LANES]
        af_s[g] = aff[:, g * LANES:(g + 1) * LANES]
    slot0 = lax.broadcasted_iota(jnp.int32, (rb_rows, LANES), 0).astype(F32)
    lane = lax.broadcasted_iota(jnp.int32, (1, LANES), 1).astype(F32)

    def per_block(k, carry):
        e = k // (cap // rb_rows)
        rb = k % (cap // rb_rows)
        first = lax.convert_element_type(rb * rb_rows, F32)
        slot = slot0 + first

        def per_group(g, acc):
            acc_i, acc_g = acc
            hit = sp_s[g, pl.ds(e, 1), :] == slot
            tok = lane + lax.convert_element_type(g * LANES, F32)
            return (acc_i + jnp.where(hit, tok, 0.0), acc_g + jnp.where(hit, af_s[g, pl.ds(e, 1), :], 0.0))

        g_lo = jnp.sum((pe_s[pl.ds(e, 1), :] <= first).astype(F32), axis=1, keepdims=True)[0, 0].astype(jnp.int32)
        g_hi = jnp.sum((ps_s[pl.ds(e, 1), :] < first + rb_rows).astype(F32), axis=1, keepdims=True)[0, 0].astype(jnp.int32)
        zero = jnp.zeros((rb_rows, LANES), F32)
        acc_i, acc_g = lax.fori_loop(g_lo, g_hi, per_group, (zero, zero))
        rows = pl.ds(pl.multiple_of(rb * rb_rows, rb_rows), rb_rows)
        idx_ref[0, e, rows, :] = jnp.sum(acc_i, axis=1, keepdims=True).astype(jnp.int32)
        gate_ref[0, e, rows, :] = jnp.sum(acc_g, axis=1, keepdims=True)
        return carry

    lax.fori_loop(0, e_ * (cap // rb_rows), per_block, 0)


def route(aff):
    b_, e_, n = aff.shape
    cap = max(1, EC_FACTOR * n // N_EXPERTS)
    grp_scr = lambda: pltpu.VMEM((n // LANES, e_, LANES), F32)
    cnt_scr = lambda: pltpu.VMEM((e_, n // LANES), F32)
    idx, gate = pl.pallas_call(
        functools.partial(_route_kernel, cap=cap),
        grid=(b_,),
        in_specs=[pl.BlockSpec((1, e_, n), lambda b: (b, 0, 0))],
        out_specs=[pl.BlockSpec((1, e_, cap, 1), lambda b: (b, 0, 0, 0)),
                   pl.BlockSpec((1, e_, cap, 1), lambda b: (b, 0, 0, 0))],
        out_shape=[jax.ShapeDtypeStruct((b_, e_, cap, 1), jnp.int32), jax.ShapeDtypeStruct((b_, e_, cap, 1), F32)],
        scratch_shapes=[grp_scr(), grp_scr(), cnt_scr(), cnt_scr()],
        compiler_params=_params(1),
        name="route",
    )(aff)
    return idx.reshape(b_, e_, 1, cap), gate


ROW_UNROLL = 8


def _gather_kernel(idx_ref, h_ref, o_ref, *, cap):
    def body(r, carry):
        o_ref[0, 0, pl.ds(r, 1), :] = h_ref[0, pl.ds(idx_ref[0, 0, 0, r], 1), :]
        return carry
    lax.fori_loop(0, cap, body, 0, unroll=ROW_UNROLL)


def gather_rows(h, idx):
    b_, n, d = h.shape
    _, e_, _, cap = idx.shape
    return pl.pallas_call(
        functools.partial(_gather_kernel, cap=cap),
        grid=(b_, e_),
        in_specs=[pl.BlockSpec((1, 1, 1, cap), lambda b, e: (b, e, 0, 0), memory_space=pltpu.SMEM),
                  pl.BlockSpec((1, n, d), lambda b, e: (b, 0, 0), pipeline_mode=pl.Buffered(1))],
        out_specs=pl.BlockSpec((1, 1, cap, d), lambda b, e: (b, e, 0, 0)),
        out_shape=jax.ShapeDtypeStruct((b_, e_, cap, d), h.dtype),
        compiler_params=_params(2),
        name="gather_rows",
    )(idx, h)


def _expert_ffn_kernel(*refs, n_lat, with_ctx):
    if with_ctx:
        (x_ref, g_ref, g2_ref, xc_ref, gc_ref, g2c_ref, w1_ref, w3_ref, w2_ref, o_ref, oc_ref, w1b, w3b, w2b) = refs
    else:
        (x_ref, g_ref, g2_ref, w1_ref, w3_ref, w2_ref, o_ref, w1b, w3b, w2b) = refs
    m = pl.program_id(2)

    @pl.when((pl.program_id(1) == 0) & (m == 0))
    def _():
        w1b[...] = w1_ref[0].astype(BF16)
        w3b[...] = w3_ref[0].astype(BF16)
        w2b[...] = w2_ref[0].astype(BF16)

    def ffn(x):
        hid = (_silu(_dot(x, w1b[...])) * _dot(x, w3b[...])).astype(BF16)
        return _dot(hid, w2b[...])

    def latent():
        o_ref[0, 0] = ffn(_unpack_halves(x_ref[0, 0])) * g_ref[0, 0] * g2_ref[0]

    def context():
        n_s, cc = xc_ref.shape[0], xc_ref.shape[2]
        y = ffn(jnp.concatenate([_unpack_halves(xc_ref[s, 0]) for s in range(n_s)], axis=0))
        for s in range(n_s):
            oc_ref[s, 0] = y[s * cc:(s + 1) * cc] * gc_ref[s, 0] * g2c_ref[s]

    if with_ctx:
        pl.when(m < n_lat)(latent)
        pl.when((m == n_lat) & (pl.program_id(1) == 0))(context)
    else:
        latent()


def expert_ffn(xs, gate, g2, w1, w3, w2, layer, ctx=None):
    b_, e_, c_, half = xs.shape
    d = 2 * half
    f = w1.shape[-1]
    tm = min(c_, 512)
    n_lat = c_ // tm
    lat = lambda e, b, m: (b, e, jnp.minimum(m, n_lat - 1), 0)
    vec = pl.BlockSpec((1, 1, d), lambda e, b, m: (b, 0, 0))
    in_specs = [pl.BlockSpec((1, 1, tm, half), lat), pl.BlockSpec((1, 1, tm, 1), lat), vec]
    args = [xs, gate, g2[:, None, :]]
    out_specs = [pl.BlockSpec((1, 1, tm, d), lat)]
    out_shape = [jax.ShapeDtypeStruct((b_, e_, c_, d), F32)]
    if ctx is not None:
        xs_c, gate_c, g2_c = ctx
        cc = xs_c.shape[2]
        one = lambda e, b, m: (0, e, 0, 0)
        in_specs += [pl.BlockSpec((b_, 1, cc, half), one), pl.BlockSpec((b_, 1, cc, 1), one),
                     pl.BlockSpec((b_, 1, d), lambda e, b, m: (0, 0, 0))]
        args += [xs_c, gate_c, g2_c[:, None, :]]
        out_specs.append(pl.BlockSpec((b_, 1, cc, d), one))
        out_shape.append(jax.ShapeDtypeStruct((b_, e_, cc, d), F32))
    wspec = lambda r, c: pl.BlockSpec((None, 1, r, c), lambda e, b, m: (layer, e, 0, 0))
    outs = pl.pallas_call(
        functools.partial(_expert_ffn_kernel, n_lat=n_lat, with_ctx=ctx is not None),
        grid=(e_, b_, n_lat + (ctx is not None)),
        scratch_shapes=[pltpu.VMEM((d, f), BF16), pltpu.VMEM((d, f), BF16), pltpu.VMEM((f, d), BF16)],
        in_specs=in_specs + [wspec(d, f), wspec(d, f), wspec(f, d)],
        out_specs=out_specs,
        out_shape=out_shape,
        compiler_params=_params(3),
        name="expert_ffn",
    )(*args, w1, w3, w2)
    return (outs[0], outs[1]) if ctx is not None else (outs[0], None)


def _combine_kernel(idx_ref, y_ref, x1_hbm, o_hbm, acc, sem, *, cap, n_experts):
    b = pl.program_id(0)
    e = pl.program_id(1)

    @pl.when(e == 0)
    def _():
        cp = pltpu.make_async_copy(x1_hbm.at[b], acc, sem.at[0])
        cp.start()
        cp.wait()

    def body(g, carry):
        r0 = g * ROW_UNROLL
        rows = [pl.ds(idx_ref[0, 0, 0, r0 + j], 1) for j in range(ROW_UNROLL)]
        vals = [acc[row, :] + y_ref[0, 0, pl.ds(r0 + j, 1), :] for j, row in enumerate(rows)]
        for row, val in zip(rows, vals):
            acc[row, :] = val
        return carry
    lax.fori_loop(0, cap // ROW_UNROLL, body, 0)

    @pl.when(e == n_experts - 1)
    def _():
        cp = pltpu.make_async_copy(acc, o_hbm.at[b], sem.at[1])
        cp.start()
        cp.wait()


def combine_rows(x1, y, idx):
    b_, n, d = x1.shape
    _, e_, _, cap = idx.shape
    return pl.pallas_call(
        functools.partial(_combine_kernel, cap=cap, n_experts=e_),
        grid=(b_, e_),
        in_specs=[pl.BlockSpec((1, 1, 1, cap), lambda b, e: (b, e, 0, 0), memory_space=pltpu.SMEM),
                  pl.BlockSpec((1, 1, cap, d), lambda b, e: (b, e, 0, 0)),
                  pl.BlockSpec(memory_space=pl.ANY)],
        out_specs=pl.BlockSpec(memory_space=pl.ANY),
        out_shape=jax.ShapeDtypeStruct((b_, n, d), F32),
        scratch_shapes=[pltpu.VMEM((n, d), F32), pltpu.SemaphoreType.DMA((2,))],
        compiler_params=_params(2),
        name="combine_rows",
    )(idx, y, x1)


def moe_residual(lat, ctx, w1, w3, w2, layer):
    routed = []
    for x1, h2, aff, g2 in (lat,) if ctx is None else (lat, ctx):
        idx, gate = route(aff)
        routed.append((x1, idx, gather_rows(h2, idx), gate, g2))
    (x1, idx, xs, gate, g2) = routed[0]
    y, y_c = expert_ffn(xs, gate, g2, w1, w3, w2, layer, ctx=None if ctx is None else routed[1][2:])
    out = combine_rows(x1, y, idx)
    return out, (None if ctx is None else combine_rows(routed[1][0], y_c, routed[1][1]))


def kernel(x, c, ctx, c_ctx, w_mod, b_mod, norm1_w, norm2_w, final_norm_w,
           ev_w_in, ev_w_out, a_conv_w, a_log, a_dt_bias, a_norm_w, b_lb_logits, b_norm_w,
           od_w_in, od_w_out, c_sink, d_conv_w, d_conv_b, d_w_r, d_b_r, d_w_i, d_b_i, d_lambda,
           moe_router, moe_w1, moe_w3, moe_w2):
    b_, n, d = x.shape
    cos, sin = _rope_tables(n)
    lb_all = jnp.cumsum(jax.nn.softmax(b_lb_logits.astype(F32), axis=0), axis=0)
    lb_all = lb_all - lb_all[0:1]
    c_rows = jnp.concatenate([c, c_ctx[None, :], jnp.zeros((-(b_ + 1) % SUBLANES, d), F32)], axis=0)
    mod_all = modulation(c_rows, w_mod, b_mod)
    for l in range(DEPTH):
        last = l == DEPTH - 1
        j = l // 2
        router_t = moe_router[l].T.astype(BF16)
        mod = mod_all[l, :b_]
        mod_c = jnp.broadcast_to(mod_all[l, b_], (b_, 6 * d))
        sh1, sc1, g1, sh2, sc2, g2 = jnp.split(mod, 6, axis=-1)
        csh1, csc1, cg1, csh2, csc2, cg2 = jnp.split(mod_c, 6, axis=-1)
        if l % 2 == 0:
            w_in, w_out = _even_w_in(ev_w_in[j]), ev_w_out[j].astype(BF16)
            zero = jnp.zeros((b_, A_HEADS, A_DK, A_DV), F32)
            pars = (a_conv_w[j], a_log[j], a_dt_bias[j], a_norm_w[j], lb_all[j].reshape(2, HB), b_norm_w[j])
            o_c, st = even_mixer_pallas(in_proj(ctx, norm1_w[l], csh1, csc1, w_in, EVEN_GROUPS), *pars,
                                        ((zero, zero), (zero, zero)))
            o_l, _ = even_mixer_pallas(in_proj(x, norm1_w[l], sh1, sc1, w_in, EVEN_GROUPS), *pars, st)
            parts_l, parts_c = [o_l], [o_c]
        else:
            w_in, w_out = od_w_in[j].astype(BF16), od_w_out[j].astype(BF16)
            qc, kc, vc, xdc, gdc = in_proj(ctx, norm1_w[l], csh1, csc1, w_in, ODD_SIZES)
            ql, kl, vl, xdl, gdl = in_proj(x, norm1_w[l], sh1, sc1, w_in, ODD_SIZES)
            att_l = window_attention_pallas(ql, kl, vl, kc, vc, c_sink[j], cos, sin)
            rg_pars = (d_conv_w[j], d_conv_b[j], d_w_r[j], d_b_r[j], d_w_i[j], d_b_i[j], d_lambda[j])
            zero = jnp.zeros((b_, D_WIDTH), F32)
            rg_c, st = rglru_pallas(xdc, gdc, *rg_pars, (zero, zero))
            rg_l, _ = rglru_pallas(xdl, gdl, *rg_pars, st)
            parts_l = [att_l, rg_l]
            if not last:
                parts_c = [context_attention_pallas(qc, kc, vc, c_sink[j]), rg_c]
        lat = tuple(post_mixer(parts_l, w_out, x, g1, norm2_w[l], sh2, sc2, router_t)) + (g2,)
        con = None if last else tuple(post_mixer(parts_c, w_out, ctx, cg1, norm2_w[l], csh2, csc2, router_t)) + (cg2,)
        x, ctx = moe_residual(lat, con, moe_w1, moe_w3, moe_w2, l)
    return final_norm(x, final_norm_w)
```

```python
import functools

import jax
import jax.numpy as jnp
from jax import lax
from jax.experimental import pallas as pl
from jax.experimental.pallas import tpu as pltpu

D_MODEL = 1024
DEPTH = 4
GRID_W = 64
EPS = 1e-6
F32 = jnp.float32
BF16 = jnp.bfloat16
HI = lax.Precision.HIGHEST

A_HEADS = 4
A_DK = 128
A_DV = 128
A_CONV = 4
A_CHUNK = 64
A_QKV = 2 * A_HEADS * A_DK + A_HEADS * A_DV
B_HEADS = 4
B_DK = 128
B_DV = 128
B_CHUNK = 64
C_HEADS = 8
C_KV_HEADS = 2
C_HD = 64
C_WIN = 128
C_BLOCK = 128
ROPE_THETA = 10000.0
D_WIDTH = 512
D_BLOCKS = 8
D_BW = D_WIDTH // D_BLOCKS
D_CONV = 4
D_C = 8.0
N_EXPERTS = 16
EXPERT_FF = 1024
EC_FACTOR = 2

EVEN_SIZES = (A_QKV, A_HEADS * A_DV, 2 * A_HEADS, 2 * A_HEADS,
              B_HEADS * B_DK, B_HEADS * B_DV, 2 * B_HEADS * B_DK, B_HEADS * B_DV)
ODD_SIZES = (C_HEADS * C_HD, C_KV_HEADS * C_HD, C_KV_HEADS * C_HD, D_WIDTH, D_WIDTH)

LANES = 128
SUBLANES = 8
VMEM_LIMIT_BYTES = 56 * 1024 * 1024

ROW_TILE = 512
HALO = SUBLANES


def _cuts(sizes):
    out, acc = [], 0
    for s in sizes[:-1]:
        acc += s
        out.append(acc)
    return out


def _params(n_axes):
    return pltpu.CompilerParams(dimension_semantics=("arbitrary",) * n_axes, vmem_limit_bytes=VMEM_LIMIT_BYTES)


def _dot(a, b, precision=None):
    return jnp.dot(a, b, preferred_element_type=F32, precision=precision)


def _dot_nt(a, b, precision=None):
    return lax.dot_general(a, b, (((1,), (1,)), ((), ())), preferred_element_type=F32, precision=precision)


def _dot_tn(a, b):
    return lax.dot_general(a, b, (((0,), (0,)), ((), ())), preferred_element_type=F32)


def _silu(x):
    return x * jax.nn.sigmoid(x)


def _rms(x):
    return x * lax.rsqrt(jnp.mean(x * x, axis=-1, keepdims=True) + EPS)


U32 = jnp.uint32
HIGH16 = 0xFFFF0000


def _pack_halves(h):
    half = h.shape[-1] // 2
    lo = pltpu.bitcast(h[:, :half].astype(BF16).astype(F32), U32)
    hi = pltpu.bitcast(h[:, half:].astype(BF16).astype(F32), U32)
    return (hi & jnp.uint32(HIGH16)) | lax.shift_right_logical(lo, jnp.uint32(16))


def _unpack_halves(p):
    lo = pltpu.bitcast(lax.shift_left(p, jnp.uint32(16)), F32).astype(BF16)
    hi = pltpu.bitcast(p & jnp.uint32(HIGH16), F32).astype(BF16)
    return jnp.concatenate([lo, hi], axis=1)


MOD_COLS = 1536


def _mod_kernel(c_ref, w_ref, b_ref, o_ref):
    o_ref[0] = _dot(_silu(c_ref[...]), w_ref[0]) + b_ref[0]


def modulation(c_rows, w_mod, b_mod):
    r, d = c_rows.shape
    depth, _, wide = w_mod.shape
    return pl.pallas_call(
        _mod_kernel,
        grid=(depth, wide // MOD_COLS),
        in_specs=[pl.BlockSpec((r, d), lambda l, j: (0, 0)),
                  pl.BlockSpec((1, d, MOD_COLS), lambda l, j: (l, 0, j)),
                  pl.BlockSpec((1, 1, MOD_COLS), lambda l, j: (l, 0, j))],
        out_specs=pl.BlockSpec((1, r, MOD_COLS), lambda l, j: (l, 0, j)),
        out_shape=jax.ShapeDtypeStruct((depth, r, wide), F32),
        compiler_params=_params(2),
        name="modulation",
    )(c_rows, w_mod, b_mod[:, None, :])


def _norm_mod(x, nw, shift, scale):
    return _rms(x) * nw * (1.0 + scale) + shift


def _in_proj_kernel(x_ref, nw_ref, sh_ref, sc_ref, w_ref, *o_refs, splits):
    h = _norm_mod(x_ref[0], nw_ref[...], sh_ref[0], sc_ref[0]).astype(BF16)
    off = 0
    for o_ref, s in zip(o_refs, splits):
        o_ref[0] = _dot(h, w_ref[:, off:off + s])
        off += s


def in_proj(x, norm_w, shift, scale, w, splits):
    b_, n, d = x.shape
    tm = min(n, ROW_TILE)
    ntot = sum(splits)
    vec = pl.BlockSpec((1, 1, d), lambda b, i: (b, 0, 0))
    return pl.pallas_call(
        functools.partial(_in_proj_kernel, splits=tuple(splits)),
        grid=(b_, n // tm),
        in_specs=[pl.BlockSpec((1, tm, d), lambda b, i: (b, i, 0)),
                  pl.BlockSpec((1, d), lambda b, i: (0, 0)), vec, vec,
                  pl.BlockSpec((d, ntot), lambda b, i: (0, 0))],
        out_specs=[pl.BlockSpec((1, tm, s), lambda b, i: (b, i, 0)) for s in splits],
        out_shape=[jax.ShapeDtypeStruct((b_, n, s), F32) for s in splits],
        compiler_params=_params(2),
        name="in_proj",
    )(x, norm_w.reshape(1, d), shift[:, None, :], scale[:, None, :], w)


def _post_mixer_kernel(*refs, n_parts):
    parts = refs[:n_parts]
    w_ref, x_ref, g1_ref, nw_ref, sh_ref, sc_ref, rt_ref, x1_ref, h2_ref, aff_ref = refs[n_parts:]
    y = None
    off = 0
    for p_ref in parts:
        k = p_ref.shape[-1]
        t = _dot(p_ref[0].astype(BF16), w_ref[off:off + k, :])
        y = t if y is None else y + t
        off += k
    x1 = x_ref[0] + g1_ref[0] * y
    x1_ref[0] = x1
    h2 = _norm_mod(x1, nw_ref[...], sh_ref[0], sc_ref[0])
    h2_ref[0] = _pack_halves(h2)
    logits = _dot_nt(rt_ref[...], h2.astype(BF16))
    e = jnp.exp(logits - jnp.max(logits, axis=0, keepdims=True))
    aff_ref[0] = e / jnp.sum(e, axis=0, keepdims=True)


def post_mixer(parts, w_out, x, g1, norm_w, shift, scale, router_t):
    b_, n, d = x.shape
    e_ = router_t.shape[0]
    tm = min(n, ROW_TILE)
    vec = pl.BlockSpec((1, 1, d), lambda b, i: (b, 0, 0))
    tok = lambda k: pl.BlockSpec((1, tm, k), lambda b, i: (b, i, 0))
    return pl.pallas_call(
        functools.partial(_post_mixer_kernel, n_parts=len(parts)),
        grid=(b_, n // tm),
        in_specs=[tok(p.shape[-1]) for p in parts] + [
            pl.BlockSpec(w_out.shape, lambda b, i: (0, 0)), tok(d), vec,
            pl.BlockSpec((1, d), lambda b, i: (0, 0)), vec, vec,
            pl.BlockSpec((e_, d), lambda b, i: (0, 0))],
        out_specs=[tok(d), tok(d // 2), pl.BlockSpec((1, e_, tm), lambda b, i: (b, 0, i))],
        out_shape=[jax.ShapeDtypeStruct((b_, n, d), F32), jax.ShapeDtypeStruct((b_, n, d // 2), U32),
                   jax.ShapeDtypeStruct((b_, e_, n), F32)],
        compiler_params=_params(2),
        name="post_mixer",
    )(*parts, w_out, x, g1[:, None, :], norm_w.reshape(1, d), shift[:, None, :], scale[:, None, :], router_t)


def _final_kernel(x_ref, nw_ref, o_ref):
    o_ref[0] = _rms(x_ref[0]) * nw_ref[...]


def final_norm(x, norm_w):
    b_, n, d = x.shape
    tm = min(n, 2 * ROW_TILE)
    tok = pl.BlockSpec((1, tm, d), lambda b, i: (b, i, 0))
    return pl.pallas_call(
        _final_kernel,
        grid=(b_, n // tm),
        in_specs=[tok, pl.BlockSpec((1, d), lambda b, i: (0, 0))],
        out_specs=tok,
        out_shape=jax.ShapeDtypeStruct((b_, n, d), F32),
        compiler_params=_params(2),
        name="final_norm",
    )(x, norm_w.reshape(1, d))


CHUNK = A_CHUNK
SUB = 16
EVEN_TILE = 256
HA = A_HEADS * A_DK
HB = B_HEADS * B_DK
AB_PAD = LANES
EVEN_GROUPS = (A_QKV, A_HEADS * A_DV, AB_PAD, HB, B_HEADS * B_DV, 2 * HB, B_HEADS * B_DV)
NEUMANN_STEPS = 5


def _split2(x):
    hi = x.astype(BF16)
    return hi, (x - hi.astype(F32)).astype(BF16)


def _dot_split(a, b):
    (ah, al), (bh, bl) = a, b
    return _dot(ah, bh) + _dot(ah, bl) + _dot(al, bh)


def _dot_exact_lhs(a_bf16, x):
    x1 = x.astype(BF16)
    r1 = x - x1.astype(F32)
    x2 = r1.astype(BF16)
    x3 = (r1 - x2.astype(F32)).astype(BF16)
    return _dot(a_bf16, x1) + _dot(a_bf16, x2) + _dot(a_bf16, x3)


def _even_kernel(*refs, tile, n_tiles, reverse, combine):
    if combine:
        (qkvn_ref, ab_ref, qb_ref, ib_ref, fb_ref, exp_ref, alog_ref, dtb_ref, lb_ref,
         sa0_ref, sb0_ref, of_ref, ga_ref, gb_ref, anw_ref, bnw_ref,
         o_ref, sa_ref, sb_ref, g_scr, st_a, st_b) = refs
    else:
        (qkv_ref, qkvp_ref, qkvnx_ref, ab_ref, qb_ref, ib_ref, fb_ref, cw_ref, exp_ref, alog_ref, dtb_ref, lb_ref,
         sa0_ref, sb0_ref,
         o_ref, qkvn_ref, sa_ref, sb_ref, xbuf, g_scr, st_a, st_b) = refs
    i = pl.program_id(1)
    t = (n_tiles - 1 - i) if reverse else i

    @pl.when(i == 0)
    def _():
        st_a[...] = sa0_ref[0]
        st_b[...] = sb0_ref[0]

    L = CHUNK
    n_chunks = tile // L
    n_blocks = tile // SUB
    chunk_order = range(n_chunks - 1, -1, -1) if reverse else range(n_chunks)
    block_order = range(n_blocks - 1, -1, -1) if reverse else range(n_blocks)

    ti = lax.broadcasted_iota(jnp.int32, (tile, tile), 0)
    tj = lax.broadcasted_iota(jnp.int32, (tile, tile), 1)
    t_incl = (tj >= ti) if reverse else (tj <= ti)
    tri_chunk = (t_incl & ((ti // L) == (tj // L))).astype(BF16)
    tri_blk = (t_incl & ((ti // SUB) == (tj // SUB))).astype(BF16)
    ones_blk = ((ti // SUB) == (tj // SUB)).astype(BF16)
    ii = lax.broadcasted_iota(jnp.int32, (L, L), 0)
    jj = lax.broadcasted_iota(jnp.int32, (L, L), 1)
    incl = (jj >= ii) if reverse else (jj <= ii)
    strict = (jj > ii) if reverse else (jj < ii)
    eye = (ii == jj).astype(F32)
    last_row = 0 if reverse else L - 1

    if not combine:
        xbuf[pl.ds(0, HALO), :] = jnp.where(t > 0, qkvp_ref[0], 0.0)
        xbuf[pl.ds(HALO, tile), :] = qkv_ref[0]
        xbuf[pl.ds(HALO + tile, HALO), :] = jnp.where(t < n_tiles - 1, qkvnx_ref[0], 0.0)
        conv = _silu(sum(cw_ref[pl.ds(k, 1), :] * xbuf[pl.ds(HALO - 1 + k, tile), :] for k in range(A_CONV)))
    ab_x = _dot(ab_ref[0], exp_ref[...], HI)
    g_all = -jnp.exp(alog_ref[...]) * jax.nn.softplus(ab_x[:, :HA] + dtb_ref[...])
    beta_all = jax.nn.sigmoid(ab_x[:, HA:])
    cum_all = _dot_exact_lhs(tri_chunk, g_all)
    ecum_all = jnp.exp(cum_all)

    chains = [(c, h) for c in chunk_order for h in range(A_HEADS)]
    pre = {}
    neg_m = []
    for c, h in chains:
        rs = slice(c * L, (c + 1) * L)
        hs = slice(h * A_DK, (h + 1) * A_DK)
        ks_, vs_ = slice(HA + h * A_DK, HA + (h + 1) * A_DK), slice(2 * HA + h * A_DV, 2 * HA + (h + 1) * A_DV)
        if combine:
            q, k, v = qkvn_ref[0, rs, hs], qkvn_ref[0, rs, ks_], qkvn_ref[0, rs, vs_]
        else:
            qh, kh, v = conv[rs, hs], conv[rs, ks_], conv[rs, vs_]
            q = qh * (lax.rsqrt(jnp.sum(qh * qh, axis=-1, keepdims=True) + EPS) * (A_DK ** -0.5))
            k = kh * lax.rsqrt(jnp.sum(kh * kh, axis=-1, keepdims=True) + EPS)
            qkvn_ref[0, rs, hs], qkvn_ref[0, rs, ks_], qkvn_ref[0, rs, vs_] = q, k, v
        beta, cum, ecum = beta_all[rs, hs], cum_all[rs, hs], ecum_all[rs, hs]
        r_ll = cum.T[:L, :]
        decay = jnp.where(incl, jnp.exp(jnp.where(incl, cum[:, :L] - r_ll, 0.0)), 0.0)
        kb = k * beta
        k_t = k.T.astype(BF16)
        neg_m.append(jnp.where(strict, -_dot(kb.astype(BF16), k_t) * decay, 0.0))
        qk = _dot(q.astype(BF16), k_t) * decay
        total = cum[last_row:last_row + 1, :]
        pre[(c, h)] = dict(rhs=jnp.concatenate([v * beta, kb * ecum], axis=1), qk=qk.astype(BF16),
                           qd=(q * ecum).astype(BF16), kd=(k * jnp.exp(total - cum)).astype(BF16),
                           gl=jnp.exp(total))
    p = [_split2(m) for m in neg_m]
    tinv = [eye + m for m in neg_m]
    for _ in range(NEUMANN_STEPS):
        p = [_split2(_dot_split(x, x)) for x in p]
        tinv = [tv + _dot_split(_split2(tv), x) for tv, x in zip(tinv, p)]
    sol_of = {ch: _dot_split(_split2(tv), _split2(pre[ch]["rhs"])) for tv, ch in zip(tinv, chains)}

    for c in chunk_order:
        rs = slice(c * L, (c + 1) * L)
        s_old = [st_a[h] for h in range(A_HEADS)]
        s16 = [s.astype(BF16) for s in s_old]
        ws = [_dot(sol_of[(c, h)][:, A_DV:].astype(BF16), s16[h]) for h in range(A_HEADS)]
        qs_ = [_dot(pre[(c, h)]["qd"], s16[h]) for h in range(A_HEADS)]
        v_new = [(sol_of[(c, h)][:, :A_DV] - ws[h]).astype(BF16) for h in range(A_HEADS)]
        for h in range(A_HEADS):
            o_ref[0, rs, h * A_DV:(h + 1) * A_DV] = qs_[h] + _dot(pre[(c, h)]["qk"], v_new[h])
            st_a[h] = s_old[h] * pre[(c, h)]["gl"] + _dot_tn(pre[(c, h)]["kd"], v_new[h])

    hq = _silu(qb_ref[0])
    hv = ib_ref[0]
    lb = lb_ref[...]
    fg = lb + (1.0 - lb) * jax.nn.sigmoid(fb_ref[0])
    lf = jnp.log(fg)
    hk = 1.0 - fg
    b_all = _dot_exact_lhs(tri_blk, lf)
    e_all = _dot_exact_lhs(ones_blk, lf)
    qs_all = (hq * jnp.exp(b_all)).astype(BF16)
    ks_all = (hk * jnp.exp(e_all - b_all)).astype(BF16)
    hv16 = hv.astype(BF16)
    row_in_blk = lax.broadcasted_iota(jnp.int32, (tile, HB), 0) % SUB
    od = [jnp.zeros((tile, B_DV), F32) for _ in range(B_HEADS)]
    for s_ in range(SUB):
        valid = (row_in_blk + s_ < SUB) if reverse else (row_in_blk >= s_)
        shift = ((tile - s_) if reverse else s_) % tile
        roll = (lambda x: x) if s_ == 0 else (lambda x: pltpu.roll(x, shift, axis=0))
        prod = jnp.where(valid, hq * roll(hk) * jnp.exp(b_all - roll(b_all)), 0.0)
        vv = roll(hv)
        for h in range(B_HEADS):
            hs = slice(h * B_DK, (h + 1) * B_DK)
            od[h] = od[h] + jnp.sum(prod[:, hs], axis=-1, keepdims=True) * vv[:, hs]
    for blk in block_order:
        bs = slice(blk * SUB, (blk + 1) * SUB)
        for h in range(B_HEADS):
            hs = slice(h * B_DK, (h + 1) * B_DK)
            g_scr[blk * B_HEADS + h] = _dot_tn(hv16[bs, hs], ks_all[bs, hs])

    st = [st_b[h] for h in range(B_HEADS)]
    for blk in block_order:
        bs = slice(blk * SUB, (blk + 1) * SUB)
        for h in range(B_HEADS):
            hs = slice(h * B_DK, (h + 1) * B_DK)
            o_ref[0, bs, HA + h * B_DV:HA + (h + 1) * B_DV] = _dot_nt(qs_all[bs, hs], st[h].astype(BF16)) + od[h][bs]
            st[h] = st[h] * jnp.exp(e_all[blk * SUB:blk * SUB + 1, hs]) + g_scr[blk * B_HEADS + h]
    for h in range(B_HEADS):
        st_b[h] = st[h]

    if combine:
        tot = of_ref[0] + o_ref[0]
        for h in range(A_HEADS + B_HEADS):
            hs = slice(h * A_DV, (h + 1) * A_DV)
            nw = anw_ref[...] if h < A_HEADS else bnw_ref[...]
            gate = ga_ref[0, :, hs] if h < A_HEADS else gb_ref[0, :, pl.ds((h - A_HEADS) * B_DV, B_DV)]
            o_ref[0, :, hs] = _rms(tot[:, hs]) * nw * _silu(gate)

    @pl.when(i == n_tiles - 1)
    def _():
        sa_ref[0] = st_a[...]
        sb_ref[0] = st_b[...]


def _even_sweep(outs, z, conv_w, a_log, dt_bias, lb, s_a, s_b, of=None, a_norm_w=None, b_norm_w=None):
    qkv, ga, ab, qb, ib, fb, gb = outs
    b_, n, _ = qkv.shape
    reverse = z == 1
    combine = of is not None
    tile = min(n, EVEN_TILE)
    n_tiles = n // tile
    bpt = tile // HALO
    nblk = n // HALO
    tidx = lambda i: (n_tiles - 1 - i) if reverse else i
    tmap = lambda b, i: (b, tidx(i), 0)
    pmap = lambda b, i: (b, jnp.maximum(tidx(i) * bpt - 1, 0), 0)
    nmap = lambda b, i: (b, jnp.minimum((tidx(i) + 1) * bpt, nblk - 1), 0)
    zmap = lambda b, i: (b, tidx(i), z)
    const2 = lambda b, i: (0, 0)
    smap = lambda b, i: (b, 0, 0, 0)
    src = lax.broadcasted_iota(jnp.int32, (AB_PAD, 2 * HA), 0)
    dst = lax.broadcasted_iota(jnp.int32, (AB_PAD, 2 * HA), 1)
    expand = (src == jnp.where(dst >= HA, 2 * A_HEADS, 0) + z * A_HEADS + (dst % HA) // A_DK).astype(F32)
    rep = lambda p_: jnp.broadcast_to(p_[:, None], (A_HEADS, A_DK)).reshape(1, HA)
    tok = lambda w: pl.BlockSpec((1, tile, w), tmap)
    common_specs = [tok(AB_PAD), tok(HB), tok(HB), pl.BlockSpec((1, tile, HB), zmap)]
    common_args = [ab, qb, ib, fb]
    par_specs = [pl.BlockSpec((AB_PAD, 2 * HA), const2),
                 pl.BlockSpec((1, HA), const2), pl.BlockSpec((1, HA), const2), pl.BlockSpec((1, HB), const2),
                 pl.BlockSpec((1, A_HEADS, A_DK, A_DV), smap), pl.BlockSpec((1, B_HEADS, B_DV, B_DK), smap)]
    par_args = [expand, rep(a_log[z]), rep(dt_bias[z]), lb[z].reshape(1, HB), s_a, s_b]
    state_specs = [pl.BlockSpec((1, A_HEADS, A_DK, A_DV), smap), pl.BlockSpec((1, B_HEADS, B_DV, B_DK), smap)]
    state_shapes = [jax.ShapeDtypeStruct(s_a.shape, F32), jax.ShapeDtypeStruct(s_b.shape, F32)]
    scratch = [pltpu.VMEM((tile // SUB * B_HEADS, B_DV, B_DK), F32),
               pltpu.VMEM((A_HEADS, A_DK, A_DV), F32), pltpu.VMEM((B_HEADS, B_DV, B_DK), F32)]
    if combine:
        in_specs = [tok(A_QKV)] + common_specs + par_specs + [
            tok(HA + HB), tok(HA), tok(HB), pl.BlockSpec((1, A_DV), const2), pl.BlockSpec((1, B_DV), const2)]
        args = [qkv] + common_args + par_args + [of, ga, gb, a_norm_w.reshape(1, A_DV), b_norm_w.reshape(1, B_DV)]
        out_specs = [tok(HA + HB)] + state_specs
        out_shape = [jax.ShapeDtypeStruct((b_, n, HA + HB), F32)] + state_shapes
    else:
        in_specs = [tok(A_QKV), pl.BlockSpec((1, HALO, A_QKV), pmap), pl.BlockSpec((1, HALO, A_QKV), nmap)] + \
            common_specs + [pl.BlockSpec((A_CONV, A_QKV), const2)] + par_specs
        args = [qkv, qkv, qkv] + common_args + [conv_w] + par_args
        out_specs = [tok(HA + HB), tok(A_QKV)] + state_specs
        out_shape = [jax.ShapeDtypeStruct((b_, n, HA + HB), F32), jax.ShapeDtypeStruct((b_, n, A_QKV), F32)] + state_shapes
        scratch = [pltpu.VMEM((tile + 2 * HALO, A_QKV), F32)] + scratch
    return pl.pallas_call(
        functools.partial(_even_kernel, tile=tile, n_tiles=n_tiles, reverse=reverse, combine=combine),
        grid=(b_, n_tiles),
        in_specs=in_specs,
        out_specs=out_specs,
        out_shape=out_shape,
        scratch_shapes=scratch,
        compiler_params=_params(2),
        name="even_bwd_combine" if combine else "even_fwd",
    )(*args)


def even_mixer_pallas(outs, conv_w, a_log, dt_bias, a_norm_w, lb, b_norm_w, states):
    (saf, sbf), (sab, sbb) = states
    o_f, qkv_n, saf, sbf = _even_sweep(outs, 0, conv_w, a_log, dt_bias, lb, saf, sbf)
    outs_b = [qkv_n] + list(outs[1:])
    o, sab, sbb = _even_sweep(outs_b, 1, conv_w, a_log, dt_bias, lb, sab, sbb, o_f, a_norm_w, b_norm_w)
    return o, ((saf, sbf), (sab, sbb))


def _even_w_in(w):
    c = _cuts(EVEN_SIZES)
    ab = jnp.pad(w[:, c[1]:c[3]], ((0, 0), (0, AB_PAD - (c[3] - c[1]))))
    return jnp.concatenate([w[:, :c[1]], ab, w[:, c[3]:]], axis=1).astype(BF16)


ROPE_NF = C_HD // 4
NEG_BIG = -1e30


def _rope(t, cos, sin_signed):
    w = t.shape[-1]
    lane = lax.broadcasted_iota(jnp.int32, t.shape, 1)
    partner = jnp.where(lane % (2 * ROPE_NF) < ROPE_NF,
                        pltpu.roll(t, w - ROPE_NF, axis=1), pltpu.roll(t, ROPE_NF, axis=1))
    return t * cos + partner * sin_signed


def _dup_kv_head(t, kh):
    lane = lax.broadcasted_iota(jnp.int32, t.shape, 1)
    rolled = pltpu.roll(t, C_HD, axis=1)
    own_half = (lane < C_HD) if kh == 0 else (lane >= C_HD)
    return jnp.where(own_half, t, rolled)


def _attn_kernel(*refs, banded, n_blocks):
    if banded:
        (q_ref, kp_ref, k_ref, kn_ref, vp_ref, v_ref, vn_ref, cq_ref, sq_ref, cp_ref, sp_ref, cn_ref, sn_ref,
         kc_ref, vc_ref, sink_ref, o_ref) = refs
    else:
        q_ref, kc_ref, vc_ref, sink_ref, o_ref = refs
    nb = pl.program_id(1)
    blk = q_ref.shape[1]
    q = q_ref[0] * (C_HD ** -0.5)
    kc = kc_ref[0]
    vc = vc_ref[0]
    if banded:
        cq, sq = cq_ref[...], sq_ref[...]
        q = _rope(q, jnp.concatenate([cq] * (C_HEADS // C_KV_HEADS), axis=1),
                  jnp.concatenate([sq] * (C_HEADS // C_KV_HEADS), axis=1))
        kb = [_rope(kp_ref[0], cp_ref[...], sp_ref[...]), _rope(k_ref[0], cq, sq), _rope(kn_ref[0], cn_ref[...], sn_ref[...])]
        keys = jnp.concatenate(kb + [kc], axis=0)
        vals = jnp.concatenate([vp_ref[0], v_ref[0], vn_ref[0], vc], axis=0)
        qpos = lax.broadcasted_iota(jnp.int32, (blk, keys.shape[0]), 0)
        kcol = lax.broadcasted_iota(jnp.int32, (blk, keys.shape[0]), 1)
        rel = kcol - blk
        in_band = (jnp.abs(rel - qpos) <= C_WIN) & (rel + nb * blk >= 0) & (rel + nb * blk < n_blocks * blk)
        mask = in_band | (kcol >= 3 * blk)
    else:
        keys, vals, mask = kc, vc, None
    keys = keys.astype(BF16)
    vals = vals.astype(BF16)
    lane_q = lax.broadcasted_iota(jnp.int32, (blk, 2 * C_HD), 1)
    grp = C_HEADS // C_KV_HEADS
    heads = range(C_HEADS)
    k_dup = [_dup_kv_head(keys, kh) for kh in range(C_KV_HEADS)]
    v_dup = [_dup_kv_head(vals, kh) for kh in range(C_KV_HEADS)]
    qh = [jnp.where((lane_q < C_HD) if h % 2 == 0 else (lane_q >= C_HD),
                    q[:, (h // 2) * 2 * C_HD:(h // 2 + 1) * 2 * C_HD], 0.0).astype(BF16) for h in heads]
    s = [_dot_nt(qh[h], k_dup[h // grp]) for h in heads]
    if mask is not None:
        s = [jnp.where(mask, x, NEG_BIG) for x in s]
    m = [jnp.maximum(jnp.max(s[h], axis=-1, keepdims=True), sink_ref[h]) for h in heads]
    e = [jnp.exp(s[h] - m[h]) for h in heads]
    denom = [jnp.sum(e[h], axis=-1, keepdims=True) + jnp.exp(sink_ref[h] - m[h]) for h in heads]
    p = [(e[h] / denom[h]).astype(BF16) for h in heads]
    out = [_dot(p[h], v_dup[h // grp]) for h in heads]
    for pair in range(C_HEADS // 2):
        o_ref[0, :, pair * 2 * C_HD:(pair + 1) * 2 * C_HD] = jnp.where(lane_q < C_HD, out[2 * pair], out[2 * pair + 1])


def _rope_tables(n):
    tok = lax.iota(jnp.int32, n)
    row = (tok // GRID_W).astype(F32)
    col = (tok % GRID_W).astype(F32)
    inv = ROPE_THETA ** (-lax.iota(F32, ROPE_NF) / ROPE_NF)
    ang_r, ang_c = row[:, None] * inv, col[:, None] * inv
    cos = jnp.concatenate([jnp.cos(ang_r)] * 2 + [jnp.cos(ang_c)] * 2, axis=-1)
    sin = jnp.concatenate([-jnp.sin(ang_r), jnp.sin(ang_r), -jnp.sin(ang_c), jnp.sin(ang_c)], axis=-1)
    return jnp.concatenate([cos, cos], axis=-1), jnp.concatenate([sin, sin], axis=-1)


def window_attention_pallas(q, k, v, kc, vc, sink, cos, sin):
    b_, n, hq = q.shape
    hk = k.shape[-1]
    lc = kc.shape[1]
    blk = C_BLOCK
    nb_ = n // blk
    cur = lambda b, i: (b, i, 0)
    prv = lambda b, i: (b, jnp.maximum(i - 1, 0), 0)
    nxt = lambda b, i: (b, jnp.minimum(i + 1, nb_ - 1), 0)
    tcur = lambda b, i: (i, 0)
    tprv = lambda b, i: (jnp.maximum(i - 1, 0), 0)
    tnxt = lambda b, i: (jnp.minimum(i + 1, nb_ - 1), 0)
    kspec = lambda m: pl.BlockSpec((1, blk, hk), m)
    tspec = lambda m: pl.BlockSpec((blk, hk), m)
    cspec = pl.BlockSpec((1, lc, hk), lambda b, i: (b, 0, 0))
    return pl.pallas_call(
        functools.partial(_attn_kernel, banded=True, n_blocks=nb_),
        grid=(b_, nb_),
        in_specs=[pl.BlockSpec((1, blk, hq), cur), kspec(prv), kspec(cur), kspec(nxt),
                  kspec(prv), kspec(cur), kspec(nxt),
                  tspec(tcur), tspec(tcur), tspec(tprv), tspec(tprv), tspec(tnxt), tspec(tnxt),
                  cspec, cspec, pl.BlockSpec(memory_space=pltpu.SMEM)],
        out_specs=pl.BlockSpec((1, blk, hq), cur),
        out_shape=jax.ShapeDtypeStruct((b_, n, hq), F32),
        compiler_params=_params(2),
        name="window_attention",
    )(q, k, k, k, v, v, v, cos, sin, cos, sin, cos, sin, kc, vc, sink)


def context_attention_pallas(q, kc, vc, sink):
    b_, lc, hq = q.shape
    hk = kc.shape[-1]
    blk = min(lc, C_BLOCK)
    cspec = pl.BlockSpec((1, lc, hk), lambda b, i: (b, 0, 0))
    return pl.pallas_call(
        functools.partial(_attn_kernel, banded=False, n_blocks=lc // blk),
        grid=(b_, lc // blk),
        in_specs=[pl.BlockSpec((1, blk, hq), lambda b, i: (b, i, 0)), cspec, cspec,
                  pl.BlockSpec(memory_space=pltpu.SMEM)],
        out_specs=pl.BlockSpec((1, blk, hq), lambda b, i: (b, i, 0)),
        out_shape=jax.ShapeDtypeStruct((b_, lc, hq), F32),
        compiler_params=_params(2),
        name="context_attention",
    )(q, kc, vc, sink)


def _scan8(a, u, reverse):
    row = lax.broadcasted_iota(jnp.int32, a.shape, 0)
    for s in (1, 2, 4):
        shift = (SUBLANES - s) if reverse else s
        a_sh = pltpu.roll(a, shift, axis=0)
        u_sh = pltpu.roll(u, shift, axis=0)
        valid = (row < SUBLANES - s) if reverse else (row >= s)
        u = jnp.where(valid, a * u_sh + u, u)
        a = jnp.where(valid, a * a_sh, a)
    return a, u


def _gelu_tanh(x):
    return 0.5 * x * (1.0 + jnp.tanh(0.7978845608028654 * (x + 0.044715 * (x * x * x))))


def _rglru_kernel(*refs, tile, n_tiles, reverse, combine):
    if combine:
        (x_ref, xp_ref, xn_ref, cw_ref, cb_ref, wbd_ref, bias_ref, lam_ref, h0_ref, hf_ref, gd_ref,
         o_ref, hl_ref, xbuf, a_s, u_s, carry) = refs
    else:
        (x_ref, xp_ref, xn_ref, cw_ref, cb_ref, wbd_ref, bias_ref, lam_ref, h0_ref,
         o_ref, hl_ref, xbuf, a_s, u_s, carry) = refs
    i = pl.program_id(1)
    t = (n_tiles - 1 - i) if reverse else i
    w = x_ref.shape[-1]

    @pl.when(i == 0)
    def _():
        carry[...] = jnp.broadcast_to(h0_ref[0], (SUBLANES, w))

    xbuf[pl.ds(0, HALO), :] = jnp.where(t > 0, xp_ref[0], 0.0)
    xbuf[pl.ds(HALO, tile), :] = x_ref[0]
    xbuf[pl.ds(HALO + tile, HALO), :] = jnp.where(t < n_tiles - 1, xn_ref[0], 0.0)
    xc = cb_ref[...] + sum(cw_ref[pl.ds(k, 1), :] * xbuf[pl.ds(HALO - 1 + k, tile), :] for k in range(D_CONV))

    z = _dot(xc.astype(BF16), wbd_ref[...]) + bias_ref[...]
    r = jax.nn.sigmoid(z[:, :w])
    gi = jax.nn.sigmoid(z[:, w:])
    lam = lam_ref[...]
    log_a = (-D_C * jnp.log(1.0 + jnp.exp(-lam))) * r
    a = jnp.exp(log_a)
    a_s[...] = a
    u_s[...] = jnp.sqrt(1.0 - a * a) * gi * xc

    n_groups = tile // SUBLANES

    def step(g, c):
        gg = (n_groups - 1 - g) if reverse else g
        r0 = pl.multiple_of(gg * SUBLANES, SUBLANES)
        ac, uc = _scan8(a_s[pl.ds(r0, SUBLANES), :], u_s[pl.ds(r0, SUBLANES), :], reverse)
        h = ac * c + uc
        o_ref[0, pl.ds(r0, SUBLANES), :] = h
        last = h[0:1, :] if reverse else h[SUBLANES - 1:SUBLANES, :]
        return jnp.broadcast_to(last, (SUBLANES, w))

    c_fin = lax.fori_loop(0, n_groups, step, carry[...], unroll=4 if n_groups % 4 == 0 else 1)
    carry[...] = c_fin

    if combine:
        o_ref[0] = (hf_ref[0] + o_ref[0]) * _gelu_tanh(gd_ref[0])

    @pl.when(i == n_tiles - 1)
    def _():
        hl_ref[0] = c_fin[0:1, :]


def _rglru_sweep(xd, conv_w, conv_b, wbd, bias, lam, h0, hf=None, gd=None, *, reverse):
    b_, n, w = xd.shape
    combine = hf is not None
    tile = min(n, 512)
    n_tiles = n // tile
    blocks_per_tile = tile // HALO
    n_blocks = n // HALO

    def tmap(b, i):
        return (b, (n_tiles - 1 - i) if reverse else i, 0)

    def pmap(b, i):
        t = (n_tiles - 1 - i) if reverse else i
        return (b, jnp.maximum(t * blocks_per_tile - 1, 0), 0)

    def nmap(b, i):
        t = (n_tiles - 1 - i) if reverse else i
        return (b, jnp.minimum((t + 1) * blocks_per_tile, n_blocks - 1), 0)

    const2 = lambda b, i: (0, 0)
    in_specs = [
        pl.BlockSpec((1, tile, w), tmap),
        pl.BlockSpec((1, HALO, w), pmap),
        pl.BlockSpec((1, HALO, w), nmap),
        pl.BlockSpec((D_CONV, w), const2),
        pl.BlockSpec((1, w), const2),
        pl.BlockSpec((w, 2 * w), const2),
        pl.BlockSpec((1, 2 * w), const2),
        pl.BlockSpec((1, w), const2),
        pl.BlockSpec((1, 1, w), lambda b, i: (b, 0, 0)),
    ]
    args = [xd, xd, xd, conv_w, conv_b, wbd, bias, lam, h0]
    if combine:
        in_specs += [pl.BlockSpec((1, tile, w), tmap), pl.BlockSpec((1, tile, w), tmap)]
        args += [hf, gd]
    return pl.pallas_call(
        functools.partial(_rglru_kernel, tile=tile, n_tiles=n_tiles, reverse=reverse, combine=combine),
        grid=(b_, n_tiles),
        in_specs=in_specs,
        out_specs=[pl.BlockSpec((1, tile, w), tmap), pl.BlockSpec((1, 1, w), lambda b, i: (b, 0, 0))],
        out_shape=[jax.ShapeDtypeStruct((b_, n, w), F32), jax.ShapeDtypeStruct((b_, 1, w), F32)],
        scratch_shapes=[pltpu.VMEM((tile + 2 * HALO, w), F32), pltpu.VMEM((tile, w), F32),
                        pltpu.VMEM((tile, w), F32), pltpu.VMEM((SUBLANES, w), F32)],
        compiler_params=_params(2),
        name="rglru_bwd_combine" if combine else "rglru_fwd",
    )(*args)


def _block_diag(wz):
    eye = jnp.eye(D_BLOCKS, dtype=wz.dtype)
    return jnp.einsum('hij,hg->higj', wz, eye).reshape(D_WIDTH, D_WIDTH)


def rglru_pallas(xd, gd, conv_w, conv_b, w_r, b_r, w_i, b_i, lam, s0):
    cb = conv_b.reshape(1, D_WIDTH)
    outs = []
    for z in range(2):
        wbd = jnp.concatenate([_block_diag(w_r[z]), _block_diag(w_i[z])], axis=1).astype(BF16)
        bias = jnp.concatenate([b_r[z], b_i[z]]).reshape(1, 2 * D_WIDTH)
        outs.append((wbd, bias, lam[z].reshape(1, D_WIDTH)))
    hf, sf = _rglru_sweep(xd, conv_w, cb, *outs[0], s0[0][:, None, :], reverse=False)
    y, sb = _rglru_sweep(xd, conv_w, cb, *outs[1], s0[1][:, None, :], hf, gd, reverse=True)
    return y, (sf[:, 0], sb[:, 0])


AFF_BITS = 31


def _lane_cumsum_exclusive(x01):
    e_, n = x01.shape
    li = lax.broadcasted_iota(jnp.int32, (LANES, LANES), 0)
    lj = lax.broadcasted_iota(jnp.int32, (LANES, LANES), 1)
    strict_upper = (li < lj).astype(BF16)
    ones = jnp.ones((LANES, LANES), BF16)
    base = jnp.zeros((e_, LANES), F32)
    outs = []
    for g in range(n // LANES):
        xg = x01[:, g * LANES:(g + 1) * LANES].astype(BF16)
        outs.append(base + _dot(xg, strict_upper))
        base = base + _dot(xg, ones)
    return jnp.concatenate(outs, axis=1)


def _route_kernel(aff_ref, idx_ref, gate_ref, sp_s, af_s, ps_s, pe_s, *, cap):
    aff = aff_ref[0]
    e_, n = aff.shape
    n_groups = n // LANES
    rb_rows = min(cap, LANES)
    bits = pltpu.bitcast(aff, jnp.int32)
    thr = jnp.zeros((e_, 1), jnp.int32)
    for bit in range(AFF_BITS - 1, -1, -1):
        cand = thr | (1 << bit)
        cnt = jnp.sum((bits >= cand).astype(F32), axis=1, keepdims=True)
        thr = jnp.where(cnt >= cap, cand, thr)
    gt = bits > thr
    eq = bits == thr
    need = cap - jnp.sum(gt.astype(F32), axis=1, keepdims=True)
    sel = gt | (eq & (_lane_cumsum_exclusive(eq.astype(F32)) < need))
    pos = _lane_cumsum_exclusive(sel.astype(F32))
    selpos = jnp.where(sel, pos, -1.0)
    tok_g = lax.broadcasted_iota(jnp.int32, (n, n_groups), 0) // LANES
    grp = lax.broadcasted_iota(jnp.int32, (n, n_groups), 1)
    cnt_g = _dot(sel.astype(BF16), (tok_g == grp).astype(BF16))
    gi = lax.broadcasted_iota(jnp.int32, (n_groups, n_groups), 0)
    gj = lax.broadcasted_iota(jnp.int32, (n_groups, n_groups), 1)
    start_g = _dot(cnt_g, (gi < gj).astype(F32), HI)
    ps_s[...] = start_g
    pe_s[...] = start_g + cnt_g
    for g in range(n_groups):
        sp_s[g] = selpos[:, g * LANES:(g + 1) * LANES]
        af_s[g] = aff[:, g * LANES:(g + 1) * LANES]
    slot0 = lax.broadcasted_iota(jnp.int32, (rb_rows, LANES), 0).astype(F32)
    lane = lax.broadcasted_iota(jnp.int32, (1, LANES), 1).astype(F32)

    def per_block(k, carry):
        e = k // (cap // rb_rows)
        rb = k % (cap // rb_rows)
        first = lax.convert_element_type(rb * rb_rows, F32)
        slot = slot0 + first

        def per_group(g, acc):
            acc_i, acc_g = acc
            hit = sp_s[g, pl.ds(e, 1), :] == slot
            tok = lane + lax.convert_element_type(g * LANES, F32)
            return (acc_i + jnp.where(hit, tok, 0.0), acc_g + jnp.where(hit, af_s[g, pl.ds(e, 1), :], 0.0))

        g_lo = jnp.sum((pe_s[pl.ds(e, 1), :] <= first).astype(F32), axis=1, keepdims=True)[0, 0].astype(jnp.int32)
        g_hi = jnp.sum((ps_s[pl.ds(e, 1), :] < first + rb_rows).astype(F32), axis=1, keepdims=True)[0, 0].astype(jnp.int32)
        zero = jnp.zeros((rb_rows, LANES), F32)
        acc_i, acc_g = lax.fori_loop(g_lo, g_hi, per_group, (zero, zero))
        rows = pl.ds(pl.multiple_of(rb * rb_rows, rb_rows), rb_rows)
        idx_ref[0, e, rows, :] = jnp.sum(acc_i, axis=1, keepdims=True).astype(jnp.int32)
        gate_ref[0, e, rows, :] = jnp.sum(acc_g, axis=1, keepdims=True)
        return carry

    lax.fori_loop(0, e_ * (cap // rb_rows), per_block, 0)


def route(aff):
    b_, e_, n = aff.shape
    cap = max(1, EC_FACTOR * n // N_EXPERTS)
    grp_scr = lambda: pltpu.VMEM((n // LANES, e_, LANES), F32)
    cnt_scr = lambda: pltpu.VMEM((e_, n // LANES), F32)
    idx, gate = pl.pallas_call(
        functools.partial(_route_kernel, cap=cap),
        grid=(b_,),
        in_specs=[pl.BlockSpec((1, e_, n), lambda b: (b, 0, 0))],
        out_specs=[pl.BlockSpec((1, e_, cap, 1), lambda b: (b, 0, 0, 0)),
                   pl.BlockSpec((1, e_, cap, 1), lambda b: (b, 0, 0, 0))],
        out_shape=[jax.ShapeDtypeStruct((b_, e_, cap, 1), jnp.int32), jax.ShapeDtypeStruct((b_, e_, cap, 1), F32)],
        scratch_shapes=[grp_scr(), grp_scr(), cnt_scr(), cnt_scr()],
        compiler_params=_params(1),
        name="route",
    )(aff)
    return idx.reshape(b_, e_, 1, cap), gate


ROW_UNROLL = 8


def _gather_kernel(idx_ref, h_ref, o_ref, *, cap):
    def body(r, carry):
        o_ref[0, 0, pl.ds(r, 1), :] = h_ref[0, pl.ds(idx_ref[0, 0, 0, r], 1), :]
        return carry
    lax.fori_loop(0, cap, body, 0, unroll=ROW_UNROLL)


def gather_rows(h, idx):
    b_, n, d = h.shape
    _, e_, _, cap = idx.shape
    return pl.pallas_call(
        functools.partial(_gather_kernel, cap=cap),
        grid=(b_, e_),
        in_specs=[pl.BlockSpec((1, 1, 1, cap), lambda b, e: (b, e, 0, 0), memory_space=pltpu.SMEM),
                  pl.BlockSpec((1, n, d), lambda b, e: (b, 0, 0), pipeline_mode=pl.Buffered(1))],
        out_specs=pl.BlockSpec((1, 1, cap, d), lambda b, e: (b, e, 0, 0)),
        out_shape=jax.ShapeDtypeStruct((b_, e_, cap, d), h.dtype),
        compiler_params=_params(2),
        name="gather_rows",
    )(idx, h)


def _expert_ffn_kernel(*refs, n_lat, with_ctx):
    if with_ctx:
        (x_ref, g_ref, g2_ref, xc_ref, gc_ref, g2c_ref, w1_ref, w3_ref, w2_ref, o_ref, oc_ref, w1b, w3b, w2b) = refs
    else:
        (x_ref, g_ref, g2_ref, w1_ref, w3_ref, w2_ref, o_ref, w1b, w3b, w2b) = refs
    m = pl.program_id(2)

    @pl.when((pl.program_id(1) == 0) & (m == 0))
    def _():
        w1b[...] = w1_ref[0].astype(BF16)
        w3b[...] = w3_ref[0].astype(BF16)
        w2b[...] = w2_ref[0].astype(BF16)

    def ffn(x):
        hid = (_silu(_dot(x, w1b[...])) * _dot(x, w3b[...])).astype(BF16)
        return _dot(hid, w2b[...])

    def latent():
        o_ref[0, 0] = ffn(_unpack_halves(x_ref[0, 0])) * g_ref[0, 0] * g2_ref[0]

    def latent_and_context():
        rows = x_ref.shape[2]
        y = ffn(jnp.concatenate([_unpack_halves(x_ref[0, 0]), _unpack_halves(xc_ref[0, 0])], axis=0))
        o_ref[0, 0] = y[:rows] * g_ref[0, 0] * g2_ref[0]
        oc_ref[0, 0] = y[rows:] * gc_ref[0, 0] * g2c_ref[0]

    if with_ctx:
        pl.when(m < n_lat - 1)(latent)
        pl.when(m == n_lat - 1)(latent_and_context)
    else:
        latent()


def expert_ffn(xs, gate, g2, w1, w3, w2, layer, ctx=None):
    b_, e_, c_, half = xs.shape
    d = 2 * half
    f = w1.shape[-1]
    tm = min(c_, 512)
    n_lat = c_ // tm
    lat = lambda e, b, m: (b, e, jnp.minimum(m, n_lat - 1), 0)
    vec = pl.BlockSpec((1, 1, d), lambda e, b, m: (b, 0, 0))
    in_specs = [pl.BlockSpec((1, 1, tm, half), lat), pl.BlockSpec((1, 1, tm, 1), lat), vec]
    args = [xs, gate, g2[:, None, :]]
    out_specs = [pl.BlockSpec((1, 1, tm, d), lat)]
    out_shape = [jax.ShapeDtypeStruct((b_, e_, c_, d), F32)]
    if ctx is not None:
        xs_c, gate_c, g2_c = ctx
        cc = xs_c.shape[2]
        one = lambda e, b, m: (b, e, 0, 0)
        in_specs += [pl.BlockSpec((1, 1, cc, half), one), pl.BlockSpec((1, 1, cc, 1), one), vec]
        args += [xs_c, gate_c, g2_c[:, None, :]]
        out_specs.append(pl.BlockSpec((1, 1, cc, d), one))
        out_shape.append(jax.ShapeDtypeStruct((b_, e_, cc, d), F32))
    wspec = lambda r, c: pl.BlockSpec((None, 1, r, c), lambda e, b, m: (layer, e, 0, 0))
    outs = pl.pallas_call(
        functools.partial(_expert_ffn_kernel, n_lat=n_lat, with_ctx=ctx is not None),
        grid=(e_, b_, n_lat),
        scratch_shapes=[pltpu.VMEM((d, f), BF16), pltpu.VMEM((d, f), BF16), pltpu.VMEM((f, d), BF16)],
        in_specs=in_specs + [wspec(d, f), wspec(d, f), wspec(f, d)],
        out_specs=out_specs,
        out_shape=out_shape,
        compiler_params=_params(3),
        name="expert_ffn",
    )(*args, w1, w3, w2)
    return (outs[0], outs[1]) if ctx is not None else (outs[0], None)


def _combine_kernel(idx_ref, y_ref, x1_hbm, o_hbm, acc, sem, *, cap, n_experts):
    b = pl.program_id(0)
    e = pl.program_id(1)

    @pl.when(e == 0)
    def _():
        cp = pltpu.make_async_copy(x1_hbm.at[b], acc, sem.at[0])
        cp.start()
        cp.wait()

    def body(g, carry):
        r0 = g * ROW_UNROLL
        rows = [pl.ds(idx_ref[0, 0, 0, r0 + j], 1) for j in range(ROW_UNROLL)]
        vals = [acc[row, :] + y_ref[0, 0, pl.ds(r0 + j, 1), :] for j, row in enumerate(rows)]
        for row, val in zip(rows, vals):
            acc[row, :] = val
        return carry
    lax.fori_loop(0, cap // ROW_UNROLL, body, 0)

    @pl.when(e == n_experts - 1)
    def _():
        cp = pltpu.make_async_copy(acc, o_hbm.at[b], sem.at[1])
        cp.start()
        cp.wait()


def combine_rows(x1, y, idx):
    b_, n, d = x1.shape
    _, e_, _, cap = idx.shape
    return pl.pallas_call(
        functools.partial(_combine_kernel, cap=cap, n_experts=e_),
        grid=(b_, e_),
        in_specs=[pl.BlockSpec((1, 1, 1, cap), lambda b, e: (b, e, 0, 0), memory_space=pltpu.SMEM),
                  pl.BlockSpec((1, 1, cap, d), lambda b, e: (b, e, 0, 0)),
                  pl.BlockSpec(memory_space=pl.ANY)],
        out_specs=pl.BlockSpec(memory_space=pl.ANY),
        out_shape=jax.ShapeDtypeStruct((b_, n, d), F32),
        scratch_shapes=[pltpu.VMEM((n, d), F32), pltpu.SemaphoreType.DMA((2,))],
        compiler_params=_params(2),
        name="combine_rows",
    )(idx, y, x1)


def moe_residual(lat, ctx, w1, w3, w2, layer):
    routed = []
    for x1, h2, aff, g2 in (lat,) if ctx is None else (lat, ctx):
        idx, gate = route(aff)
        routed.append((x1, idx, gather_rows(h2, idx), gate, g2))
    (x1, idx, xs, gate, g2) = routed[0]
    y, y_c = expert_ffn(xs, gate, g2, w1, w3, w2, layer, ctx=None if ctx is None else routed[1][2:])
    out = combine_rows(x1, y, idx)
    return out, (None if ctx is None else combine_rows(routed[1][0], y_c, routed[1][1]))


def kernel(x, c, ctx, c_ctx, w_mod, b_mod, norm1_w, norm2_w, final_norm_w,
           ev_w_in, ev_w_out, a_conv_w, a_log, a_dt_bias, a_norm_w, b_lb_logits, b_norm_w,
           od_w_in, od_w_out, c_sink, d_conv_w, d_conv_b, d_w_r, d_b_r, d_w_i, d_b_i, d_lambda,
           moe_router, moe_w1, moe_w3, moe_w2):
    b_, n, d = x.shape
    cos, sin = _rope_tables(n)
    lb_all = jnp.cumsum(jax.nn.softmax(b_lb_logits.astype(F32), axis=0), axis=0)
    lb_all = lb_all - lb_all[0:1]
    c_rows = jnp.concatenate([c, c_ctx[None, :], jnp.zeros((-(b_ + 1) % SUBLANES, d), F32)], axis=0)
    mod_all = modulation(c_rows, w_mod, b_mod)
    for l in range(DEPTH):
        last = l == DEPTH - 1
        j = l // 2
        router_t = moe_router[l].T.astype(BF16)
        mod = mod_all[l, :b_]
        mod_c = jnp.broadcast_to(mod_all[l, b_], (b_, 6 * d))
        sh1, sc1, g1, sh2, sc2, g2 = jnp.split(mod, 6, axis=-1)
        csh1, csc1, cg1, csh2, csc2, cg2 = jnp.split(mod_c, 6, axis=-1)
        if l % 2 == 0:
            w_in, w_out = _even_w_in(ev_w_in[j]), ev_w_out[j].astype(BF16)
            zero = jnp.zeros((b_, A_HEADS, A_DK, A_DV), F32)
            pars = (a_conv_w[j], a_log[j], a_dt_bias[j], a_norm_w[j], lb_all[j].reshape(2, HB), b_norm_w[j])
            o_c, st = even_mixer_pallas(in_proj(ctx, norm1_w[l], csh1, csc1, w_in, EVEN_GROUPS), *pars,
                                        ((zero, zero), (zero, zero)))
            o_l, _ = even_mixer_pallas(in_proj(x, norm1_w[l], sh1, sc1, w_in, EVEN_GROUPS), *pars, st)
            parts_l, parts_c = [o_l], [o_c]
        else:
            w_in, w_out = od_w_in[j].astype(BF16), od_w_out[j].astype(BF16)
            qc, kc, vc, xdc, gdc = in_proj(ctx, norm1_w[l], csh1, csc1, w_in, ODD_SIZES)
            ql, kl, vl, xdl, gdl = in_proj(x, norm1_w[l], sh1, sc1, w_in, ODD_SIZES)
            att_l = window_attention_pallas(ql, kl, vl, kc, vc, c_sink[j], cos, sin)
            rg_pars = (d_conv_w[j], d_conv_b[j], d_w_r[j], d_b_r[j], d_w_i[j], d_b_i[j], d_lambda[j])
            zero = jnp.zeros((b_, D_WIDTH), F32)
            rg_c, st = rglru_pallas(xdc, gdc, *rg_pars, (zero, zero))
            rg_l, _ = rglru_pallas(xdl, gdl, *rg_pars, st)
            parts_l = [att_l, rg_l]
            if not last:
                parts_c = [context_attention_pallas(qc, kc, vc, c_sink[j]), rg_c]
        lat = tuple(post_mixer(parts_l, w_out, x, g1, norm2_w[l], sh2, sc2, router_t)) + (g2,)
        con = None if last else tuple(post_mixer(parts_c, w_out, ctx, cg1, norm2_w[l], csh2, csc2, router_t)) + (cg2,)
        x, ctx = moe_residual(lat, con, moe_w1, moe_w3, moe_w2, l)
    return final_norm(x, final_norm_w)
```
